```python
import math
import jax, jax.numpy as jnp
from jax import lax
import numpy as np

D_MODEL = 1024
BATCH = 8
SEQ = 4096
DEPTH = 1

N_HEADS = 8
HEAD_DIM = 128
ATTN_WIDTH = N_HEADS * HEAD_DIM
CONV_WIDTH = 1024
CONV_K = 3
MOBA_BLOCK = 256
MOBA_TOPK = 3
Q_CHUNK = 8
N_BUCKETS = 32
MAX_DISTANCE = 128
EPS = 1e-6
IN_WIDTHS = (ATTN_WIDTH, ATTN_WIDTH, ATTN_WIDTH, ATTN_WIDTH,
             CONV_WIDTH, CONV_WIDTH, CONV_WIDTH, CONV_WIDTH,
             D_MODEL, D_MODEL)
IN_COLS = sum(IN_WIDTHS)
IN_OFFSETS = tuple(int(o) for o in np.cumsum(IN_WIDTHS)[:-1])

kernel_name = "moba_shortconv_gated_hybrid_layer"


def rmsnorm(x, g):
    x32 = x.astype(jnp.float32)
    y = x32 * lax.rsqrt(jnp.mean(x32 * x32, axis=-1, keepdims=True) + EPS)
    return (y * g.astype(jnp.float32)).astype(x.dtype)


def t5_bucket(dist):
    n = jnp.maximum(dist, 0)
    max_exact = N_BUCKETS // 2
    nf = jnp.maximum(n, 1).astype(jnp.float32)
    large = max_exact + (jnp.log(nf / max_exact) / math.log(MAX_DISTANCE / max_exact)
                         * (N_BUCKETS - max_exact)).astype(jnp.int32)
    large = jnp.minimum(large, N_BUCKETS - 1)
    return jnp.where(n < max_exact, n, large)


def moba_attention(q, k, v, rel_bias):
    bsz, n_h, seq, hd = q.shape
    s_pad = -(-seq // MOBA_BLOCK) * MOBA_BLOCK
    padw = ((0, 0), (0, 0), (0, s_pad - seq), (0, 0))
    q32 = jnp.pad(q.astype(jnp.float32), padw) * (hd ** -0.5)
    nb = s_pad // MOBA_BLOCK
    kb = jnp.pad(k.astype(jnp.float32), padw).reshape(bsz, n_h, nb, MOBA_BLOCK, hd)
    vb = jnp.pad(v.astype(jnp.float32), padw).reshape(bsz, n_h, nb, MOBA_BLOCK, hd)
    bias_hb = rel_bias.astype(jnp.float32).T
    n_sel = min(MOBA_TOPK, nb - 1)
    n_chunks = s_pad // Q_CHUNK
    q_ch = q32.reshape(bsz, n_h, n_chunks, Q_CHUNK, hd).transpose(2, 0, 1, 3, 4)
    starts = jnp.arange(n_chunks, dtype=jnp.int32) * Q_CHUNK

    if n_sel > 0:
        k_mean = kb.mean(axis=3)
        scores = jnp.einsum('bhsd,bhnd->bhsn', q32, k_mean)
        q_blk = jnp.arange(s_pad) // MOBA_BLOCK
        fully_past = jnp.arange(nb)[None, :] < q_blk[:, None]
        scores = jnp.where(fully_past, scores, -jnp.inf)
        _, sel = lax.top_k(scores, n_sel)
        sel = sel.astype(jnp.int32)
    else:
        sel = jnp.zeros((bsz, n_h, s_pad, 0), jnp.int32)
    sel_ch = sel.reshape(bsz, n_h, n_chunks, Q_CHUNK, n_sel).transpose(2, 0, 1, 3, 4)

    b_idx = jnp.arange(bsz)[:, None, None, None]
    h_idx = jnp.arange(n_h)[None, :, None, None]
    offs = jnp.arange(MOBA_BLOCK, dtype=jnp.int32)

    def chunk(args):
        qc, selc, c0 = args
        qpos = c0 + jnp.arange(Q_CHUNK, dtype=jnp.int32)
        own = c0 // MOBA_BLOCK
        k_own = kb[:, :, own]
        v_own = vb[:, :, own]
        dist_own = qpos[:, None] - (own * MOBA_BLOCK + offs)[None, :]
        lg_own = jnp.einsum('bhqd,bhkd->bhqk', qc, k_own) + bias_hb[:, t5_bucket(dist_own)]
        lg_own = jnp.where(dist_own >= 0, lg_own, -jnp.inf)
        k_sel = kb[b_idx, h_idx, selc]
        v_sel = vb[b_idx, h_idx, selc]
        dist_sel = qpos[:, None, None] - (selc[..., None] * MOBA_BLOCK + offs)
        lg_sel = (jnp.einsum('bhqd,bhqnkd->bhqnk', qc, k_sel)
                  + bias_hb[h_idx[..., None], t5_bucket(dist_sel)])
        lg_sel = jnp.where((selc < own)[..., None], lg_sel, -jnp.inf)
        logits = jnp.concatenate(
            [lg_sel.reshape(bsz, n_h, Q_CHUNK, n_sel * MOBA_BLOCK), lg_own], axis=-1)
        p = jax.nn.softmax(logits, axis=-1)
        p_sel = p[..., :n_sel * MOBA_BLOCK].reshape(bsz, n_h, Q_CHUNK, n_sel, MOBA_BLOCK)
        p_own = p[..., n_sel * MOBA_BLOCK:]
        return (jnp.einsum('bhqnk,bhqnkd->bhqd', p_sel, v_sel)
                + jnp.einsum('bhqk,bhkd->bhqd', p_own, v_own))

    out = lax.map(chunk, (q_ch, sel_ch, starts))
    out = out.transpose(1, 2, 0, 3, 4).reshape(bsz, n_h, s_pad, hd)[:, :, :seq]
    return out.astype(q.dtype)


def short_conv(cb, cc, cx, conv_w):
    u = cc * cx
    y = lax.conv_general_dilated(
        u, conv_w[:, None, :].astype(u.dtype), window_strides=(1,),
        padding=[(CONV_K - 1, 0)], dimension_numbers=('NWC', 'WIO', 'NWC'),
        feature_group_count=CONV_WIDTH)
    return cb * y


def split_heads(t):
    b, s, _ = t.shape
    return t.reshape(b, s, N_HEADS, HEAD_DIM).transpose(0, 2, 1, 3)


def hybrid_layer(x, c, norm_g, w_ada, b_ada, w_in, conv_w, w_o_attn, w_o_conv, w_out, rel_bias):
    bsz, seq, _ = x.shape
    mod = jax.nn.silu(c) @ w_ada + b_ada
    shift, scale, gate = jnp.split(mod, 3, axis=-1)
    h = rmsnorm(x, norm_g) * (1 + scale[:, None, :]) + shift[:, None, :]
    proj = h @ w_in
    q, k, v, g_attn, cb, cc, cx, g_conv, m_attn, m_conv = jnp.split(proj, IN_OFFSETS, axis=-1)
    attn = moba_attention(split_heads(q), split_heads(k), split_heads(v), rel_bias)
    attn = attn.transpose(0, 2, 1, 3).reshape(bsz, seq, ATTN_WIDTH)
    y_attn = (attn * jax.nn.silu(g_attn)) @ w_o_attn
    y_conv = (short_conv(cb, cc, cx, conv_w) * jax.nn.silu(g_conv)) @ w_o_conv
    merged = jax.nn.sigmoid(m_attn) * y_attn + jax.nn.sigmoid(m_conv) * y_conv
    return x + gate[:, None, :] * (merged @ w_out)


def setup_inputs(seed: int = 0) -> dict:
    key = jax.random.key(seed)
    ks = jax.random.split(key, 13)
    nrm = jax.random.normal
    f32 = jnp.float32
    return {
        "x": nrm(ks[0], (BATCH, SEQ, D_MODEL), f32),
        "c": nrm(ks[1], (BATCH, D_MODEL), f32),
        "norm_g": 1.0 + 0.05 * nrm(ks[2], (DEPTH, D_MODEL), f32),
        "w_ada": 0.5 * D_MODEL ** -0.5 * nrm(ks[3], (DEPTH, D_MODEL, 3 * D_MODEL), f32),
        "b_ada": 0.1 * nrm(ks[4], (DEPTH, 3 * D_MODEL), f32),
        "w_in": D_MODEL ** -0.5 * nrm(ks[5], (DEPTH, D_MODEL, IN_COLS), f32),
        "conv_w": CONV_K ** -0.5 * nrm(ks[6], (DEPTH, CONV_K, CONV_WIDTH), f32),
        "w_o_attn": ATTN_WIDTH ** -0.5 * nrm(ks[7], (DEPTH, ATTN_WIDTH, D_MODEL), f32),
        "w_o_conv": CONV_WIDTH ** -0.5 * nrm(ks[8], (DEPTH, CONV_WIDTH, D_MODEL), f32),
        "w_out": D_MODEL ** -0.5 * nrm(ks[9], (DEPTH, D_MODEL, D_MODEL), f32),
        "rel_bias": 0.5 * nrm(ks[10], (N_BUCKETS, N_HEADS), f32),
        "final_g": 1.0 + 0.05 * nrm(ks[11], (D_MODEL,), f32),
    }


def reference(x, c, norm_g, w_ada, b_ada, w_in, conv_w, w_o_attn, w_o_conv, w_out, rel_bias, final_g):
    for layer in range(DEPTH):
        x = hybrid_layer(x, c, norm_g[layer], w_ada[layer], b_ada[layer], w_in[layer],
                         conv_w[layer], w_o_attn[layer], w_o_conv[layer], w_out[layer], rel_bias)
    return rmsnorm(x, final_g)
```

```python
import functools
import math

import jax
import jax.numpy as jnp
from jax import lax
from jax.experimental import pallas as pl
from jax.experimental.pallas import tpu as pltpu

N_HEADS = 8
HEAD_DIM = 128
CONV_K = 3
MOBA_BLOCK = 256
MOBA_TOPK = 3
N_BUCKETS = 32
MAX_DISTANCE = 128
EPS = 1e-6
N_PROJ = 10

LANES = 128
NEG = -1e30
VMEM_LIMIT = 56 * 1024 * 1024

F32 = jnp.float32
BF16 = jnp.bfloat16


def _silu(v):
    return v * jax.nn.sigmoid(v)


def _mod_kernel(c_ref, w_ref, b_ref, o_ref):
    a = _silu(c_ref[...])
    o_ref[...] = jnp.dot(a, w_ref[...], precision=lax.Precision.HIGHEST,
                         preferred_element_type=F32) + b_ref[...]


def _modulation(c, w_ada, b_ada):
    bsz, d = c.shape
    n = w_ada.shape[1]
    tn = d
    return pl.pallas_call(
        _mod_kernel,
        grid=(n // tn,),
        in_specs=[pl.BlockSpec((bsz, d), lambda j: (0, 0)),
                  pl.BlockSpec((d, tn), lambda j: (0, j)),
                  pl.BlockSpec((1, tn), lambda j: (0, j))],
        out_specs=pl.BlockSpec((bsz, tn), lambda j: (0, j)),
        out_shape=jax.ShapeDtypeStruct((bsz, n), F32),
        compiler_params=pltpu.CompilerParams(vmem_limit_bytes=VMEM_LIMIT),
    )(c, w_ada, b_ada.reshape(1, n))


def _inproj_kernel(x_ref, mod_ref, g_ref, w_ref, o_ref, h_ref, *, q_scale, row_chunk):
    j = pl.program_id(2)
    ts = x_ref.shape[1]

    @pl.when(j == 0)
    def _():
        shift = mod_ref[0, 0:1, :]
        scale1 = 1.0 + mod_ref[0, 1:2, :]
        g = g_ref[...]

        def body(r, carry):
            rows = pl.ds(pl.multiple_of(r * row_chunk, row_chunk), row_chunk)
            xr = x_ref[0, rows, :]
            inv = lax.rsqrt(jnp.mean(xr * xr, axis=-1, keepdims=True) + EPS)
            h_ref[rows, :] = ((xr * inv * g) * scale1 + shift).astype(BF16)
            return carry

        lax.fori_loop(0, ts // row_chunk, body, 0)

    acc = jnp.dot(h_ref[...], w_ref[...], preferred_element_type=F32)
    acc = acc * jnp.where(j == 0, q_scale, 1.0).astype(F32)
    o_ref[0] = acc.astype(o_ref.dtype)


def _in_projection(x, mod3, norm_g, w_in_bf16, ts=1024, row_chunk=128):
    bsz, seq, d = x.shape
    n = w_in_bf16.shape[1]
    tn = d
    kern = functools.partial(_inproj_kernel, q_scale=HEAD_DIM ** -0.5, row_chunk=row_chunk)
    return pl.pallas_call(
        kern,
        grid=(bsz, seq // ts, n // tn),
        in_specs=[pl.BlockSpec((1, ts, d), lambda b, s, j: (b, s, 0)),
                  pl.BlockSpec((1, 3, d), lambda b, s, j: (b, 0, 0)),
                  pl.BlockSpec((1, d), lambda b, s, j: (0, 0)),
                  pl.BlockSpec((d, tn), lambda b, s, j: (0, j))],
        out_specs=pl.BlockSpec((1, ts, tn), lambda b, s, j: (b, s, j)),
        out_shape=jax.ShapeDtypeStruct((bsz, seq, n), BF16),
        scratch_shapes=[pltpu.VMEM((ts, d), BF16)],
        compiler_params=pltpu.CompilerParams(
            dimension_semantics=("arbitrary", "arbitrary", "arbitrary"),
            vmem_limit_bytes=VMEM_LIMIT),
    )(x, mod3, norm_g.reshape(1, d), w_in_bf16)


def _t5_bucket(dist):
    n = jnp.maximum(dist, 0)
    max_exact = N_BUCKETS // 2
    nf = jnp.maximum(n, 1).astype(F32)
    large = max_exact + (jnp.log(nf / max_exact) / math.log(MAX_DISTANCE / max_exact)
                         * (N_BUCKETS - max_exact)).astype(jnp.int32)
    large = jnp.minimum(large, N_BUCKETS - 1)
    return jnp.where(n < max_exact, n, large)


def _table_kernel(rb_ref, o_ref):
    h = pl.program_id(0)
    blk = o_ref.shape[2]
    r = lax.broadcasted_iota(jnp.int32, (blk, blk), 0)
    c = lax.broadcasted_iota(jnp.int32, (blk, blk), 1)
    for t in range(3):
        dist = r - c + t * blk
        bucket = _t5_bucket(dist)
        bias = jnp.zeros((blk, blk), F32)
        for b in range(N_BUCKETS):
            bias = jnp.where(bucket == b, rb_ref[b, h], bias)
        if t == 0:
            bias = jnp.where(dist >= 0, bias, NEG)
        o_ref[0, t] = bias


def _bias_tables(rel_bias):
    return pl.pallas_call(
        _table_kernel,
        grid=(N_HEADS,),
        in_specs=[pl.BlockSpec(memory_space=pltpu.SMEM)],
        out_specs=pl.BlockSpec((1, 3, MOBA_BLOCK, MOBA_BLOCK), lambda h: (h, 0, 0, 0)),
        out_shape=jax.ShapeDtypeStruct((N_HEADS, 3, MOBA_BLOCK, MOBA_BLOCK), F32),
        compiler_params=pltpu.CompilerParams(vmem_limit_bytes=VMEM_LIMIT),
    )(rel_bias)


def _attn_kernel(q_ref, k_ref, v_ref, tab_ref, o_ref, kaug_ref, kmean_ref, *, nb):
    i = pl.program_id(2)
    blk = MOBA_BLOCK
    hd = HEAD_DIM

    @pl.when(i == 0)
    def _():
        lane = lax.broadcasted_iota(jnp.int32, (blk, LANES), 1)
        for n in range(nb):
            kb = k_ref[0, n * blk:(n + 1) * blk, :]
            kaug_ref[n * blk:(n + 1) * blk, 0:hd] = kb
            kaug_ref[n * blk:(n + 1) * blk, hd:hd + LANES] = jnp.where(lane == n, 1.0, 0.0).astype(BF16)
            kmean_ref[n:n + 1, :] = jnp.mean(kb.astype(F32), axis=0, keepdims=True)

    q = q_ref[0]

    sc = lax.dot_general(kmean_ref[...], q.astype(F32), (((1,), (1,)), ((), ())),
                         precision=lax.Precision.HIGHEST, preferred_element_type=F32)
    n_iota = lax.broadcasted_iota(jnp.int32, (nb, blk), 0)
    rank = jnp.zeros((nb, blk), jnp.int32)
    for m in range(nb):
        row = sc[m:m + 1, :]
        beats = (row > sc) | ((row == sc) & (m < n_iota))
        rank = rank + jnp.where(beats & (m < i), 1, 0)
    keep = ((n_iota < i) & (rank < MOBA_TOPK)) | (n_iota == i)
    neg_t = jnp.where(keep, 0.0, NEG).astype(F32)
    neg_t = jnp.concatenate([neg_t, jnp.zeros((LANES - nb, blk), F32)], axis=0)
    neg_c = neg_t.T
    q_aug = jnp.concatenate([q, neg_c.astype(BF16)], axis=1)

    def logits(j, kind):
        rows = pl.ds(pl.multiple_of(j * blk, blk), blk)
        s = lax.dot_general(q_aug, kaug_ref[rows, :], (((1,), (1,)), ((), ())),
                            preferred_element_type=F32)
        return s + tab_ref[0, kind], rows

    s, rows = logits(i, 0)
    m0 = jnp.max(s, axis=-1, keepdims=True)
    p = jnp.exp(s - m0)
    l0 = jnp.sum(p, axis=-1, keepdims=True)
    acc0 = jnp.dot(p.astype(BF16), v_ref[0, rows, :], preferred_element_type=F32)

    def body(j, carry):
        m_prev, l_prev, acc = carry
        s, rows = logits(j, jnp.where(j == i - 1, 1, 2))
        m_new = jnp.maximum(m_prev, jnp.max(s, axis=-1, keepdims=True))
        alpha = jnp.exp(m_prev - m_new)
        p = jnp.exp(s - m_new)
        l_new = alpha * l_prev + jnp.sum(p, axis=-1, keepdims=True)
        acc = alpha * acc + jnp.dot(p.astype(BF16), v_ref[0, rows, :], preferred_element_type=F32)
        return m_new, l_new, acc

    _, l_fin, acc = lax.fori_loop(0, i, body, (m0, l0, acc0))
    o_ref[0] = (acc / l_fin).astype(o_ref.dtype)


def _moba_attention(proj, tables, d_model):
    bsz, seq, _ = proj.shape
    nb = seq // MOBA_BLOCK
    nh = d_model // HEAD_DIM
    kern = functools.partial(_attn_kernel, nb=nb)
    return pl.pallas_call(
        kern,
        grid=(bsz, nh, nb),
        in_specs=[pl.BlockSpec((1, MOBA_BLOCK, HEAD_DIM), lambda b, h, i: (b, i, h)),
                  pl.BlockSpec((1, seq, HEAD_DIM), lambda b, h, i: (b, 0, nh + h)),
                  pl.BlockSpec((1, seq, HEAD_DIM), lambda b, h, i: (b, 0, 2 * nh + h)),
                  pl.BlockSpec((1, 3, MOBA_BLOCK, MOBA_BLOCK), lambda b, h, i: (h, 0, 0, 0))],
        out_specs=pl.BlockSpec((1, MOBA_BLOCK, HEAD_DIM), lambda b, h, i: (b, i, h)),
        out_shape=jax.ShapeDtypeStruct((bsz, seq, d_model), BF16),
        scratch_shapes=[pltpu.VMEM((seq, HEAD_DIM + LANES), BF16),
                        pltpu.VMEM((nb, HEAD_DIM), F32)],
        compiler_params=pltpu.CompilerParams(
            dimension_semantics=("arbitrary", "arbitrary", "arbitrary"),
            vmem_limit_bytes=VMEM_LIMIT),
    )(proj, proj, proj, tables)


def _out_kernel(attn_ref, ga_ref, cb_ref, cc_ref, cx_ref, gc_ref, ma_ref, mc_ref,
                ccp_ref, cxp_ref, x_ref, mod_ref, cw_ref, woa_ref, woc_ref, wout_ref, fg_ref,
                o_ref):
    s = pl.program_id(1)
    ts = x_ref.shape[1]
    halo = ccp_ref.shape[1]

    u = cc_ref[0].astype(F32) * cx_ref[0].astype(F32)
    up = ccp_ref[0].astype(F32) * cxp_ref[0].astype(F32)
    up = up * jnp.where(s > 0, 1.0, 0.0).astype(F32)
    p1 = up[halo - 1:halo, :]
    p2 = up[halo - 2:halo - 1, :]
    row = lax.broadcasted_iota(jnp.int32, u.shape, 0)
    u1 = jnp.where(row == 0, p1, pltpu.roll(u, 1, 0))
    u2 = jnp.where(row == 0, p2, jnp.where(row == 1, p1, pltpu.roll(u, 2, 0)))
    y = cw_ref[0:1, :] * u2 + cw_ref[1:2, :] * u1 + cw_ref[2:3, :] * u
    z_conv = cb_ref[0].astype(F32) * y * _silu(gc_ref[0].astype(F32))
    y_conv = jnp.dot(z_conv.astype(BF16), woc_ref[...], preferred_element_type=F32)

    z_attn = attn_ref[0].astype(F32) * _silu(ga_ref[0].astype(F32))
    y_attn = jnp.dot(z_attn.astype(BF16), woa_ref[...], preferred_element_type=F32)

    merged = (jax.nn.sigmoid(ma_ref[0].astype(F32)) * y_attn
              + jax.nn.sigmoid(mc_ref[0].astype(F32)) * y_conv)
    branch = jnp.dot(merged.astype(BF16), wout_ref[...], preferred_element_type=F32)
    res = x_ref[0] + mod_ref[0, 2:3, :] * branch
    inv = lax.rsqrt(jnp.mean(res * res, axis=-1, keepdims=True) + EPS)
    o_ref[0] = res * inv * fg_ref[...]


def _output_stage(attn, proj, x, mod3, conv_w, woa, woc, wout, final_g, ts=512, halo=8):
    bsz, seq, d = x.shape
    tpb = ts // halo

    def col(k):
        return pl.BlockSpec((1, ts, d), lambda b, s: (b, s, k))

    def prev(k):
        return pl.BlockSpec((1, halo, d), lambda b, s: (b, jnp.maximum(s * tpb - 1, 0), k))

    def whole(shape):
        return pl.BlockSpec(shape, lambda b, s: (0,) * len(shape))

    return pl.pallas_call(
        _out_kernel,
        grid=(bsz, seq // ts),
        in_specs=[col(0), col(3), col(4), col(5), col(6), col(7), col(8), col(9),
                  prev(5), prev(6),
                  col(0), pl.BlockSpec((1, 3, d), lambda b, s: (b, 0, 0)),
                  whole((CONV_K, d)), whole((d, d)), whole((d, d)), whole((d, d)), whole((1, d))],
        out_specs=col(0),
        out_shape=jax.ShapeDtypeStruct((bsz, seq, d), F32),
        compiler_params=pltpu.CompilerParams(
            dimension_semantics=("arbitrary", "arbitrary"),
            vmem_limit_bytes=VMEM_LIMIT),
    )(attn, proj, proj, proj, proj, proj, proj, proj, proj, proj,
      x, mod3, conv_w, woa, woc, wout, final_g.reshape(1, d))


def kernel(x, c, norm_g, w_ada, b_ada, w_in, conv_w, w_o_attn, w_o_conv, w_out, rel_bias, final_g):
    bsz, seq, d = x.shape
    depth = norm_g.shape[0]
    assert depth == 1, "the fused output stage applies the final RMSNorm after the single layer"
    assert d == N_HEADS * HEAD_DIM and w_in.shape[2] == N_PROJ * d and seq % MOBA_BLOCK == 0
    tables = _bias_tables(rel_bias)
    mod3 = _modulation(c, w_ada[0], b_ada[0]).reshape(bsz, 3, d)
    proj = _in_projection(x, mod3, norm_g[0], w_in[0].astype(BF16))
    attn = _moba_attention(proj, tables, d)
    return _output_stage(attn, proj, x, mod3, conv_w[0], w_o_attn[0].astype(BF16),
                         w_o_conv[0].astype(BF16), w_out[0].astype(BF16), final_g)
```

```python
import functools
import math

import jax
import jax.numpy as jnp
from jax import lax
from jax.experimental import pallas as pl
from jax.experimental.pallas import tpu as pltpu

N_HEADS = 8
HEAD_DIM = 128
CONV_K = 3
MOBA_BLOCK = 256
MOBA_TOPK = 3
N_BUCKETS = 32
MAX_DISTANCE = 128
EPS = 1e-6
N_PROJ = 10

LANES = 128
NEG = -1e30
VMEM_LIMIT = 56 * 1024 * 1024

F32 = jnp.float32
BF16 = jnp.bfloat16


def _silu(v):
    return v * jax.nn.sigmoid(v)


def _mod_kernel(c_ref, w_ref, b_ref, o_ref):
    a = _silu(c_ref[...])
    o_ref[...] = jnp.dot(a, w_ref[...], precision=lax.Precision.HIGHEST,
                         preferred_element_type=F32) + b_ref[...]


def _modulation(c, w_ada, b_ada):
    bsz, d = c.shape
    n = w_ada.shape[1]
    tn = d
    return pl.pallas_call(
        _mod_kernel,
        grid=(n // tn,),
        in_specs=[pl.BlockSpec((bsz, d), lambda j: (0, 0)),
                  pl.BlockSpec((d, tn), lambda j: (0, j)),
                  pl.BlockSpec((1, tn), lambda j: (0, j))],
        out_specs=pl.BlockSpec((bsz, tn), lambda j: (0, j)),
        out_shape=jax.ShapeDtypeStruct((bsz, n), F32),
        compiler_params=pltpu.CompilerParams(vmem_limit_bytes=VMEM_LIMIT),
    )(c, w_ada, b_ada.reshape(1, n))


def _inproj_kernel(x_ref, mod_ref, g_ref, w_ref, o_ref, h_ref, *, q_scale, row_chunk):
    j = pl.program_id(2)
    ts = x_ref.shape[1]

    @pl.when(j == 0)
    def _():
        shift = mod_ref[0, 0:1, :]
        scale1 = 1.0 + mod_ref[0, 1:2, :]
        g = g_ref[...]

        def body(r, carry):
            rows = pl.ds(pl.multiple_of(r * row_chunk, row_chunk), row_chunk)
            xr = x_ref[0, rows, :]
            inv = lax.rsqrt(jnp.mean(xr * xr, axis=-1, keepdims=True) + EPS)
            h_ref[rows, :] = ((xr * inv * g) * scale1 + shift).astype(BF16)
            return carry

        lax.fori_loop(0, ts // row_chunk, body, 0)

    acc = jnp.dot(h_ref[...], w_ref[...], preferred_element_type=F32)
    acc = acc * jnp.where(j == 0, q_scale, 1.0).astype(F32)
    o_ref[0] = acc.astype(o_ref.dtype)


def _in_projection(x, mod3, norm_g, w_in_bf16, ts=1024, row_chunk=128):
    bsz, seq, d = x.shape
    n = w_in_bf16.shape[1]
    tn = d
    kern = functools.partial(_inproj_kernel, q_scale=HEAD_DIM ** -0.5, row_chunk=row_chunk)
    return pl.pallas_call(
        kern,
        grid=(bsz, seq // ts, n // tn),
        in_specs=[pl.BlockSpec((1, ts, d), lambda b, s, j: (b, s, 0)),
                  pl.BlockSpec((1, 3, d), lambda b, s, j: (b, 0, 0)),
                  pl.BlockSpec((1, d), lambda b, s, j: (0, 0)),
                  pl.BlockSpec((d, tn), lambda b, s, j: (0, j))],
        out_specs=pl.BlockSpec((1, ts, tn), lambda b, s, j: (b, s, j)),
        out_shape=jax.ShapeDtypeStruct((bsz, seq, n), BF16),
        scratch_shapes=[pltpu.VMEM((ts, d), BF16)],
        compiler_params=pltpu.CompilerParams(
            dimension_semantics=("arbitrary", "arbitrary", "arbitrary"),
            vmem_limit_bytes=VMEM_LIMIT),
    )(x, mod3, norm_g.reshape(1, d), w_in_bf16)


def _t5_bucket(dist):
    n = jnp.maximum(dist, 0)
    max_exact = N_BUCKETS // 2
    nf = jnp.maximum(n, 1).astype(F32)
    large = max_exact + (jnp.log(nf / max_exact) / math.log(MAX_DISTANCE / max_exact)
                         * (N_BUCKETS - max_exact)).astype(jnp.int32)
    large = jnp.minimum(large, N_BUCKETS - 1)
    return jnp.where(n < max_exact, n, large)


def _table_kernel(rb_ref, o_ref):
    h = pl.program_id(0)
    blk = o_ref.shape[2]
    r = lax.broadcasted_iota(jnp.int32, (blk, blk), 0)
    c = lax.broadcasted_iota(jnp.int32, (blk, blk), 1)
    for t in range(2):
        dist = r - c + t * blk
        bucket = _t5_bucket(dist)
        bias = jnp.zeros((blk, blk), F32)
        for b in range(N_BUCKETS):
            bias = jnp.where(bucket == b, rb_ref[b, h], bias)
        if t == 0:
            bias = jnp.where(dist >= 0, bias, NEG)
        o_ref[0, t] = bias
    o_ref[0, 2] = jnp.full((blk, blk), NEG, F32)


def _bias_tables(rel_bias):
    return pl.pallas_call(
        _table_kernel,
        grid=(N_HEADS,),
        in_specs=[pl.BlockSpec(memory_space=pltpu.SMEM)],
        out_specs=pl.BlockSpec((1, 3, MOBA_BLOCK, MOBA_BLOCK), lambda h: (h, 0, 0, 0)),
        out_shape=jax.ShapeDtypeStruct((N_HEADS, 3, MOBA_BLOCK, MOBA_BLOCK), F32),
        compiler_params=pltpu.CompilerParams(vmem_limit_bytes=VMEM_LIMIT),
    )(rel_bias)


FAR_GROUP = 4


def _attn_kernel(rb_ref, q_ref, k_ref, v_ref, tab_ref, o_ref, kaug_ref, kmean_ref, *, nb):
    h = pl.program_id(1)
    i = pl.program_id(2)
    blk = MOBA_BLOCK
    hd = HEAD_DIM

    @pl.when(i == 0)
    def _():
        lane = lax.broadcasted_iota(jnp.int32, (blk, LANES), 1)
        for n in range(nb):
            kb = k_ref[0, n * blk:(n + 1) * blk, :]
            kaug_ref[n * blk:(n + 1) * blk, 0:hd] = kb
            onehot = jnp.where((lane == n) | (lane == nb + n), 1.0, 0.0)
            kaug_ref[n * blk:(n + 1) * blk, hd:hd + LANES] = onehot.astype(BF16)
            kmean_ref[n:n + 1, :] = jnp.mean(kb.astype(F32), axis=0, keepdims=True)

    q = q_ref[0]

    sc = lax.dot_general(kmean_ref[...], q.astype(F32), (((1,), (1,)), ((), ())),
                         precision=lax.Precision.HIGHEST, preferred_element_type=F32)
    n_iota = lax.broadcasted_iota(jnp.int32, (nb, blk), 0)
    rank = jnp.zeros((nb, blk), jnp.int32)
    for m in range(nb):
        row = sc[m:m + 1, :]
        beats = (row > sc) | ((row == sc) & (m < n_iota))
        rank = rank + jnp.where(beats & (m < i), 1, 0)
    chosen = (n_iota < i) & (rank < MOBA_TOPK)

    b_far = jnp.full((nb, blk), rb_ref[N_BUCKETS - 1, h], F32)
    b_hi = b_far.astype(BF16).astype(F32)
    far = chosen & (n_iota < i - 1)
    far_t = jnp.concatenate([jnp.where(far, b_hi, NEG), jnp.where(far, b_far - b_hi, 0.0),
                             jnp.zeros((LANES - 2 * nb, blk), F32)], axis=0)
    near_t = jnp.concatenate([jnp.where((n_iota == i - 1) & jnp.logical_not(chosen), NEG, 0.0),
                              jnp.zeros((LANES - nb, blk), F32)], axis=0)
    q_far = jnp.concatenate([q, far_t.T.astype(BF16)], axis=1)
    q_near = jnp.concatenate([q, near_t.T.astype(BF16)], axis=1)

    nt = (((1,), (1,)), ((), ()))
    own_rows = pl.ds(pl.multiple_of(i * blk, blk), blk)
    prev_rows = pl.ds(pl.multiple_of(jnp.maximum(i - 1, 0) * blk, blk), blk)
    prev_kind = jnp.where(i == 0, 2, 1)

    def attend(n_far):
        s_own = lax.dot_general(q_near, kaug_ref[own_rows, :], nt,
                                preferred_element_type=F32) + tab_ref[0, 0]
        s_prev = lax.dot_general(q_near, kaug_ref[prev_rows, :], nt,
                                 preferred_element_type=F32) + tab_ref[0, prev_kind]
        mx = jnp.maximum(jnp.max(s_own, axis=-1, keepdims=True),
                         jnp.max(s_prev, axis=-1, keepdims=True))
        if n_far:
            s_far = lax.dot_general(q_far, kaug_ref[0:n_far * blk, :], nt, preferred_element_type=F32)
            mx = jnp.maximum(mx, jnp.max(s_far, axis=-1, keepdims=True))
        p_own = jnp.exp(s_own - mx)
        p_prev = jnp.exp(s_prev - mx)
        den = jnp.sum(p_own, axis=-1, keepdims=True) + jnp.sum(p_prev, axis=-1, keepdims=True)
        acc = (jnp.dot(p_own.astype(BF16), v_ref[0, own_rows, :], preferred_element_type=F32)
               + jnp.dot(p_prev.astype(BF16), v_ref[0, prev_rows, :], preferred_element_type=F32))
        if n_far:
            p_far = jnp.exp(s_far - mx)
            den = den + jnp.sum(p_far, axis=-1, keepdims=True)
            acc = acc + jnp.dot(p_far.astype(BF16), v_ref[0, 0:n_far * blk, :],
                                preferred_element_type=F32)
        o_ref[0] = (acc / den).astype(o_ref.dtype)

    n_groups = (i + FAR_GROUP - 2) // FAR_GROUP
    for g in range(nb // FAR_GROUP + 1):
        pl.when(n_groups == g)(functools.partial(attend, g * FAR_GROUP))


def _moba_attention(proj, tables, rel_bias, d_model):
    bsz, seq, _ = proj.shape
    nb = seq // MOBA_BLOCK
    nh = d_model // HEAD_DIM
    assert MAX_DISTANCE <= MOBA_BLOCK + 1 and nb % FAR_GROUP == 0 and 2 * nb <= LANES
    kern = functools.partial(_attn_kernel, nb=nb)
    return pl.pallas_call(
        kern,
        grid=(bsz, nh, nb),
        in_specs=[pl.BlockSpec(memory_space=pltpu.SMEM),
                  pl.BlockSpec((1, MOBA_BLOCK, HEAD_DIM), lambda b, h, i: (b, i, h)),
                  pl.BlockSpec((1, seq, HEAD_DIM), lambda b, h, i: (b, 0, nh + h)),
                  pl.BlockSpec((1, seq, HEAD_DIM), lambda b, h, i: (b, 0, 2 * nh + h)),
                  pl.BlockSpec((1, 3, MOBA_BLOCK, MOBA_BLOCK), lambda b, h, i: (h, 0, 0, 0))],
        out_specs=pl.BlockSpec((1, MOBA_BLOCK, HEAD_DIM), lambda b, h, i: (b, i, h)),
        out_shape=jax.ShapeDtypeStruct((bsz, seq, d_model), BF16),
        scratch_shapes=[pltpu.VMEM((seq, HEAD_DIM + LANES), BF16),
                        pltpu.VMEM((nb, HEAD_DIM), F32)],
        compiler_params=pltpu.CompilerParams(
            dimension_semantics=("arbitrary", "arbitrary", "arbitrary"),
            vmem_limit_bytes=VMEM_LIMIT),
    )(rel_bias, proj, proj, proj, tables)


def _out_kernel(attn_ref, ga_ref, cb_ref, cc_ref, cx_ref, gc_ref, ma_ref, mc_ref,
                ccp_ref, cxp_ref, x_ref, mod_ref, cw_ref, woa_ref, woc_ref, wout_ref, fg_ref,
                o_ref):
    s = pl.program_id(1)
    ts = x_ref.shape[1]
    halo = ccp_ref.shape[1]

    u = cc_ref[0].astype(F32) * cx_ref[0].astype(F32)
    up = ccp_ref[0].astype(F32) * cxp_ref[0].astype(F32)
    up = up * jnp.where(s > 0, 1.0, 0.0).astype(F32)
    p1 = up[halo - 1:halo, :]
    p2 = up[halo - 2:halo - 1, :]
    row = lax.broadcasted_iota(jnp.int32, u.shape, 0)
    u1 = jnp.where(row == 0, p1, pltpu.roll(u, 1, 0))
    u2 = jnp.where(row == 0, p2, jnp.where(row == 1, p1, pltpu.roll(u, 2, 0)))
    y = cw_ref[0:1, :] * u2 + cw_ref[1:2, :] * u1 + cw_ref[2:3, :] * u
    z_conv = cb_ref[0].astype(F32) * y * _silu(gc_ref[0].astype(F32))
    y_conv = jnp.dot(z_conv.astype(BF16), woc_ref[...], preferred_element_type=F32)

    z_attn = attn_ref[0].astype(F32) * _silu(ga_ref[0].astype(F32))
    y_attn = jnp.dot(z_attn.astype(BF16), woa_ref[...], preferred_element_type=F32)

    merged = (jax.nn.sigmoid(ma_ref[0].astype(F32)) * y_attn
              + jax.nn.sigmoid(mc_ref[0].astype(F32)) * y_conv)
    branch = jnp.dot(merged.astype(BF16), wout_ref[...], preferred_element_type=F32)
    res = x_ref[0] + mod_ref[0, 2:3, :] * branch
    inv = lax.rsqrt(jnp.mean(res * res, axis=-1, keepdims=True) + EPS)
    o_ref[0] = res * inv * fg_ref[...]


def _output_stage(attn, proj, x, mod3, conv_w, woa, woc, wout, final_g, ts=512, halo=8):
    bsz, seq, d = x.shape
    tpb = ts // halo

    def col(k):
        return pl.BlockSpec((1, ts, d), lambda b, s: (b, s, k))

    def prev(k):
        return pl.BlockSpec((1, halo, d), lambda b, s: (b, jnp.maximum(s * tpb - 1, 0), k))

    def whole(shape):
        return pl.BlockSpec(shape, lambda b, s: (0,) * len(shape))

    return pl.pallas_call(
        _out_kernel,
        grid=(bsz, seq // ts),
        in_specs=[col(0), col(3), col(4), col(5), col(6), col(7), col(8), col(9),
                  prev(5), prev(6),
                  col(0), pl.BlockSpec((1, 3, d), lambda b, s: (b, 0, 0)),
                  whole((CONV_K, d)), whole((d, d)), whole((d, d)), whole((d, d)), whole((1, d))],
        out_specs=col(0),
        out_shape=jax.ShapeDtypeStruct((bsz, seq, d), F32),
        compiler_params=pltpu.CompilerParams(
            dimension_semantics=("arbitrary", "arbitrary"),
            vmem_limit_bytes=VMEM_LIMIT),
    )(attn, proj, proj, proj, proj, proj, proj, proj, proj, proj,
      x, mod3, conv_w, woa, woc, wout, final_g.reshape(1, d))


def kernel(x, c, norm_g, w_ada, b_ada, w_in, conv_w, w_o_attn, w_o_conv, w_out, rel_bias, final_g):
    bsz, seq, d = x.shape
    depth = norm_g.shape[0]
    assert depth == 1, "the fused output stage applies the final RMSNorm after the single layer"
    assert d == N_HEADS * HEAD_DIM and w_in.shape[2] == N_PROJ * d and seq % MOBA_BLOCK == 0
    tables = _bias_tables(rel_bias)
    mod3 = _modulation(c, w_ada[0], b_ada[0]).reshape(bsz, 3, d)
    proj = _in_projection(x, mod3, norm_g[0], w_in[0].astype(BF16))
    attn = _moba_attention(proj, tables, rel_bias, d)
    return _output_stage(attn, proj, x, mod3, conv_w[0], w_o_attn[0].astype(BF16),
                         w_o_conv[0].astype(BF16), w_out[0].astype(BF16), final_g)
```

```python
import functools
import math

import jax
import jax.numpy as jnp
from jax import lax
from jax.experimental import pallas as pl
from jax.experimental.pallas import tpu as pltpu

N_HEADS = 8
HEAD_DIM = 128
CONV_K = 3
MOBA_BLOCK = 256
MOBA_TOPK = 3
N_BUCKETS = 32
MAX_DISTANCE = 128
EPS = 1e-6
N_PROJ = 10

LANES = 128
NEG = -1e30
VMEM_LIMIT = 56 * 1024 * 1024

F32 = jnp.float32
BF16 = jnp.bfloat16


def _silu(v):
    return v * jax.nn.sigmoid(v)


def _mod_kernel(c_ref, w_ref, b_ref, o_ref):
    a = _silu(c_ref[...])
    o_ref[...] = jnp.dot(a, w_ref[...], precision=lax.Precision.HIGHEST,
                         preferred_element_type=F32) + b_ref[...]


def _modulation(c, w_ada, b_ada):
    bsz, d = c.shape
    n = w_ada.shape[1]
    tn = d
    return pl.pallas_call(
        _mod_kernel,
        grid=(n // tn,),
        in_specs=[pl.BlockSpec((bsz, d), lambda j: (0, 0)),
                  pl.BlockSpec((d, tn), lambda j: (0, j)),
                  pl.BlockSpec((1, tn), lambda j: (0, j))],
        out_specs=pl.BlockSpec((bsz, tn), lambda j: (0, j)),
        out_shape=jax.ShapeDtypeStruct((bsz, n), F32),
        compiler_params=pltpu.CompilerParams(vmem_limit_bytes=VMEM_LIMIT),
    )(c, w_ada, b_ada.reshape(1, n))


def _inproj_kernel(x_ref, mod_ref, g_ref, w_ref, o_ref, h_ref, *, q_scale, row_chunk):
    j = pl.program_id(2)
    ts = x_ref.shape[1]

    @pl.when(j == 0)
    def _():
        shift = mod_ref[0, 0:1, :]
        scale1 = 1.0 + mod_ref[0, 1:2, :]
        g = g_ref[...]

        def body(r, carry):
            rows = pl.ds(pl.multiple_of(r * row_chunk, row_chunk), row_chunk)
            xr = x_ref[0, rows, :]
            inv = lax.rsqrt(jnp.mean(xr * xr, axis=-1, keepdims=True) + EPS)
            h_ref[rows, :] = ((xr * inv * g) * scale1 + shift).astype(BF16)
            return carry

        lax.fori_loop(0, ts // row_chunk, body, 0)

    acc = jnp.dot(h_ref[...], w_ref[...], preferred_element_type=F32)
    acc = acc * jnp.where(j == 0, q_scale, 1.0).astype(F32)
    o_ref[0] = acc.astype(o_ref.dtype)


def _in_projection(x, mod3, norm_g, w_in_bf16, ts=1024, row_chunk=128):
    bsz, seq, d = x.shape
    n = w_in_bf16.shape[1]
    tn = d
    kern = functools.partial(_inproj_kernel, q_scale=HEAD_DIM ** -0.5, row_chunk=row_chunk)
    return pl.pallas_call(
        kern,
        grid=(bsz, seq // ts, n // tn),
        in_specs=[pl.BlockSpec((1, ts, d), lambda b, s, j: (b, s, 0)),
                  pl.BlockSpec((1, 3, d), lambda b, s, j: (b, 0, 0)),
                  pl.BlockSpec((1, d), lambda b, s, j: (0, 0)),
                  pl.BlockSpec((d, tn), lambda b, s, j: (0, j))],
        out_specs=pl.BlockSpec((1, ts, tn), lambda b, s, j: (b, s, j)),
        out_shape=jax.ShapeDtypeStruct((bsz, seq, n), BF16),
        scratch_shapes=[pltpu.VMEM((ts, d), BF16)],
        compiler_params=pltpu.CompilerParams(
            dimension_semantics=("arbitrary", "arbitrary", "arbitrary"),
            vmem_limit_bytes=VMEM_LIMIT),
    )(x, mod3, norm_g.reshape(1, d), w_in_bf16)


def _t5_bucket(dist):
    n = jnp.maximum(dist, 0)
    max_exact = N_BUCKETS // 2
    nf = jnp.maximum(n, 1).astype(F32)
    large = max_exact + (jnp.log(nf / max_exact) / math.log(MAX_DISTANCE / max_exact)
                         * (N_BUCKETS - max_exact)).astype(jnp.int32)
    large = jnp.minimum(large, N_BUCKETS - 1)
    return jnp.where(n < max_exact, n, large)


def _table_kernel(rb_ref, o_ref):
    h = pl.program_id(0)
    blk = o_ref.shape[1]
    r = lax.broadcasted_iota(jnp.int32, (blk, blk), 0)
    c = lax.broadcasted_iota(jnp.int32, (blk, blk), 1)
    for t in range(2):
        dist = r - c + t * blk
        bucket = _t5_bucket(dist)
        bias = jnp.zeros((blk, blk), F32)
        for b in range(N_BUCKETS):
            bias = jnp.where(bucket == b, rb_ref[b, h], bias)
        if t == 0:
            bias = jnp.where(dist >= 0, bias, NEG)
        o_ref[0, :, (1 - t) * blk:(2 - t) * blk] = bias


def _bias_tables(rel_bias):
    return pl.pallas_call(
        _table_kernel,
        grid=(N_HEADS,),
        in_specs=[pl.BlockSpec(memory_space=pltpu.SMEM)],
        out_specs=pl.BlockSpec((1, MOBA_BLOCK, 2 * MOBA_BLOCK), lambda h: (h, 0, 0)),
        out_shape=jax.ShapeDtypeStruct((N_HEADS, MOBA_BLOCK, 2 * MOBA_BLOCK), F32),
        compiler_params=pltpu.CompilerParams(vmem_limit_bytes=VMEM_LIMIT),
    )(rel_bias)


def _attn_kernel(rb_ref, q_ref, k_ref, v_ref, tab_ref, o_ref, qaug_ref, kaug_ref, kmean_ref, *, nb):
    h = pl.program_id(1)
    blk = MOBA_BLOCK
    hd = HEAD_DIM
    seq = nb * blk
    nt = (((1,), (1,)), ((), ()))

    lane = lax.broadcasted_iota(jnp.int32, (blk, LANES), 1)
    for n in range(nb):
        rows = slice(n * blk, (n + 1) * blk)
        kb = k_ref[0, rows, :]
        kaug_ref[rows, 0:hd] = kb
        kaug_ref[rows, hd:hd + LANES] = jnp.where((lane == n) | (lane == nb + n), 1.0, 0.0).astype(BF16)
        kmean_ref[n:n + 1, :] = jnp.mean(kb.astype(F32), axis=0, keepdims=True)

    km = kmean_ref[...]
    km_hi = km.astype(BF16)
    rem = km - km_hi.astype(F32)
    km_mid = rem.astype(BF16)
    km_lo = (rem - km_mid.astype(F32)).astype(BF16)
    sc3 = lax.dot_general(jnp.concatenate([km_hi, km_mid, km_lo], axis=0), q_ref[0], nt,
                          preferred_element_type=F32)
    sc = (sc3[0:nb] + sc3[nb:2 * nb]) + sc3[2 * nb:3 * nb]

    n_iota = lax.broadcasted_iota(jnp.int32, (nb, seq), 0)
    q_blk = lax.shift_right_logical(lax.broadcasted_iota(jnp.int32, (nb, seq), 1),
                                    int(math.log2(blk)))
    rank = jnp.zeros((nb, seq), jnp.int32)
    for m in range(nb):
        row = sc[m:m + 1, :]
        beats = (row > sc) | ((row == sc) & (m < n_iota))
        rank = rank + jnp.where(beats & (m < q_blk), 1, 0)
    past = n_iota < q_blk
    chosen = past & (rank < MOBA_TOPK)
    far = chosen & (n_iota < q_blk - 1)

    b_far = jnp.full((nb, seq), rb_ref[N_BUCKETS - 1, h], F32)
    b_hi = b_far.astype(BF16).astype(F32)
    add_hi = jnp.where(past, jnp.where(chosen, jnp.where(far, b_hi, 0.0), NEG), 0.0)
    add_lo = jnp.where(far, b_far - b_hi, 0.0)
    add_t = jnp.concatenate([add_hi, add_lo, jnp.zeros((LANES - 2 * nb, seq), F32)], axis=0)
    qaug_ref[:, 0:hd] = q_ref[0]
    qaug_ref[:, hd:hd + LANES] = add_t.T.astype(BF16)

    for i in range(nb):
        n_keys = (i + 1) * blk
        s = lax.dot_general(qaug_ref[i * blk:(i + 1) * blk, :], kaug_ref[0:n_keys, :], nt,
                            preferred_element_type=F32)
        if i == 0:
            s = s + tab_ref[0, :, blk:2 * blk]
        elif i == 1:
            s = s + tab_ref[0]
        else:
            s = jnp.concatenate([s[:, :n_keys - 2 * blk], s[:, n_keys - 2 * blk:] + tab_ref[0]], axis=1)
        mx = jnp.max(s, axis=-1, keepdims=True)
        p = jnp.exp(s - mx)
        den = jnp.sum(p, axis=-1, keepdims=True)
        acc = jnp.dot(p.astype(BF16), v_ref[0, 0:n_keys, :], preferred_element_type=F32)
        o_ref[0, i * blk:(i + 1) * blk, :] = (acc / den).astype(o_ref.dtype)


def _moba_attention(proj, tables, rel_bias, d_model):
    bsz, seq, _ = proj.shape
    nb = seq // MOBA_BLOCK
    nh = d_model // HEAD_DIM
    assert MAX_DISTANCE <= MOBA_BLOCK + 1 and 2 * nb <= LANES and MOBA_BLOCK & (MOBA_BLOCK - 1) == 0
    kern = functools.partial(_attn_kernel, nb=nb)

    def head_cols(first):
        return pl.BlockSpec((1, seq, HEAD_DIM), lambda b, h: (b, 0, first + h))

    return pl.pallas_call(
        kern,
        grid=(bsz, nh),
        in_specs=[pl.BlockSpec(memory_space=pltpu.SMEM),
                  head_cols(0), head_cols(nh), head_cols(2 * nh),
                  pl.BlockSpec((1, MOBA_BLOCK, 2 * MOBA_BLOCK), lambda b, h: (h, 0, 0))],
        out_specs=head_cols(0),
        out_shape=jax.ShapeDtypeStruct((bsz, seq, d_model), BF16),
        scratch_shapes=[pltpu.VMEM((seq, HEAD_DIM + LANES), BF16),
                        pltpu.VMEM((seq, HEAD_DIM + LANES), BF16),
                        pltpu.VMEM((nb, HEAD_DIM), F32)],
        compiler_params=pltpu.CompilerParams(
            dimension_semantics=("arbitrary", "arbitrary"),
            vmem_limit_bytes=VMEM_LIMIT),
    )(rel_bias, proj, proj, proj, tables)


def _out_kernel(attn_ref, ga_ref, cb_ref, cc_ref, cx_ref, gc_ref, ma_ref, mc_ref,
                ccp_ref, cxp_ref, x_ref, mod_ref, cw_ref, woa_ref, woc_ref, wout_ref, fg_ref,
                o_ref):
    s = pl.program_id(1)
    ts = x_ref.shape[1]
    halo = ccp_ref.shape[1]

    u = cc_ref[0].astype(F32) * cx_ref[0].astype(F32)
    up = ccp_ref[0].astype(F32) * cxp_ref[0].astype(F32)
    up = up * jnp.where(s > 0, 1.0, 0.0).astype(F32)
    p1 = up[halo - 1:halo, :]
    p2 = up[halo - 2:halo - 1, :]
    row = lax.broadcasted_iota(jnp.int32, u.shape, 0)
    u1 = jnp.where(row == 0, p1, pltpu.roll(u, 1, 0))
    u2 = jnp.where(row == 0, p2, jnp.where(row == 1, p1, pltpu.roll(u, 2, 0)))
    y = cw_ref[0:1, :] * u2 + cw_ref[1:2, :] * u1 + cw_ref[2:3, :] * u
    z_conv = cb_ref[0].astype(F32) * y * _silu(gc_ref[0].astype(F32))
    y_conv = jnp.dot(z_conv.astype(BF16), woc_ref[...], preferred_element_type=F32)

    z_attn = attn_ref[0].astype(F32) * _silu(ga_ref[0].astype(F32))
    y_attn = jnp.dot(z_attn.astype(BF16), woa_ref[...], preferred_element_type=F32)

    merged = (jax.nn.sigmoid(ma_ref[0].astype(F32)) * y_attn
              + jax.nn.sigmoid(mc_ref[0].astype(F32)) * y_conv)
    branch = jnp.dot(merged.astype(BF16), wout_ref[...], preferred_element_type=F32)
    res = x_ref[0] + mod_ref[0, 2:3, :] * branch
    inv = lax.rsqrt(jnp.mean(res * res, axis=-1, keepdims=True) + EPS)
    o_ref[0] = res * inv * fg_ref[...]


def _output_stage(attn, proj, x, mod3, conv_w, woa, woc, wout, final_g, ts=512, halo=8):
    bsz, seq, d = x.shape
    tpb = ts // halo

    def col(k):
        return pl.BlockSpec((1, ts, d), lambda b, s: (b, s, k))

    def prev(k):
        return pl.BlockSpec((1, halo, d), lambda b, s: (b, jnp.maximum(s * tpb - 1, 0), k))

    def whole(shape):
        return pl.BlockSpec(shape, lambda b, s: (0,) * len(shape))

    return pl.pallas_call(
        _out_kernel,
        grid=(bsz, seq // ts),
        in_specs=[col(0), col(3), col(4), col(5), col(6), col(7), col(8), col(9),
                  prev(5), prev(6),
                  col(0), pl.BlockSpec((1, 3, d), lambda b, s: (b, 0, 0)),
                  whole((CONV_K, d)), whole((d, d)), whole((d, d)), whole((d, d)), whole((1, d))],
        out_specs=col(0),
        out_shape=jax.ShapeDtypeStruct((bsz, seq, d), F32),
        compiler_params=pltpu.CompilerParams(
            dimension_semantics=("arbitrary", "arbitrary"),
            vmem_limit_bytes=VMEM_LIMIT),
    )(attn, proj, proj, proj, proj, proj, proj, proj, proj, proj,
      x, mod3, conv_w, woa, woc, wout, final_g.reshape(1, d))


def kernel(x, c, norm_g, w_ada, b_ada, w_in, conv_w, w_o_attn, w_o_conv, w_out, rel_bias, final_g):
    bsz, seq, d = x.shape
    depth = norm_g.shape[0]
    assert depth == 1, "the fused output stage applies the final RMSNorm after the single layer"
    assert d == N_HEADS * HEAD_DIM and w_in.shape[2] == N_PROJ * d and seq % MOBA_BLOCK == 0
    tables = _bias_tables(rel_bias)
    mod3 = _modulation(c, w_ada[0], b_ada[0]).reshape(bsz, 3, d)
    proj = _in_projection(x, mod3, norm_g[0], w_in[0].astype(BF16))
    attn = _moba_attention(proj, tables, rel_bias, d)
    return _output_stage(attn, proj, x, mod3, conv_w[0], w_o_attn[0].astype(BF16),
                         w_o_conv[0].astype(BF16), w_out[0].astype(BF16), final_g)
```

```python
import functools
import math

import jax
import jax.numpy as jnp
from jax import lax
from jax.experimental import pallas as pl
from jax.experimental.pallas import tpu as pltpu

N_HEADS = 8
HEAD_DIM = 128
CONV_K = 3
MOBA_BLOCK = 256
MOBA_TOPK = 3
N_BUCKETS = 32
MAX_DISTANCE = 128
EPS = 1e-6
N_PROJ = 10

LANES = 128
NEG = -1e30
LOG2E = math.log2(math.e)
VMEM_LIMIT = 56 * 1024 * 1024

F32 = jnp.float32
BF16 = jnp.bfloat16


def _silu(v):
    return v * jax.nn.sigmoid(v)


def _mod_kernel(c_ref, w_ref, b_ref, o_ref):
    a = _silu(c_ref[...])
    o_ref[...] = jnp.dot(a, w_ref[...], precision=lax.Precision.HIGHEST,
                         preferred_element_type=F32) + b_ref[...]


def _modulation(c, w_ada, b_ada):
    bsz, d = c.shape
    n = w_ada.shape[1]
    tn = d
    return pl.pallas_call(
        _mod_kernel,
        grid=(n // tn,),
        in_specs=[pl.BlockSpec((bsz, d), lambda j: (0, 0)),
                  pl.BlockSpec((d, tn), lambda j: (0, j)),
                  pl.BlockSpec((1, tn), lambda j: (0, j))],
        out_specs=pl.BlockSpec((bsz, tn), lambda j: (0, j)),
        out_shape=jax.ShapeDtypeStruct((bsz, n), F32),
        compiler_params=pltpu.CompilerParams(vmem_limit_bytes=VMEM_LIMIT),
    )(c, w_ada, b_ada.reshape(1, n))


def _inproj_kernel(x_ref, mod_ref, g_ref, w_ref, o_ref, h_ref, *, q_scale, row_chunk):
    j = pl.program_id(2)
    ts = x_ref.shape[1]

    @pl.when(j == 0)
    def _():
        shift = mod_ref[0, 0:1, :]
        scale1 = 1.0 + mod_ref[0, 1:2, :]
        g = g_ref[...]

        def body(r, carry):
            rows = pl.ds(pl.multiple_of(r * row_chunk, row_chunk), row_chunk)
            xr = x_ref[0, rows, :]
            inv = lax.rsqrt(jnp.mean(xr * xr, axis=-1, keepdims=True) + EPS)
            h_ref[rows, :] = ((xr * inv * g) * scale1 + shift).astype(BF16)
            return carry

        lax.fori_loop(0, ts // row_chunk, body, 0)

    acc = jnp.dot(h_ref[...], w_ref[...], preferred_element_type=F32)
    acc = acc * jnp.where(j == 0, q_scale, 1.0).astype(F32)
    o_ref[0] = acc.astype(o_ref.dtype)


def _in_projection(x, mod3, norm_g, w_in_bf16, ts=1024, row_chunk=128):
    bsz, seq, d = x.shape
    n = w_in_bf16.shape[1]
    tn = d
    kern = functools.partial(_inproj_kernel, q_scale=LOG2E * HEAD_DIM ** -0.5, row_chunk=row_chunk)
    return pl.pallas_call(
        kern,
        grid=(bsz, seq // ts, n // tn),
        in_specs=[pl.BlockSpec((1, ts, d), lambda b, s, j: (b, s, 0)),
                  pl.BlockSpec((1, 3, d), lambda b, s, j: (b, 0, 0)),
                  pl.BlockSpec((1, d), lambda b, s, j: (0, 0)),
                  pl.BlockSpec((d, tn), lambda b, s, j: (0, j))],
        out_specs=pl.BlockSpec((1, ts, tn), lambda b, s, j: (b, s, j)),
        out_shape=jax.ShapeDtypeStruct((bsz, seq, n), BF16),
        scratch_shapes=[pltpu.VMEM((ts, d), BF16)],
        compiler_params=pltpu.CompilerParams(
            dimension_semantics=("arbitrary", "arbitrary", "arbitrary"),
            vmem_limit_bytes=VMEM_LIMIT),
    )(x, mod3, norm_g.reshape(1, d), w_in_bf16)


def _t5_bucket(dist):
    n = jnp.maximum(dist, 0)
    max_exact = N_BUCKETS // 2
    nf = jnp.maximum(n, 1).astype(F32)
    large = max_exact + (jnp.log(nf / max_exact) / math.log(MAX_DISTANCE / max_exact)
                         * (N_BUCKETS - max_exact)).astype(jnp.int32)
    large = jnp.minimum(large, N_BUCKETS - 1)
    return jnp.where(n < max_exact, n, large)


def _table_kernel(rb_ref, o_ref):
    h = pl.program_id(0)
    blk = o_ref.shape[1]
    r = lax.broadcasted_iota(jnp.int32, (blk, blk), 0)
    c = lax.broadcasted_iota(jnp.int32, (blk, blk), 1)
    for t in range(2):
        dist = r - c + t * blk
        bucket = _t5_bucket(dist)
        bias = jnp.zeros((blk, blk), F32)
        for b in range(N_BUCKETS):
            bias = jnp.where(bucket == b, rb_ref[b, h] * LOG2E, bias)
        if t == 0:
            bias = jnp.where(dist >= 0, bias, NEG)
        o_ref[0, :, (1 - t) * blk:(2 - t) * blk] = bias


def _bias_tables(rel_bias):
    return pl.pallas_call(
        _table_kernel,
        grid=(N_HEADS,),
        in_specs=[pl.BlockSpec(memory_space=pltpu.SMEM)],
        out_specs=pl.BlockSpec((1, MOBA_BLOCK, 2 * MOBA_BLOCK), lambda h: (h, 0, 0)),
        out_shape=jax.ShapeDtypeStruct((N_HEADS, MOBA_BLOCK, 2 * MOBA_BLOCK), F32),
        compiler_params=pltpu.CompilerParams(vmem_limit_bytes=VMEM_LIMIT),
    )(rel_bias)


def _attn_kernel(rb_ref, q_ref, k_ref, v_ref, tab_ref, o_ref, qaug_ref, kaug_ref, kmean_ref, *, nb):
    h = pl.program_id(1)
    blk = MOBA_BLOCK
    hd = HEAD_DIM
    seq = nb * blk
    nt = (((1,), (1,)), ((), ()))

    lane = lax.broadcasted_iota(jnp.int32, (blk, LANES), 1)
    for n in range(nb):
        rows = slice(n * blk, (n + 1) * blk)
        kb = k_ref[0, rows, :]
        kaug_ref[rows, 0:hd] = kb
        kaug_ref[rows, hd:hd + LANES] = jnp.where((lane == n) | (lane == nb + n), 1.0, 0.0).astype(BF16)
        kmean_ref[n:n + 1, :] = jnp.mean(kb.astype(F32), axis=0, keepdims=True)

    km = kmean_ref[...]
    km_hi = km.astype(BF16)
    rem = km - km_hi.astype(F32)
    km_mid = rem.astype(BF16)
    km_lo = (rem - km_mid.astype(F32)).astype(BF16)
    sc3 = lax.dot_general(jnp.concatenate([km_hi, km_mid, km_lo], axis=0), q_ref[0], nt,
                          preferred_element_type=F32)
    sc = (sc3[0:nb] + sc3[nb:2 * nb]) + sc3[2 * nb:3 * nb]

    n_iota = lax.broadcasted_iota(jnp.int32, (nb, seq), 0)
    q_blk = lax.shift_right_logical(lax.broadcasted_iota(jnp.int32, (nb, seq), 1),
                                    int(math.log2(blk)))
    rank = jnp.zeros((nb, seq), jnp.int32)
    for m in range(nb):
        row = sc[m:m + 1, :]
        beats = (row > sc) | ((row == sc) & (m < n_iota))
        rank = rank + jnp.where(beats & (m < q_blk), 1, 0)
    past = n_iota < q_blk
    chosen = past & (rank < MOBA_TOPK)
    far = chosen & (n_iota < q_blk - 1)

    b_far = jnp.full((nb, seq), rb_ref[N_BUCKETS - 1, h] * LOG2E, F32)
    b_hi = b_far.astype(BF16).astype(F32)
    add_hi = jnp.where(past, jnp.where(chosen, jnp.where(far, b_hi, 0.0), NEG), 0.0)
    add_lo = jnp.where(far, b_far - b_hi, 0.0)
    add_t = jnp.concatenate([add_hi, add_lo, jnp.zeros((LANES - 2 * nb, seq), F32)], axis=0)
    qaug_ref[:, 0:hd] = q_ref[0]
    qaug_ref[:, hd:hd + LANES] = add_t.T.astype(BF16)

    def logits(i):
        n_keys = (i + 1) * blk
        s = lax.dot_general(qaug_ref[i * blk:(i + 1) * blk, :], kaug_ref[0:n_keys, :], nt,
                            preferred_element_type=F32)
        if i == 0:
            return s + tab_ref[0, :, blk:2 * blk]
        if i == 1:
            return s + tab_ref[0]
        return jnp.concatenate([s[:, :n_keys - 2 * blk], s[:, n_keys - 2 * blk:] + tab_ref[0]], axis=1)

    s = logits(0)
    for i in range(nb):
        s_next = logits(i + 1) if i + 1 < nb else None
        mx = jnp.max(s, axis=-1, keepdims=True)
        p = jnp.exp2(s - mx)
        den = jnp.sum(p, axis=-1, keepdims=True)
        acc = jnp.dot(p.astype(BF16), v_ref[0, 0:(i + 1) * blk, :], preferred_element_type=F32)
        o_ref[0, i * blk:(i + 1) * blk, :] = (acc / den).astype(o_ref.dtype)
        s = s_next


def _moba_attention(proj, tables, rel_bias, d_model):
    bsz, seq, _ = proj.shape
    nb = seq // MOBA_BLOCK
    nh = d_model // HEAD_DIM
    assert MAX_DISTANCE <= MOBA_BLOCK + 1 and 2 * nb <= LANES and MOBA_BLOCK & (MOBA_BLOCK - 1) == 0
    kern = functools.partial(_attn_kernel, nb=nb)

    def head_cols(first):
        return pl.BlockSpec((1, seq, HEAD_DIM), lambda b, h: (b, 0, first + h))

    return pl.pallas_call(
        kern,
        grid=(bsz, nh),
        in_specs=[pl.BlockSpec(memory_space=pltpu.SMEM),
                  head_cols(0), head_cols(nh), head_cols(2 * nh),
                  pl.BlockSpec((1, MOBA_BLOCK, 2 * MOBA_BLOCK), lambda b, h: (h, 0, 0))],
        out_specs=head_cols(0),
        out_shape=jax.ShapeDtypeStruct((bsz, seq, d_model), BF16),
        scratch_shapes=[pltpu.VMEM((seq, HEAD_DIM + LANES), BF16),
                        pltpu.VMEM((seq, HEAD_DIM + LANES), BF16),
                        pltpu.VMEM((nb, HEAD_DIM), F32)],
        compiler_params=pltpu.CompilerParams(
            dimension_semantics=("arbitrary", "arbitrary"),
            vmem_limit_bytes=VMEM_LIMIT),
    )(rel_bias, proj, proj, proj, tables)


def _out_kernel(attn_ref, ga_ref, cb_ref, cc_ref, cx_ref, gc_ref, ma_ref, mc_ref,
                ccp_ref, cxp_ref, x_ref, mod_ref, cw_ref, woa_ref, woc_ref, wout_ref, fg_ref,
                o_ref):
    s = pl.program_id(1)
    ts = x_ref.shape[1]
    halo = ccp_ref.shape[1]

    u = cc_ref[0].astype(F32) * cx_ref[0].astype(F32)
    up = ccp_ref[0].astype(F32) * cxp_ref[0].astype(F32)
    up = up * jnp.where(s > 0, 1.0, 0.0).astype(F32)
    p1 = up[halo - 1:halo, :]
    p2 = up[halo - 2:halo - 1, :]
    row = lax.broadcasted_iota(jnp.int32, u.shape, 0)
    u1 = jnp.where(row == 0, p1, pltpu.roll(u, 1, 0))
    u2 = jnp.where(row == 0, p2, jnp.where(row == 1, p1, pltpu.roll(u, 2, 0)))
    y = cw_ref[0:1, :] * u2 + cw_ref[1:2, :] * u1 + cw_ref[2:3, :] * u
    z_conv = cb_ref[0].astype(F32) * y * _silu(gc_ref[0].astype(F32))
    y_conv = jnp.dot(z_conv.astype(BF16), woc_ref[...], preferred_element_type=F32)

    z_attn = attn_ref[0].astype(F32) * _silu(ga_ref[0].astype(F32))
    y_attn = jnp.dot(z_attn.astype(BF16), woa_ref[...], preferred_element_type=F32)

    merged = (jax.nn.sigmoid(ma_ref[0].astype(F32)) * y_attn
              + jax.nn.sigmoid(mc_ref[0].astype(F32)) * y_conv)
    branch = jnp.dot(merged.astype(BF16), wout_ref[...], preferred_element_type=F32)
    res = x_ref[0] + mod_ref[0, 2:3, :] * branch
    inv = lax.rsqrt(jnp.mean(res * res, axis=-1, keepdims=True) + EPS)
    o_ref[0] = res * inv * fg_ref[...]


def _output_stage(attn, proj, x, mod3, conv_w, woa, woc, wout, final_g, ts=512, halo=8):
    bsz, seq, d = x.shape
    tpb = ts // halo

    def col(k):
        return pl.BlockSpec((1, ts, d), lambda b, s: (b, s, k))

    def prev(k):
        return pl.BlockSpec((1, halo, d), lambda b, s: (b, jnp.maximum(s * tpb - 1, 0), k))

    def whole(shape):
        return pl.BlockSpec(shape, lambda b, s: (0,) * len(shape))

    return pl.pallas_call(
        _out_kernel,
        grid=(bsz, seq // ts),
        in_specs=[col(0), col(3), col(4), col(5), col(6), col(7), col(8), col(9),
                  prev(5), prev(6),
                  col(0), pl.BlockSpec((1, 3, d), lambda b, s: (b, 0, 0)),
                  whole((CONV_K, d)), whole((d, d)), whole((d, d)), whole((d, d)), whole((1, d))],
        out_specs=col(0),
        out_shape=jax.ShapeDtypeStruct((bsz, seq, d), F32),
        compiler_params=pltpu.CompilerParams(
            dimension_semantics=("arbitrary", "arbitrary"),
            vmem_limit_bytes=VMEM_LIMIT),
    )(attn, proj, proj, proj, proj, proj, proj, proj, proj, proj,
      x, mod3, conv_w, woa, woc, wout, final_g.reshape(1, d))


def kernel(x, c, norm_g, w_ada, b_ada, w_in, conv_w, w_o_attn, w_o_conv, w_out, rel_bias, final_g):
    bsz, seq, d = x.shape
    depth = norm_g.shape[0]
    assert depth == 1, "the fused output stage applies the final RMSNorm after the single layer"
    assert d == N_HEADS * HEAD_DIM and w_in.shape[2] == N_PROJ * d and seq % MOBA_BLOCK == 0
    tables = _bias_tables(rel_bias)
    mod3 = _modulation(c, w_ada[0], b_ada[0]).reshape(bsz, 3, d)
    proj = _in_projection(x, mod3, norm_g[0], w_in[0].astype(BF16))
    attn = _moba_attention(proj, tables, rel_bias, d)
    return _output_stage(attn, proj, x, mod3, conv_w[0], w_o_attn[0].astype(BF16),
                         w_o_conv[0].astype(BF16), w_out[0].astype(BF16), final_g)
```

```python
import functools
import math

import jax
import jax.numpy as jnp
from jax import lax
from jax.experimental import pallas as pl
from jax.experimental.pallas import tpu as pltpu

N_HEADS = 8
HEAD_DIM = 128
CONV_K = 3
MOBA_BLOCK = 256
MOBA_TOPK = 3
N_BUCKETS = 32
MAX_DISTANCE = 128
EPS = 1e-6
N_PROJ = 10

LANES = 128
NEG = -1e30
LOG2E = math.log2(math.e)
VMEM_LIMIT = 56 * 1024 * 1024

F32 = jnp.float32
BF16 = jnp.bfloat16


def _silu(v):
    return v * jax.nn.sigmoid(v)


def _mod_kernel(c_ref, w_ref, b_ref, o_ref):
    a = _silu(c_ref[...])
    o_ref[...] = jnp.dot(a, w_ref[...], precision=lax.Precision.HIGHEST,
                         preferred_element_type=F32) + b_ref[...]


def _modulation(c, w_ada, b_ada):
    bsz, d = c.shape
    n = w_ada.shape[1]
    tn = d
    return pl.pallas_call(
        _mod_kernel,
        grid=(n // tn,),
        in_specs=[pl.BlockSpec((bsz, d), lambda j: (0, 0)),
                  pl.BlockSpec((d, tn), lambda j: (0, j)),
                  pl.BlockSpec((1, tn), lambda j: (0, j))],
        out_specs=pl.BlockSpec((bsz, tn), lambda j: (0, j)),
        out_shape=jax.ShapeDtypeStruct((bsz, n), F32),
        compiler_params=pltpu.CompilerParams(vmem_limit_bytes=VMEM_LIMIT),
    )(c, w_ada, b_ada.reshape(1, n))


def _inproj_kernel(x_ref, mod_ref, g_ref, w_ref, o_ref, h_ref, *, q_scale, row_chunk, tn):
    ts, d = x_ref.shape[1], x_ref.shape[2]
    shift = mod_ref[0, 0:1, :]
    scale1 = 1.0 + mod_ref[0, 1:2, :]
    g = g_ref[...]

    def body(r, carry):
        rows = pl.ds(pl.multiple_of(r * row_chunk, row_chunk), row_chunk)
        xr = x_ref[0, rows, :]
        inv = lax.rsqrt(jnp.mean(xr * xr, axis=-1, keepdims=True) + EPS)
        h_ref[rows, :] = ((xr * inv * g) * scale1 + shift).astype(BF16)
        return carry

    lax.fori_loop(0, ts // row_chunk, body, 0)

    for j in range(w_ref.shape[1] // tn):
        cols = slice(j * tn, (j + 1) * tn)
        acc = jnp.dot(h_ref[...], w_ref[:, cols], preferred_element_type=F32)
        if j * tn < d:
            acc = acc * q_scale
        o_ref[0, :, cols] = acc.astype(o_ref.dtype)


def _in_projection(x, mod3, norm_g, w_in_bf16, ts=512, tn=1024, row_chunk=128):
    bsz, seq, d = x.shape
    n = w_in_bf16.shape[1]
    assert d % tn == 0
    kern = functools.partial(_inproj_kernel, q_scale=LOG2E * HEAD_DIM ** -0.5, row_chunk=row_chunk,
                             tn=tn)
    return pl.pallas_call(
        kern,
        grid=(bsz, seq // ts),
        in_specs=[pl.BlockSpec((1, ts, d), lambda b, s: (b, s, 0)),
                  pl.BlockSpec((1, 3, d), lambda b, s: (b, 0, 0)),
                  pl.BlockSpec((1, d), lambda b, s: (0, 0)),
                  pl.BlockSpec((d, n), lambda b, s: (0, 0), pipeline_mode=pl.Buffered(1))],
        out_specs=pl.BlockSpec((1, ts, n), lambda b, s: (b, s, 0)),
        out_shape=jax.ShapeDtypeStruct((bsz, seq, n), BF16),
        scratch_shapes=[pltpu.VMEM((ts, d), BF16)],
        compiler_params=pltpu.CompilerParams(
            dimension_semantics=("arbitrary", "arbitrary"),
            vmem_limit_bytes=VMEM_LIMIT),
    )(x, mod3, norm_g.reshape(1, d), w_in_bf16)


def _t5_bucket(dist):
    n = jnp.maximum(dist, 0)
    max_exact = N_BUCKETS // 2
    nf = jnp.maximum(n, 1).astype(F32)
    large = max_exact + (jnp.log(nf / max_exact) / math.log(MAX_DISTANCE / max_exact)
                         * (N_BUCKETS - max_exact)).astype(jnp.int32)
    large = jnp.minimum(large, N_BUCKETS - 1)
    return jnp.where(n < max_exact, n, large)


def _table_kernel(rb_ref, o_ref):
    h = pl.program_id(0)
    blk = o_ref.shape[1]
    r = lax.broadcasted_iota(jnp.int32, (blk, blk), 0)
    c = lax.broadcasted_iota(jnp.int32, (blk, blk), 1)
    for t in range(2):
        dist = r - c + t * blk
        bucket = _t5_bucket(dist)
        bias = jnp.zeros((blk, blk), F32)
        for b in range(N_BUCKETS):
            bias = jnp.where(bucket == b, rb_ref[b, h] * LOG2E, bias)
        if t == 0:
            bias = jnp.where(dist >= 0, bias, NEG)
        o_ref[0, :, (1 - t) * blk:(2 - t) * blk] = bias


def _bias_tables(rel_bias):
    return pl.pallas_call(
        _table_kernel,
        grid=(N_HEADS,),
        in_specs=[pl.BlockSpec(memory_space=pltpu.SMEM)],
        out_specs=pl.BlockSpec((1, MOBA_BLOCK, 2 * MOBA_BLOCK), lambda h: (h, 0, 0)),
        out_shape=jax.ShapeDtypeStruct((N_HEADS, MOBA_BLOCK, 2 * MOBA_BLOCK), F32),
        compiler_params=pltpu.CompilerParams(vmem_limit_bytes=VMEM_LIMIT),
    )(rel_bias)


def _attn_kernel(rb_ref, q_ref, k_ref, v_ref, tab_ref, o_ref, qaug_ref, kaug_ref, kmean_ref, *, nb):
    h = pl.program_id(1)
    blk = MOBA_BLOCK
    hd = HEAD_DIM
    seq = nb * blk
    nt = (((1,), (1,)), ((), ()))

    lane = lax.broadcasted_iota(jnp.int32, (blk, LANES), 1)
    for n in range(nb):
        rows = slice(n * blk, (n + 1) * blk)
        kb = k_ref[0, rows, :]
        kaug_ref[rows, 0:hd] = kb
        kaug_ref[rows, hd:hd + LANES] = jnp.where((lane == n) | (lane == nb + n), 1.0, 0.0).astype(BF16)
        kmean_ref[n:n + 1, :] = jnp.mean(kb.astype(F32), axis=0, keepdims=True)

    km = kmean_ref[...]
    km_hi = km.astype(BF16)
    rem = km - km_hi.astype(F32)
    km_mid = rem.astype(BF16)
    km_lo = (rem - km_mid.astype(F32)).astype(BF16)
    sc3 = lax.dot_general(jnp.concatenate([km_hi, km_mid, km_lo], axis=0), q_ref[0], nt,
                          preferred_element_type=F32)
    sc = (sc3[0:nb] + sc3[nb:2 * nb]) + sc3[2 * nb:3 * nb]

    n_iota = lax.broadcasted_iota(jnp.int32, (nb, seq), 0)
    q_blk = lax.shift_right_logical(lax.broadcasted_iota(jnp.int32, (nb, seq), 1),
                                    int(math.log2(blk)))
    rank = jnp.zeros((nb, seq), jnp.int32)
    for m in range(nb):
        row = sc[m:m + 1, :]
        beats = (row > sc) | ((row == sc) & (m < n_iota))
        rank = rank + jnp.where(beats & (m < q_blk), 1, 0)
    past = n_iota < q_blk
    chosen = past & (rank < MOBA_TOPK)
    far = chosen & (n_iota < q_blk - 1)

    b_far = jnp.full((nb, seq), rb_ref[N_BUCKETS - 1, h] * LOG2E, F32)
    b_hi = b_far.astype(BF16).astype(F32)
    add_hi = jnp.where(past, jnp.where(chosen, jnp.where(far, b_hi, 0.0), NEG), 0.0)
    add_lo = jnp.where(far, b_far - b_hi, 0.0)
    add_t = jnp.concatenate([add_hi, add_lo, jnp.zeros((LANES - 2 * nb, seq), F32)], axis=0)
    qaug_ref[:, 0:hd] = q_ref[0]
    qaug_ref[:, hd:hd + LANES] = add_t.T.astype(BF16)

    def logits(i):
        n_keys = (i + 1) * blk
        s = lax.dot_general(qaug_ref[i * blk:(i + 1) * blk, :], kaug_ref[0:n_keys, :], nt,
                            preferred_element_type=F32)
        if i == 0:
            return s + tab_ref[0, :, blk:2 * blk]
        if i == 1:
            return s + tab_ref[0]
        return jnp.concatenate([s[:, :n_keys - 2 * blk], s[:, n_keys - 2 * blk:] + tab_ref[0]], axis=1)

    s = logits(0)
    for i in range(nb):
        s_next = logits(i + 1) if i + 1 < nb else None
        mx = jnp.max(s, axis=-1, keepdims=True)
        p = jnp.exp2(s - mx)
        den = jnp.sum(p, axis=-1, keepdims=True)
        acc = jnp.dot(p.astype(BF16), v_ref[0, 0:(i + 1) * blk, :], preferred_element_type=F32)
        o_ref[0, i * blk:(i + 1) * blk, :] = (acc / den).astype(o_ref.dtype)
        s = s_next


def _moba_attention(proj, tables, rel_bias, d_model):
    bsz, seq, _ = proj.shape
    nb = seq // MOBA_BLOCK
    nh = d_model // HEAD_DIM
    assert MAX_DISTANCE <= MOBA_BLOCK + 1 and 2 * nb <= LANES and MOBA_BLOCK & (MOBA_BLOCK - 1) == 0
    kern = functools.partial(_attn_kernel, nb=nb)

    def head_cols(first):
        return pl.BlockSpec((1, seq, HEAD_DIM), lambda b, h: (b, 0, first + h))

    return pl.pallas_call(
        kern,
        grid=(bsz, nh),
        in_specs=[pl.BlockSpec(memory_space=pltpu.SMEM),
                  head_cols(0), head_cols(nh), head_cols(2 * nh),
                  pl.BlockSpec((1, MOBA_BLOCK, 2 * MOBA_BLOCK), lambda b, h: (h, 0, 0))],
        out_specs=head_cols(0),
        out_shape=jax.ShapeDtypeStruct((bsz, seq, d_model), BF16),
        scratch_shapes=[pltpu.VMEM((seq, HEAD_DIM + LANES), BF16),
                        pltpu.VMEM((seq, HEAD_DIM + LANES), BF16),
                        pltpu.VMEM((nb, HEAD_DIM), F32)],
        compiler_params=pltpu.CompilerParams(
            dimension_semantics=("arbitrary", "arbitrary"),
            vmem_limit_bytes=VMEM_LIMIT),
    )(rel_bias, proj, proj, proj, tables)


def _out_kernel(attn_ref, ga_ref, cb_ref, cc_ref, cx_ref, gc_ref, ma_ref, mc_ref,
                ccp_ref, cxp_ref, x_ref, mod_ref, cw_ref, woa_ref, woc_ref, wout_ref, fg_ref,
                o_ref):
    s = pl.program_id(1)
    ts = x_ref.shape[1]
    halo = ccp_ref.shape[1]

    u = cc_ref[0].astype(F32) * cx_ref[0].astype(F32)
    up = ccp_ref[0].astype(F32) * cxp_ref[0].astype(F32)
    up = up * jnp.where(s > 0, 1.0, 0.0).astype(F32)
    p1 = up[halo - 1:halo, :]
    p2 = up[halo - 2:halo - 1, :]
    row = lax.broadcasted_iota(jnp.int32, u.shape, 0)
    u1 = jnp.where(row == 0, p1, pltpu.roll(u, 1, 0))
    u2 = jnp.where(row == 0, p2, jnp.where(row == 1, p1, pltpu.roll(u, 2, 0)))
    y = cw_ref[0:1, :] * u2 + cw_ref[1:2, :] * u1 + cw_ref[2:3, :] * u
    z_conv = cb_ref[0].astype(F32) * y * _silu(gc_ref[0].astype(F32))
    y_conv = jnp.dot(z_conv.astype(BF16), woc_ref[...], preferred_element_type=F32)

    z_attn = attn_ref[0].astype(F32) * _silu(ga_ref[0].astype(F32))
    y_attn = jnp.dot(z_attn.astype(BF16), woa_ref[...], preferred_element_type=F32)

    merged = (jax.nn.sigmoid(ma_ref[0].astype(F32)) * y_attn
              + jax.nn.sigmoid(mc_ref[0].astype(F32)) * y_conv)
    branch = jnp.dot(merged.astype(BF16), wout_ref[...], preferred_element_type=F32)
    res = x_ref[0] + mod_ref[0, 2:3, :] * branch
    inv = lax.rsqrt(jnp.mean(res * res, axis=-1, keepdims=True) + EPS)
    o_ref[0] = res * inv * fg_ref[...]


def _output_stage(attn, proj, x, mod3, conv_w, woa, woc, wout, final_g, ts=512, halo=8):
    bsz, seq, d = x.shape
    tpb = ts // halo

    def col(k):
        return pl.BlockSpec((1, ts, d), lambda b, s: (b, s, k))

    def prev(k):
        return pl.BlockSpec((1, halo, d), lambda b, s: (b, jnp.maximum(s * tpb - 1, 0), k))

    def whole(shape):
        return pl.BlockSpec(shape, lambda b, s: (0,) * len(shape))

    return pl.pallas_call(
        _out_kernel,
        grid=(bsz, seq // ts),
        in_specs=[col(0), col(3), col(4), col(5), col(6), col(7), col(8), col(9),
                  prev(5), prev(6),
                  col(0), pl.BlockSpec((1, 3, d), lambda b, s: (b, 0, 0)),
                  whole((CONV_K, d)), whole((d, d)), whole((d, d)), whole((d, d)), whole((1, d))],
        out_specs=col(0),
        out_shape=jax.ShapeDtypeStruct((bsz, seq, d), F32),
        compiler_params=pltpu.CompilerParams(
            dimension_semantics=("arbitrary", "arbitrary"),
            vmem_limit_bytes=VMEM_LIMIT),
    )(attn, proj, proj, proj, proj, proj, proj, proj, proj, proj,
      x, mod3, conv_w, woa, woc, wout, final_g.reshape(1, d))


def kernel(x, c, norm_g, w_ada, b_ada, w_in, conv_w, w_o_attn, w_o_conv, w_out, rel_bias, final_g):
    bsz, seq, d = x.shape
    depth = norm_g.shape[0]
    assert depth == 1, "the fused output stage applies the final RMSNorm after the single layer"
    assert d == N_HEADS * HEAD_DIM and w_in.shape[2] == N_PROJ * d and seq % MOBA_BLOCK == 0
    tables = _bias_tables(rel_bias)
    mod3 = _modulation(c, w_ada[0], b_ada[0]).reshape(bsz, 3, d)
    proj = _in_projection(x, mod3, norm_g[0], w_in[0].astype(BF16))
    attn = _moba_attention(proj, tables, rel_bias, d)
    return _output_stage(attn, proj, x, mod3, conv_w[0], w_o_attn[0].astype(BF16),
                         w_o_conv[0].astype(BF16), w_out[0].astype(BF16), final_g)
```

```python
import functools
import math

import jax
import jax.numpy as jnp
from jax import lax
from jax.experimental import pallas as pl
from jax.experimental.pallas import tpu as pltpu

N_HEADS = 8
HEAD_DIM = 128
CONV_K = 3
MOBA_BLOCK = 256
MOBA_TOPK = 3
N_BUCKETS = 32
MAX_DISTANCE = 128
EPS = 1e-6
N_PROJ = 10

LANES = 128
NEG = -1e30
LOG2E = math.log2(math.e)
VMEM_LIMIT = 56 * 1024 * 1024

F32 = jnp.float32
BF16 = jnp.bfloat16


def _silu(v):
    return v * jax.nn.sigmoid(v)


def _mod_kernel(c_ref, w_ref, b_ref, o_ref):
    a = _silu(c_ref[...])
    o_ref[...] = jnp.dot(a, w_ref[...], precision=lax.Precision.HIGHEST,
                         preferred_element_type=F32) + b_ref[...]


def _modulation(c, w_ada, b_ada):
    bsz, d = c.shape
    n = w_ada.shape[1]
    tn = d
    return pl.pallas_call(
        _mod_kernel,
        grid=(n // tn,),
        in_specs=[pl.BlockSpec((bsz, d), lambda j: (0, 0)),
                  pl.BlockSpec((d, tn), lambda j: (0, j)),
                  pl.BlockSpec((1, tn), lambda j: (0, j))],
        out_specs=pl.BlockSpec((bsz, tn), lambda j: (0, j)),
        out_shape=jax.ShapeDtypeStruct((bsz, n), F32),
        compiler_params=pltpu.CompilerParams(vmem_limit_bytes=VMEM_LIMIT),
    )(c, w_ada, b_ada.reshape(1, n))


W_Q, W_K, W_V, W_GA, W_CB, W_CC, W_CX, W_GC, W_MA, W_MC = range(N_PROJ)
P_Q, P_K, P_V, P_GA, P_ZC, P_MA, P_MC = range(7)
CONV_HALO = 8


def _inproj_kernel(x_ref, mod_ref, g_ref, cw_ref, w_ref, o_ref, h_ref, u_ref, t_ref, *, q_scale,
                   row_chunk):
    s = pl.program_id(1)
    ts, d = x_ref.shape[1], x_ref.shape[2]
    shift = mod_ref[0, 0:1, :]
    scale1 = 1.0 + mod_ref[0, 1:2, :]
    g = g_ref[...]

    def body(r, carry):
        rows = pl.ds(pl.multiple_of(r * row_chunk, row_chunk), row_chunk)
        xr = x_ref[0, rows, :]
        inv = lax.rsqrt(jnp.mean(xr * xr, axis=-1, keepdims=True) + EPS)
        h_ref[rows, :] = ((xr * inv * g) * scale1 + shift).astype(BF16)
        return carry

    lax.fori_loop(0, ts // row_chunk, body, 0)

    def proj(j):
        return jnp.dot(h_ref[...], w_ref[:, j * d:(j + 1) * d], preferred_element_type=F32)

    def put(k, val):
        o_ref[0, :, k * d:(k + 1) * d] = val.astype(o_ref.dtype)

    @pl.when(s == 0)
    def _():
        u_ref[0:CONV_HALO, :] = jnp.zeros((CONV_HALO, d), F32)

    def keep_cc(acc):
        t_ref[...] = acc

    def conv(acc):
        u_ref[CONV_HALO:CONV_HALO + ts, :] = t_ref[...] * acc
        t_ref[...] = (cw_ref[0:1, :] * u_ref[CONV_HALO - 2:CONV_HALO - 2 + ts, :]
                      + cw_ref[1:2, :] * u_ref[CONV_HALO - 1:CONV_HALO - 1 + ts, :]
                      + cw_ref[2:3, :] * u_ref[CONV_HALO:CONV_HALO + ts, :])
        u_ref[0:CONV_HALO, :] = u_ref[ts:ts + CONV_HALO, :]

    def gate_cb(acc):
        t_ref[...] = t_ref[...] * acc

    stages = [
        (W_Q, lambda acc: put(P_Q, acc * q_scale)),
        (W_K, lambda acc: put(P_K, acc)),
        (W_V, lambda acc: put(P_V, acc)),
        (W_GA, lambda acc: put(P_GA, _silu(acc))),
        (W_MA, lambda acc: put(P_MA, jax.nn.sigmoid(acc))),
        (W_MC, lambda acc: put(P_MC, jax.nn.sigmoid(acc))),
        (W_CC, keep_cc),
        (W_CX, conv),
        (W_CB, gate_cb),
        (W_GC, lambda acc: put(P_ZC, t_ref[...] * _silu(acc))),
    ]
    acc = proj(stages[0][0])
    for n, (_, epilogue) in enumerate(stages):
        nxt = proj(stages[n + 1][0]) if n + 1 < len(stages) else None
        epilogue(acc)
        acc = nxt


def _in_projection(x, mod3, norm_g, conv_w, w_in_bf16, ts=512, row_chunk=128):
    bsz, seq, d = x.shape
    n = w_in_bf16.shape[1]
    n_out = 7 * d
    kern = functools.partial(_inproj_kernel, q_scale=LOG2E * HEAD_DIM ** -0.5, row_chunk=row_chunk)
    return pl.pallas_call(
        kern,
        grid=(bsz, seq // ts),
        in_specs=[pl.BlockSpec((1, ts, d), lambda b, s: (b, s, 0)),
                  pl.BlockSpec((1, 3, d), lambda b, s: (b, 0, 0)),
                  pl.BlockSpec((1, d), lambda b, s: (0, 0)),
                  pl.BlockSpec((CONV_K, d), lambda b, s: (0, 0)),
                  pl.BlockSpec((d, n), lambda b, s: (0, 0), pipeline_mode=pl.Buffered(1))],
        out_specs=pl.BlockSpec((1, ts, n_out), lambda b, s: (b, s, 0)),
        out_shape=jax.ShapeDtypeStruct((bsz, seq, n_out), BF16),
        scratch_shapes=[pltpu.VMEM((ts, d), BF16), pltpu.VMEM((CONV_HALO + ts, d), F32),
                        pltpu.VMEM((ts, d), F32)],
        compiler_params=pltpu.CompilerParams(
            dimension_semantics=("arbitrary", "arbitrary"),
            vmem_limit_bytes=VMEM_LIMIT),
    )(x, mod3, norm_g.reshape(1, d), conv_w, w_in_bf16)


def _t5_bucket(dist):
    n = jnp.maximum(dist, 0)
    max_exact = N_BUCKETS // 2
    nf = jnp.maximum(n, 1).astype(F32)
    large = max_exact + (jnp.log(nf / max_exact) / math.log(MAX_DISTANCE / max_exact)
                         * (N_BUCKETS - max_exact)).astype(jnp.int32)
    large = jnp.minimum(large, N_BUCKETS - 1)
    return jnp.where(n < max_exact, n, large)


def _table_kernel(rb_ref, o_ref):
    h = pl.program_id(0)
    blk = o_ref.shape[1]
    r = lax.broadcasted_iota(jnp.int32, (blk, blk), 0)
    c = lax.broadcasted_iota(jnp.int32, (blk, blk), 1)
    for t in range(2):
        dist = r - c + t * blk
        bucket = _t5_bucket(dist)
        bias = jnp.zeros((blk, blk), F32)
        for b in range(N_BUCKETS):
            bias = jnp.where(bucket == b, rb_ref[b, h] * LOG2E, bias)
        if t == 0:
            bias = jnp.where(dist >= 0, bias, NEG)
        o_ref[0, :, (1 - t) * blk:(2 - t) * blk] = bias


def _bias_tables(rel_bias):
    return pl.pallas_call(
        _table_kernel,
        grid=(N_HEADS,),
        in_specs=[pl.BlockSpec(memory_space=pltpu.SMEM)],
        out_specs=pl.BlockSpec((1, MOBA_BLOCK, 2 * MOBA_BLOCK), lambda h: (h, 0, 0)),
        out_shape=jax.ShapeDtypeStruct((N_HEADS, MOBA_BLOCK, 2 * MOBA_BLOCK), F32),
        compiler_params=pltpu.CompilerParams(vmem_limit_bytes=VMEM_LIMIT),
    )(rel_bias)


def _attn_kernel(rb_ref, q_ref, k_ref, v_ref, tab_ref, o_ref, qaug_ref, kaug_ref, kmean_ref, *, nb):
    h = pl.program_id(1)
    blk = MOBA_BLOCK
    hd = HEAD_DIM
    seq = nb * blk
    nt = (((1,), (1,)), ((), ()))

    lane = lax.broadcasted_iota(jnp.int32, (blk, LANES), 1)
    for n in range(nb):
        rows = slice(n * blk, (n + 1) * blk)
        kb = k_ref[0, rows, :]
        kaug_ref[rows, 0:hd] = kb
        kaug_ref[rows, hd:hd + LANES] = jnp.where((lane == n) | (lane == nb + n), 1.0, 0.0).astype(BF16)
        kmean_ref[n:n + 1, :] = jnp.mean(kb.astype(F32), axis=0, keepdims=True)

    km = kmean_ref[...]
    km_hi = km.astype(BF16)
    rem = km - km_hi.astype(F32)
    km_mid = rem.astype(BF16)
    km_lo = (rem - km_mid.astype(F32)).astype(BF16)
    sc3 = lax.dot_general(jnp.concatenate([km_hi, km_mid, km_lo], axis=0), q_ref[0], nt,
                          preferred_element_type=F32)
    sc = (sc3[0:nb] + sc3[nb:2 * nb]) + sc3[2 * nb:3 * nb]

    n_iota = lax.broadcasted_iota(jnp.int32, (nb, seq), 0)
    q_blk = lax.shift_right_logical(lax.broadcasted_iota(jnp.int32, (nb, seq), 1),
                                    int(math.log2(blk)))
    rank = jnp.zeros((nb, seq), jnp.int32)
    for m in range(nb):
        row = sc[m:m + 1, :]
        beats = (row > sc) | ((row == sc) & (m < n_iota))
        rank = rank + jnp.where(beats & (m < q_blk), 1, 0)
    past = n_iota < q_blk
    chosen = past & (rank < MOBA_TOPK)
    far = chosen & (n_iota < q_blk - 1)

    b_far = jnp.full((nb, seq), rb_ref[N_BUCKETS - 1, h] * LOG2E, F32)
    b_hi = b_far.astype(BF16).astype(F32)
    add_hi = jnp.where(past, jnp.where(chosen, jnp.where(far, b_hi, 0.0), NEG), 0.0)
    add_lo = jnp.where(far, b_far - b_hi, 0.0)
    add_t = jnp.concatenate([add_hi, add_lo, jnp.zeros((LANES - 2 * nb, seq), F32)], axis=0)
    qaug_ref[:, 0:hd] = q_ref[0]
    qaug_ref[:, hd:hd + LANES] = add_t.T.astype(BF16)

    def logits(i):
        n_keys = (i + 1) * blk
        s = lax.dot_general(qaug_ref[i * blk:(i + 1) * blk, :], kaug_ref[0:n_keys, :], nt,
                            preferred_element_type=F32)
        if i == 0:
            return s + tab_ref[0, :, blk:2 * blk]
        if i == 1:
            return s + tab_ref[0]
        return jnp.concatenate([s[:, :n_keys - 2 * blk], s[:, n_keys - 2 * blk:] + tab_ref[0]], axis=1)

    s = logits(0)
    for i in range(nb):
        s_next = logits(i + 1) if i + 1 < nb else None
        mx = jnp.max(s, axis=-1, keepdims=True)
        p = jnp.exp2(s - mx)
        den = jnp.sum(p, axis=-1, keepdims=True)
        acc = jnp.dot(p.astype(BF16), v_ref[0, 0:(i + 1) * blk, :], preferred_element_type=F32)
        o_ref[0, i * blk:(i + 1) * blk, :] = (acc / den).astype(o_ref.dtype)
        s = s_next


def _moba_attention(proj, tables, rel_bias, d_model):
    bsz, seq, _ = proj.shape
    nb = seq // MOBA_BLOCK
    nh = d_model // HEAD_DIM
    assert MAX_DISTANCE <= MOBA_BLOCK + 1 and 2 * nb <= LANES and MOBA_BLOCK & (MOBA_BLOCK - 1) == 0
    kern = functools.partial(_attn_kernel, nb=nb)

    def head_cols(first):
        return pl.BlockSpec((1, seq, HEAD_DIM), lambda b, h: (b, 0, first + h))

    return pl.pallas_call(
        kern,
        grid=(bsz, nh),
        in_specs=[pl.BlockSpec(memory_space=pltpu.SMEM),
                  head_cols(0), head_cols(nh), head_cols(2 * nh),
                  pl.BlockSpec((1, MOBA_BLOCK, 2 * MOBA_BLOCK), lambda b, h: (h, 0, 0))],
        out_specs=head_cols(0),
        out_shape=jax.ShapeDtypeStruct((bsz, seq, d_model), BF16),
        scratch_shapes=[pltpu.VMEM((seq, HEAD_DIM + LANES), BF16),
                        pltpu.VMEM((seq, HEAD_DIM + LANES), BF16),
                        pltpu.VMEM((nb, HEAD_DIM), F32)],
        compiler_params=pltpu.CompilerParams(
            dimension_semantics=("arbitrary", "arbitrary"),
            vmem_limit_bytes=VMEM_LIMIT),
    )(rel_bias, proj, proj, proj, tables)


def _out_kernel(attn_ref, ga_ref, zc_ref, ma_ref, mc_ref, x_ref, mod_ref, woa_ref, woc_ref, wout_ref,
                fg_ref, o_ref):
    y_conv = jnp.dot(zc_ref[0], woc_ref[...], preferred_element_type=F32)
    z_attn = attn_ref[0].astype(F32) * ga_ref[0].astype(F32)
    y_attn = jnp.dot(z_attn.astype(BF16), woa_ref[...], preferred_element_type=F32)
    merged = ma_ref[0].astype(F32) * y_attn + mc_ref[0].astype(F32) * y_conv
    branch = jnp.dot(merged.astype(BF16), wout_ref[...], preferred_element_type=F32)
    res = x_ref[0] + mod_ref[0, 2:3, :] * branch
    inv = lax.rsqrt(jnp.mean(res * res, axis=-1, keepdims=True) + EPS)
    o_ref[0] = res * inv * fg_ref[...]


def _output_stage(attn, proj, x, mod3, woa, woc, wout, final_g, ts=512):
    bsz, seq, d = x.shape

    def col(k):
        return pl.BlockSpec((1, ts, d), lambda b, s: (b, s, k))

    def whole(shape):
        return pl.BlockSpec(shape, lambda b, s: (0,) * len(shape))

    return pl.pallas_call(
        _out_kernel,
        grid=(bsz, seq // ts),
        in_specs=[col(0), col(P_GA), col(P_ZC), col(P_MA), col(P_MC),
                  col(0), pl.BlockSpec((1, 3, d), lambda b, s: (b, 0, 0)),
                  whole((d, d)), whole((d, d)), whole((d, d)), whole((1, d))],
        out_specs=col(0),
        out_shape=jax.ShapeDtypeStruct((bsz, seq, d), F32),
        compiler_params=pltpu.CompilerParams(
            dimension_semantics=("arbitrary", "arbitrary"),
            vmem_limit_bytes=VMEM_LIMIT),
    )(attn, proj, proj, proj, proj, x, mod3, woa, woc, wout, final_g.reshape(1, d))


def kernel(x, c, norm_g, w_ada, b_ada, w_in, conv_w, w_o_attn, w_o_conv, w_out, rel_bias, final_g):
    bsz, seq, d = x.shape
    depth = norm_g.shape[0]
    assert depth == 1, "the fused output stage applies the final RMSNorm after the single layer"
    assert d == N_HEADS * HEAD_DIM and w_in.shape[2] == N_PROJ * d and seq % MOBA_BLOCK == 0
    tables = _bias_tables(rel_bias)
    mod3 = _modulation(c, w_ada[0], b_ada[0]).reshape(bsz, 3, d)
    proj = _in_projection(x, mod3, norm_g[0], conv_w[0], w_in[0].astype(BF16))
    attn = _moba_attention(proj, tables, rel_bias, d)
    return _output_stage(attn, proj, x, mod3, w_o_attn[0].astype(BF16),
                         w_o_conv[0].astype(BF16), w_out[0].astype(BF16), final_g)
```

```python
import functools
import math

import jax
import jax.numpy as jnp
from jax import lax
from jax.experimental import pallas as pl
from jax.experimental.pallas import tpu as pltpu

N_HEADS = 8
HEAD_DIM = 128
CONV_K = 3
MOBA_BLOCK = 256
MOBA_TOPK = 3
N_BUCKETS = 32
MAX_DISTANCE = 128
EPS = 1e-6
N_PROJ = 10

LANES = 128
NEG = -1e30
LOG2E = math.log2(math.e)
VMEM_LIMIT = 56 * 1024 * 1024

F32 = jnp.float32
BF16 = jnp.bfloat16


def _silu(v):
    return v * jax.nn.sigmoid(v)


def _mod_kernel(c_ref, w_ref, b_ref, o_ref):
    a = _silu(c_ref[...])
    o_ref[...] = jnp.dot(a, w_ref[...], precision=lax.Precision.HIGHEST,
                         preferred_element_type=F32) + b_ref[...]


def _modulation(c, w_ada, b_ada):
    bsz, d = c.shape
    n = w_ada.shape[1]
    tn = d
    return pl.pallas_call(
        _mod_kernel,
        grid=(n // tn,),
        in_specs=[pl.BlockSpec((bsz, d), lambda j: (0, 0)),
                  pl.BlockSpec((d, tn), lambda j: (0, j)),
                  pl.BlockSpec((1, tn), lambda j: (0, j))],
        out_specs=pl.BlockSpec((bsz, tn), lambda j: (0, j)),
        out_shape=jax.ShapeDtypeStruct((bsz, n), F32),
        compiler_params=pltpu.CompilerParams(vmem_limit_bytes=VMEM_LIMIT),
    )(c, w_ada, b_ada.reshape(1, n))


W_Q, W_K, W_V, W_GA, W_CB, W_CC, W_CX, W_GC, W_MA, W_MC = range(N_PROJ)
P_Q, P_K, P_V, P_GA, P_ZC, P_MA, P_MC = range(7)
CONV_HALO = 8


def _inproj_kernel(x_ref, mod_ref, g_ref, cw_ref, w_ref, o_ref, h_ref, u_ref, t_ref, *, q_scale,
                   row_chunk):
    s = pl.program_id(1)
    ts, d = x_ref.shape[1], x_ref.shape[2]
    shift = mod_ref[0, 0:1, :]
    scale1 = 1.0 + mod_ref[0, 1:2, :]
    g = g_ref[...]

    def body(r, carry):
        rows = pl.ds(pl.multiple_of(r * row_chunk, row_chunk), row_chunk)
        xr = x_ref[0, rows, :]
        inv = lax.rsqrt(jnp.mean(xr * xr, axis=-1, keepdims=True) + EPS)
        h_ref[rows, :] = ((xr * inv * g) * scale1 + shift).astype(BF16)
        return carry

    lax.fori_loop(0, ts // row_chunk, body, 0)

    def proj(j):
        return jnp.dot(h_ref[...], w_ref[:, j * d:(j + 1) * d], preferred_element_type=F32)

    def put(k, val):
        o_ref[0, :, k * d:(k + 1) * d] = val.astype(o_ref.dtype)

    @pl.when(s == 0)
    def _():
        u_ref[0:CONV_HALO, :] = jnp.zeros((CONV_HALO, d), F32)

    def keep_cc(acc):
        t_ref[...] = acc

    def conv(acc):
        u_ref[CONV_HALO:CONV_HALO + ts, :] = t_ref[...] * acc
        t_ref[...] = (cw_ref[0:1, :] * u_ref[CONV_HALO - 2:CONV_HALO - 2 + ts, :]
                      + cw_ref[1:2, :] * u_ref[CONV_HALO - 1:CONV_HALO - 1 + ts, :]
                      + cw_ref[2:3, :] * u_ref[CONV_HALO:CONV_HALO + ts, :])
        u_ref[0:CONV_HALO, :] = u_ref[ts:ts + CONV_HALO, :]

    def gate_cb(acc):
        t_ref[...] = t_ref[...] * acc

    stages = [
        (W_Q, lambda acc: put(P_Q, acc * q_scale)),
        (W_K, lambda acc: put(P_K, acc)),
        (W_V, lambda acc: put(P_V, acc)),
        (W_GA, lambda acc: put(P_GA, _silu(acc))),
        (W_MA, lambda acc: put(P_MA, jax.nn.sigmoid(acc))),
        (W_MC, lambda acc: put(P_MC, jax.nn.sigmoid(acc))),
        (W_CC, keep_cc),
        (W_CX, conv),
        (W_CB, gate_cb),
        (W_GC, lambda acc: put(P_ZC, t_ref[...] * _silu(acc))),
    ]
    acc = proj(stages[0][0])
    for n, (_, epilogue) in enumerate(stages):
        nxt = proj(stages[n + 1][0]) if n + 1 < len(stages) else None
        epilogue(acc)
        acc = nxt


def _in_projection(x, mod3, norm_g, conv_w, w_in_bf16, ts=512, row_chunk=128):
    bsz, seq, d = x.shape
    n = w_in_bf16.shape[1]
    n_out = 7 * d
    kern = functools.partial(_inproj_kernel, q_scale=LOG2E * HEAD_DIM ** -0.5, row_chunk=row_chunk)
    return pl.pallas_call(
        kern,
        grid=(bsz, seq // ts),
        in_specs=[pl.BlockSpec((1, ts, d), lambda b, s: (b, s, 0)),
                  pl.BlockSpec((1, 3, d), lambda b, s: (b, 0, 0)),
                  pl.BlockSpec((1, d), lambda b, s: (0, 0)),
                  pl.BlockSpec((CONV_K, d), lambda b, s: (0, 0)),
                  pl.BlockSpec((d, n), lambda b, s: (0, 0), pipeline_mode=pl.Buffered(1))],
        out_specs=pl.BlockSpec((1, ts, n_out), lambda b, s: (b, s, 0)),
        out_shape=jax.ShapeDtypeStruct((bsz, seq, n_out), BF16),
        scratch_shapes=[pltpu.VMEM((ts, d), BF16), pltpu.VMEM((CONV_HALO + ts, d), F32),
                        pltpu.VMEM((ts, d), F32)],
        compiler_params=pltpu.CompilerParams(
            dimension_semantics=("arbitrary", "arbitrary"),
            vmem_limit_bytes=VMEM_LIMIT),
    )(x, mod3, norm_g.reshape(1, d), conv_w, w_in_bf16)


def _t5_bucket(dist):
    n = jnp.maximum(dist, 0)
    max_exact = N_BUCKETS // 2
    nf = jnp.maximum(n, 1).astype(F32)
    large = max_exact + (jnp.log(nf / max_exact) / math.log(MAX_DISTANCE / max_exact)
                         * (N_BUCKETS - max_exact)).astype(jnp.int32)
    large = jnp.minimum(large, N_BUCKETS - 1)
    return jnp.where(n < max_exact, n, large)


def _table_kernel(rb_ref, o_ref):
    h = pl.program_id(0)
    blk = o_ref.shape[1]
    r = lax.broadcasted_iota(jnp.int32, (blk, blk), 0)
    c = lax.broadcasted_iota(jnp.int32, (blk, blk), 1)
    for t in range(2):
        dist = r - c + t * blk
        bucket = _t5_bucket(dist)
        bias = jnp.zeros((blk, blk), F32)
        for b in range(N_BUCKETS):
            bias = jnp.where(bucket == b, rb_ref[b, h] * LOG2E, bias)
        if t == 0:
            bias = jnp.where(dist >= 0, bias, NEG)
        o_ref[0, :, (1 - t) * blk:(2 - t) * blk] = bias


def _bias_tables(rel_bias):
    return pl.pallas_call(
        _table_kernel,
        grid=(N_HEADS,),
        in_specs=[pl.BlockSpec(memory_space=pltpu.SMEM)],
        out_specs=pl.BlockSpec((1, MOBA_BLOCK, 2 * MOBA_BLOCK), lambda h: (h, 0, 0)),
        out_shape=jax.ShapeDtypeStruct((N_HEADS, MOBA_BLOCK, 2 * MOBA_BLOCK), F32),
        compiler_params=pltpu.CompilerParams(vmem_limit_bytes=VMEM_LIMIT),
    )(rel_bias)


def _attn_kernel(rb_ref, q_ref, k_ref, v_ref, tab_ref, o_ref, qaug_ref, kaug_ref, vaug_ref, kmean_ref,
                 *, nb):
    h = pl.program_id(1)
    blk = MOBA_BLOCK
    hd = HEAD_DIM
    seq = nb * blk
    nt = (((1,), (1,)), ((), ()))

    lane = lax.broadcasted_iota(jnp.int32, (blk, LANES), 1)
    for n in range(nb):
        rows = slice(n * blk, (n + 1) * blk)
        kb = k_ref[0, rows, :]
        kaug_ref[rows, 0:hd] = kb
        kaug_ref[rows, hd:hd + LANES] = jnp.where((lane == n) | (lane == nb + n), 1.0, 0.0).astype(BF16)
        kmean_ref[n:n + 1, :] = jnp.mean(kb.astype(F32), axis=0, keepdims=True)
    vaug_ref[:, 0:hd] = v_ref[0]
    vaug_ref[:, hd:2 * hd] = jnp.ones((seq, hd), BF16)

    km = kmean_ref[...]
    km_hi = km.astype(BF16)
    rem = km - km_hi.astype(F32)
    km_mid = rem.astype(BF16)
    km_lo = (rem - km_mid.astype(F32)).astype(BF16)
    sc3 = lax.dot_general(jnp.concatenate([km_hi, km_mid, km_lo], axis=0), q_ref[0], nt,
                          preferred_element_type=F32)
    sc = (sc3[0:nb] + sc3[nb:2 * nb]) + sc3[2 * nb:3 * nb]

    n_iota = lax.broadcasted_iota(jnp.int32, (nb, seq), 0)
    q_blk = lax.shift_right_logical(lax.broadcasted_iota(jnp.int32, (nb, seq), 1),
                                    int(math.log2(blk)))
    rank = jnp.zeros((nb, seq), jnp.int32)
    for m in range(nb):
        row = sc[m:m + 1, :]
        beats = (row > sc) | ((row == sc) & (m < n_iota))
        rank = rank + jnp.where(beats & (m < q_blk), 1, 0)
    past = n_iota < q_blk
    chosen = past & (rank < MOBA_TOPK)
    far = chosen & (n_iota < q_blk - 1)

    b_far = jnp.full((nb, seq), rb_ref[N_BUCKETS - 1, h] * LOG2E, F32)
    b_hi = b_far.astype(BF16).astype(F32)
    add_hi = jnp.where(past, jnp.where(chosen, jnp.where(far, b_hi, 0.0), NEG), 0.0)
    add_lo = jnp.where(far, b_far - b_hi, 0.0)
    add_t = jnp.concatenate([add_hi, add_lo, jnp.zeros((LANES - 2 * nb, seq), F32)], axis=0)
    qaug_ref[:, 0:hd] = q_ref[0]
    qaug_ref[:, hd:hd + LANES] = add_t.T.astype(BF16)

    def logits(i):
        n_keys = (i + 1) * blk
        s = lax.dot_general(qaug_ref[i * blk:(i + 1) * blk, :], kaug_ref[0:n_keys, :], nt,
                            preferred_element_type=F32)
        if i == 0:
            return s + tab_ref[0, :, blk:2 * blk]
        if i == 1:
            return s + tab_ref[0]
        return jnp.concatenate([s[:, :n_keys - 2 * blk], s[:, n_keys - 2 * blk:] + tab_ref[0]], axis=1)

    s = logits(0)
    for i in range(nb):
        s_next = logits(i + 1) if i + 1 < nb else None
        mx = jnp.max(s, axis=-1, keepdims=True)
        p = jnp.exp2(s - mx)
        pv = jnp.dot(p.astype(BF16), vaug_ref[0:(i + 1) * blk, :], preferred_element_type=F32)
        o_ref[0, i * blk:(i + 1) * blk, :] = (pv[:, 0:hd] / pv[:, hd:2 * hd]).astype(o_ref.dtype)
        s = s_next


def _moba_attention(proj, tables, rel_bias, d_model):
    bsz, seq, _ = proj.shape
    nb = seq // MOBA_BLOCK
    nh = d_model // HEAD_DIM
    assert MAX_DISTANCE <= MOBA_BLOCK + 1 and 2 * nb <= LANES and MOBA_BLOCK & (MOBA_BLOCK - 1) == 0
    kern = functools.partial(_attn_kernel, nb=nb)

    def head_cols(first):
        return pl.BlockSpec((1, seq, HEAD_DIM), lambda b, h: (b, 0, first + h))

    return pl.pallas_call(
        kern,
        grid=(bsz, nh),
        in_specs=[pl.BlockSpec(memory_space=pltpu.SMEM),
                  head_cols(0), head_cols(nh), head_cols(2 * nh),
                  pl.BlockSpec((1, MOBA_BLOCK, 2 * MOBA_BLOCK), lambda b, h: (h, 0, 0))],
        out_specs=head_cols(0),
        out_shape=jax.ShapeDtypeStruct((bsz, seq, d_model), BF16),
        scratch_shapes=[pltpu.VMEM((seq, HEAD_DIM + LANES), BF16),
                        pltpu.VMEM((seq, HEAD_DIM + LANES), BF16),
                        pltpu.VMEM((seq, 2 * HEAD_DIM), BF16),
                        pltpu.VMEM((nb, HEAD_DIM), F32)],
        compiler_params=pltpu.CompilerParams(
            dimension_semantics=("arbitrary", "arbitrary"),
            vmem_limit_bytes=VMEM_LIMIT),
    )(rel_bias, proj, proj, proj, tables)


def _out_kernel(attn_ref, ga_ref, zc_ref, ma_ref, mc_ref, x_ref, mod_ref, woa_ref, woc_ref, wout_ref,
                fg_ref, o_ref):
    y_conv = jnp.dot(zc_ref[0], woc_ref[...], preferred_element_type=F32)
    z_attn = attn_ref[0].astype(F32) * ga_ref[0].astype(F32)
    y_attn = jnp.dot(z_attn.astype(BF16), woa_ref[...], preferred_element_type=F32)
    merged = ma_ref[0].astype(F32) * y_attn + mc_ref[0].astype(F32) * y_conv
    branch = jnp.dot(merged.astype(BF16), wout_ref[...], preferred_element_type=F32)
    res = x_ref[0] + mod_ref[0, 2:3, :] * branch
    inv = lax.rsqrt(jnp.mean(res * res, axis=-1, keepdims=True) + EPS)
    o_ref[0] = res * inv * fg_ref[...]


def _output_stage(attn, proj, x, mod3, woa, woc, wout, final_g, ts=512):
    bsz, seq, d = x.shape

    def col(k):
        return pl.BlockSpec((1, ts, d), lambda b, s: (b, s, k))

    def whole(shape):
        return pl.BlockSpec(shape, lambda b, s: (0,) * len(shape))

    return pl.pallas_call(
        _out_kernel,
        grid=(bsz, seq // ts),
        in_specs=[col(0), col(P_GA), col(P_ZC), col(P_MA), col(P_MC),
                  col(0), pl.BlockSpec((1, 3, d), lambda b, s: (b, 0, 0)),
                  whole((d, d)), whole((d, d)), whole((d, d)), whole((1, d))],
        out_specs=col(0),
        out_shape=jax.ShapeDtypeStruct((bsz, seq, d), F32),
        compiler_params=pltpu.CompilerParams(
            dimension_semantics=("arbitrary", "arbitrary"),
            vmem_limit_bytes=VMEM_LIMIT),
    )(attn, proj, proj, proj, proj, x, mod3, woa, woc, wout, final_g.reshape(1, d))


def kernel(x, c, norm_g, w_ada, b_ada, w_in, conv_w, w_o_attn, w_o_conv, w_out, rel_bias, final_g):
    bsz, seq, d = x.shape
    depth = norm_g.shape[0]
    assert depth == 1, "the fused output stage applies the final RMSNorm after the single layer"
    assert d == N_HEADS * HEAD_DIM and w_in.shape[2] == N_PROJ * d and seq % MOBA_BLOCK == 0
    tables = _bias_tables(rel_bias)
    mod3 = _modulation(c, w_ada[0], b_ada[0]).reshape(bsz, 3, d)
    proj = _in_projection(x, mod3, norm_g[0], conv_w[0], w_in[0].astype(BF16))
    attn = _moba_attention(proj, tables, rel_bias, d)
    return _output_stage(attn, proj, x, mod3, w_o_attn[0].astype(BF16),
                         w_o_conv[0].astype(BF16), w_out[0].astype(BF16), final_g)
```

```python
import functools
import math

import jax
import jax.numpy as jnp
from jax import lax
from jax.experimental import pallas as pl
from jax.experimental.pallas import tpu as pltpu

N_HEADS = 8
HEAD_DIM = 128
CONV_K = 3
MOBA_BLOCK = 256
MOBA_TOPK = 3
N_BUCKETS = 32
MAX_DISTANCE = 128
EPS = 1e-6
N_PROJ = 10

LANES = 128
NEG = -1e30
LOG2E = math.log2(math.e)
VMEM_LIMIT = 56 * 1024 * 1024

F32 = jnp.float32
BF16 = jnp.bfloat16


def _silu(v):
    return v * jax.nn.sigmoid(v)


def _mod_kernel(c_ref, w_ref, b_ref, o_ref):
    a = _silu(c_ref[...])
    o_ref[...] = jnp.dot(a, w_ref[...], precision=lax.Precision.HIGHEST,
                         preferred_element_type=F32) + b_ref[...]


def _modulation(c, w_ada, b_ada):
    bsz, d = c.shape
    n = w_ada.shape[1]
    tn = d
    return pl.pallas_call(
        _mod_kernel,
        grid=(n // tn,),
        in_specs=[pl.BlockSpec((bsz, d), lambda j: (0, 0)),
                  pl.BlockSpec((d, tn), lambda j: (0, j)),
                  pl.BlockSpec((1, tn), lambda j: (0, j))],
        out_specs=pl.BlockSpec((bsz, tn), lambda j: (0, j)),
        out_shape=jax.ShapeDtypeStruct((bsz, n), F32),
        compiler_params=pltpu.CompilerParams(vmem_limit_bytes=VMEM_LIMIT),
    )(c, w_ada, b_ada.reshape(1, n))


W_Q, W_K, W_V, W_GA, W_CB, W_CC, W_CX, W_GC, W_MA, W_MC = range(N_PROJ)
P_Q, P_K, P_V, P_GA, P_ZC, P_MA, P_MC = range(7)
CONV_HALO = 8


def _inproj_kernel(x0_ref, xn_ref, mod0_ref, modn_ref, g_ref, cw_ref, w_ref, o_ref, h_ref, u_ref,
                   t_ref, *, q_scale, row_chunk):
    s = pl.program_id(1)
    step = pl.program_id(0) * pl.num_programs(1) + s
    slot = lax.rem(step, 2)
    ts, d = xn_ref.shape[1], xn_ref.shape[2]
    n_chunks = ts // row_chunk
    g = g_ref[...]

    def normalise(x_ref, mod_ref, dst_slot, r):
        rows = slice(r * row_chunk, (r + 1) * row_chunk)
        xr = x_ref[0, rows, :]
        inv = lax.rsqrt(jnp.mean(xr * xr, axis=-1, keepdims=True) + EPS)
        hr = (xr * inv * g) * (1.0 + mod_ref[0, 1:2, :]) + mod_ref[0, 0:1, :]
        h_ref[dst_slot, rows, :] = hr.astype(BF16)

    @pl.when(step == 0)
    def _():
        for r in range(n_chunks):
            normalise(x0_ref, mod0_ref, 0, r)

    def proj(j):
        return jnp.dot(h_ref[slot], w_ref[:, j * d:(j + 1) * d], preferred_element_type=F32)

    def put(k, val):
        o_ref[0, :, k * d:(k + 1) * d] = val.astype(o_ref.dtype)

    @pl.when(s == 0)
    def _():
        u_ref[0:CONV_HALO, :] = jnp.zeros((CONV_HALO, d), F32)

    def keep_cc(acc):
        t_ref[...] = acc

    def conv(acc):
        u_ref[CONV_HALO:CONV_HALO + ts, :] = t_ref[...] * acc
        t_ref[...] = (cw_ref[0:1, :] * u_ref[CONV_HALO - 2:CONV_HALO - 2 + ts, :]
                      + cw_ref[1:2, :] * u_ref[CONV_HALO - 1:CONV_HALO - 1 + ts, :]
                      + cw_ref[2:3, :] * u_ref[CONV_HALO:CONV_HALO + ts, :])
        u_ref[0:CONV_HALO, :] = u_ref[ts:ts + CONV_HALO, :]

    def gate_cb(acc):
        t_ref[...] = t_ref[...] * acc

    stages = [
        (W_Q, lambda acc: put(P_Q, acc * q_scale)),
        (W_K, lambda acc: put(P_K, acc)),
        (W_V, lambda acc: put(P_V, acc)),
        (W_GA, lambda acc: put(P_GA, _silu(acc))),
        (W_MA, lambda acc: put(P_MA, jax.nn.sigmoid(acc))),
        (W_MC, lambda acc: put(P_MC, jax.nn.sigmoid(acc))),
        (W_CC, keep_cc),
        (W_CX, conv),
        (W_CB, gate_cb),
        (W_GC, lambda acc: put(P_ZC, t_ref[...] * _silu(acc))),
    ]
    acc = proj(stages[0][0])
    for n, (_, epilogue) in enumerate(stages):
        nxt = proj(stages[n + 1][0]) if n + 1 < len(stages) else None
        epilogue(acc)
        if n < n_chunks:
            normalise(xn_ref, modn_ref, 1 - slot, n)
        acc = nxt


def _in_projection(x, mod3, norm_g, conv_w, w_in_bf16, ts=512, row_chunk=128):
    bsz, seq, d = x.shape
    n = w_in_bf16.shape[1]
    n_out = 7 * d
    n_s = seq // ts
    assert ts // row_chunk <= N_PROJ
    kern = functools.partial(_inproj_kernel, q_scale=LOG2E * HEAD_DIM ** -0.5, row_chunk=row_chunk)

    def next_tile(b, s):
        flat = jnp.minimum(b * n_s + s + 1, bsz * n_s - 1)
        return flat // n_s, flat % n_s

    return pl.pallas_call(
        kern,
        grid=(bsz, n_s),
        in_specs=[pl.BlockSpec((1, ts, d), lambda b, s: (0, 0, 0), pipeline_mode=pl.Buffered(1)),
                  pl.BlockSpec((1, ts, d), lambda b, s: (*next_tile(b, s), 0)),
                  pl.BlockSpec((1, 3, d), lambda b, s: (0, 0, 0)),
                  pl.BlockSpec((1, 3, d), lambda b, s: (next_tile(b, s)[0], 0, 0)),
                  pl.BlockSpec((1, d), lambda b, s: (0, 0)),
                  pl.BlockSpec((CONV_K, d), lambda b, s: (0, 0)),
                  pl.BlockSpec((d, n), lambda b, s: (0, 0), pipeline_mode=pl.Buffered(1))],
        out_specs=pl.BlockSpec((1, ts, n_out), lambda b, s: (b, s, 0)),
        out_shape=jax.ShapeDtypeStruct((bsz, seq, n_out), BF16),
        scratch_shapes=[pltpu.VMEM((2, ts, d), BF16), pltpu.VMEM((CONV_HALO + ts, d), F32),
                        pltpu.VMEM((ts, d), F32)],
        compiler_params=pltpu.CompilerParams(
            dimension_semantics=("arbitrary", "arbitrary"),
            vmem_limit_bytes=VMEM_LIMIT),
    )(x, x, mod3, mod3, norm_g.reshape(1, d), conv_w, w_in_bf16)


def _t5_bucket(dist):
    n = jnp.maximum(dist, 0)
    max_exact = N_BUCKETS // 2
    nf = jnp.maximum(n, 1).astype(F32)
    large = max_exact + (jnp.log(nf / max_exact) / math.log(MAX_DISTANCE / max_exact)
                         * (N_BUCKETS - max_exact)).astype(jnp.int32)
    large = jnp.minimum(large, N_BUCKETS - 1)
    return jnp.where(n < max_exact, n, large)


def _table_kernel(rb_ref, o_ref):
    h = pl.program_id(0)
    blk = o_ref.shape[1]
    r = lax.broadcasted_iota(jnp.int32, (blk, blk), 0)
    c = lax.broadcasted_iota(jnp.int32, (blk, blk), 1)
    for t in range(2):
        dist = r - c + t * blk
        bucket = _t5_bucket(dist)
        bias = jnp.zeros((blk, blk), F32)
        for b in range(N_BUCKETS):
            bias = jnp.where(bucket == b, rb_ref[b, h] * LOG2E, bias)
        if t == 0:
            bias = jnp.where(dist >= 0, bias, NEG)
        o_ref[0, :, (1 - t) * blk:(2 - t) * blk] = bias


def _bias_tables(rel_bias):
    return pl.pallas_call(
        _table_kernel,
        grid=(N_HEADS,),
        in_specs=[pl.BlockSpec(memory_space=pltpu.SMEM)],
        out_specs=pl.BlockSpec((1, MOBA_BLOCK, 2 * MOBA_BLOCK), lambda h: (h, 0, 0)),
        out_shape=jax.ShapeDtypeStruct((N_HEADS, MOBA_BLOCK, 2 * MOBA_BLOCK), F32),
        compiler_params=pltpu.CompilerParams(vmem_limit_bytes=VMEM_LIMIT),
    )(rel_bias)


def _attn_kernel(rb_ref, q_ref, k_ref, v_ref, tab_ref, o_ref, qaug_ref, kaug_ref, vaug_ref, kmean_ref,
                 *, nb):
    h = pl.program_id(1)
    blk = MOBA_BLOCK
    hd = HEAD_DIM
    seq = nb * blk
    nt = (((1,), (1,)), ((), ()))

    lane = lax.broadcasted_iota(jnp.int32, (blk, LANES), 1)
    for n in range(nb):
        rows = slice(n * blk, (n + 1) * blk)
        kb = k_ref[0, rows, :]
        kaug_ref[rows, 0:hd] = kb
        kaug_ref[rows, hd:hd + LANES] = jnp.where((lane == n) | (lane == nb + n), 1.0, 0.0).astype(BF16)
        kmean_ref[n:n + 1, :] = jnp.mean(kb.astype(F32), axis=0, keepdims=True)
    vaug_ref[:, 0:hd] = v_ref[0]
    vaug_ref[:, hd:2 * hd] = jnp.ones((seq, hd), BF16)

    km = kmean_ref[...]
    km_hi = km.astype(BF16)
    rem = km - km_hi.astype(F32)
    km_mid = rem.astype(BF16)
    km_lo = (rem - km_mid.astype(F32)).astype(BF16)
    sc3 = lax.dot_general(jnp.concatenate([km_hi, km_mid, km_lo], axis=0), q_ref[0], nt,
                          preferred_element_type=F32)
    sc = (sc3[0:nb] + sc3[nb:2 * nb]) + sc3[2 * nb:3 * nb]

    n_iota = lax.broadcasted_iota(jnp.int32, (nb, seq), 0)
    q_blk = lax.shift_right_logical(lax.broadcasted_iota(jnp.int32, (nb, seq), 1),
                                    int(math.log2(blk)))
    rank = jnp.zeros((nb, seq), jnp.int32)
    for m in range(nb):
        row = sc[m:m + 1, :]
        beats = (row > sc) | ((row == sc) & (m < n_iota))
        rank = rank + jnp.where(beats & (m < q_blk), 1, 0)
    past = n_iota < q_blk
    chosen = past & (rank < MOBA_TOPK)
    far = chosen & (n_iota < q_blk - 1)

    b_far = jnp.full((nb, seq), rb_ref[N_BUCKETS - 1, h] * LOG2E, F32)
    b_hi = b_far.astype(BF16).astype(F32)
    add_hi = jnp.where(past, jnp.where(chosen, jnp.where(far, b_hi, 0.0), NEG), 0.0)
    add_lo = jnp.where(far, b_far - b_hi, 0.0)
    add_t = jnp.concatenate([add_hi, add_lo, jnp.zeros((LANES - 2 * nb, seq), F32)], axis=0)
    qaug_ref[:, 0:hd] = q_ref[0]
    qaug_ref[:, hd:hd + LANES] = add_t.T.astype(BF16)

    def logits(i):
        n_keys = (i + 1) * blk
        s = lax.dot_general(qaug_ref[i * blk:(i + 1) * blk, :], kaug_ref[0:n_keys, :], nt,
                            preferred_element_type=F32)
        if i == 0:
            return s + tab_ref[0, :, blk:2 * blk]
        if i == 1:
            return s + tab_ref[0]
        return jnp.concatenate([s[:, :n_keys - 2 * blk], s[:, n_keys - 2 * blk:] + tab_ref[0]], axis=1)

    s = logits(0)
    for i in range(nb):
        s_next = logits(i + 1) if i + 1 < nb else None
        mx = jnp.max(s, axis=-1, keepdims=True)
        p = jnp.exp2(s - mx)
        pv = jnp.dot(p.astype(BF16), vaug_ref[0:(i + 1) * blk, :], preferred_element_type=F32)
        o_ref[0, i * blk:(i + 1) * blk, :] = (pv[:, 0:hd] / pv[:, hd:2 * hd]).astype(o_ref.dtype)
        s = s_next


def _moba_attention(proj, tables, rel_bias, d_model):
    bsz, seq, _ = proj.shape
    nb = seq // MOBA_BLOCK
    nh = d_model // HEAD_DIM
    assert MAX_DISTANCE <= MOBA_BLOCK + 1 and 2 * nb <= LANES and MOBA_BLOCK & (MOBA_BLOCK - 1) == 0
    kern = functools.partial(_attn_kernel, nb=nb)

    def head_cols(first):
        return pl.BlockSpec((1, seq, HEAD_DIM), lambda b, h: (b, 0, first + h))

    return pl.pallas_call(
        kern,
        grid=(bsz, nh),
        in_specs=[pl.BlockSpec(memory_space=pltpu.SMEM),
                  head_cols(0), head_cols(nh), head_cols(2 * nh),
                  pl.BlockSpec((1, MOBA_BLOCK, 2 * MOBA_BLOCK), lambda b, h: (h, 0, 0))],
        out_specs=head_cols(0),
        out_shape=jax.ShapeDtypeStruct((bsz, seq, d_model), BF16),
        scratch_shapes=[pltpu.VMEM((seq, HEAD_DIM + LANES), BF16),
                        pltpu.VMEM((seq, HEAD_DIM + LANES), BF16),
                        pltpu.VMEM((seq, 2 * HEAD_DIM), BF16),
                        pltpu.VMEM((nb, HEAD_DIM), F32)],
        compiler_params=pltpu.CompilerParams(
            dimension_semantics=("arbitrary", "arbitrary"),
            vmem_limit_bytes=VMEM_LIMIT),
    )(rel_bias, proj, proj, proj, tables)


def _out_kernel(attn_ref, ga_ref, zc_ref, ma_ref, mc_ref, x_ref, mod_ref, woa_ref, woc_ref, wout_ref,
                fg_ref, o_ref):
    y_conv = jnp.dot(zc_ref[0], woc_ref[...], preferred_element_type=F32)
    z_attn = attn_ref[0].astype(F32) * ga_ref[0].astype(F32)
    y_attn = jnp.dot(z_attn.astype(BF16), woa_ref[...], preferred_element_type=F32)
    merged = ma_ref[0].astype(F32) * y_attn + mc_ref[0].astype(F32) * y_conv
    branch = jnp.dot(merged.astype(BF16), wout_ref[...], preferred_element_type=F32)
    res = x_ref[0] + mod_ref[0, 2:3, :] * branch
    inv = lax.rsqrt(jnp.mean(res * res, axis=-1, keepdims=True) + EPS)
    o_ref[0] = res * inv * fg_ref[...]


def _output_stage(attn, proj, x, mod3, woa, woc, wout, final_g, ts=512):
    bsz, seq, d = x.shape

    def col(k):
        return pl.BlockSpec((1, ts, d), lambda b, s: (b, s, k))

    def whole(shape):
        return pl.BlockSpec(shape, lambda b, s: (0,) * len(shape))

    return pl.pallas_call(
        _out_kernel,
        grid=(bsz, seq // ts),
        in_specs=[col(0), col(P_GA), col(P_ZC), col(P_MA), col(P_MC),
                  col(0), pl.BlockSpec((1, 3, d), lambda b, s: (b, 0, 0)),
                  whole((d, d)), whole((d, d)), whole((d, d)), whole((1, d))],
        out_specs=col(0),
        out_shape=jax.ShapeDtypeStruct((bsz, seq, d), F32),
        compiler_params=pltpu.CompilerParams(
            dimension_semantics=("arbitrary", "arbitrary"),
            vmem_limit_bytes=VMEM_LIMIT),
    )(attn, proj, proj, proj, proj, x, mod3, woa, woc, wout, final_g.reshape(1, d))


def kernel(x, c, norm_g, w_ada, b_ada, w_in, conv_w, w_o_attn, w_o_conv, w_out, rel_bias, final_g):
    bsz, seq, d = x.shape
    depth = norm_g.shape[0]
    assert depth == 1, "the fused output stage applies the final RMSNorm after the single layer"
    assert d == N_HEADS * HEAD_DIM and w_in.shape[2] == N_PROJ * d and seq % MOBA_BLOCK == 0
    tables = _bias_tables(rel_bias)
    mod3 = _modulation(c, w_ada[0], b_ada[0]).reshape(bsz, 3, d)
    proj = _in_projection(x, mod3, norm_g[0], conv_w[0], w_in[0].astype(BF16))
    attn = _moba_attention(proj, tables, rel_bias, d)
    return _output_stage(attn, proj, x, mod3, w_o_attn[0].astype(BF16),
                         w_o_conv[0].astype(BF16), w_out[0].astype(BF16), final_g)
```

```python
import functools
import math

import jax
import jax.numpy as jnp
from jax import lax
from jax.experimental import pallas as pl
from jax.experimental.pallas import tpu as pltpu

N_HEADS = 8
HEAD_DIM = 128
CONV_K = 3
MOBA_BLOCK = 256
MOBA_TOPK = 3
N_BUCKETS = 32
MAX_DISTANCE = 128
EPS = 1e-6
N_PROJ = 10

LANES = 128
NEG = -1e30
LOG2E = math.log2(math.e)
VMEM_LIMIT = 56 * 1024 * 1024

F32 = jnp.float32
BF16 = jnp.bfloat16


def _silu(v):
    return v * jax.nn.sigmoid(v)


def _mod_kernel(c_ref, w_ref, b_ref, o_ref):
    a = _silu(c_ref[...])
    o_ref[...] = jnp.dot(a, w_ref[...], precision=lax.Precision.HIGHEST,
                         preferred_element_type=F32) + b_ref[...]


def _modulation(c, w_ada, b_ada):
    bsz, d = c.shape
    n = w_ada.shape[1]
    tn = d
    return pl.pallas_call(
        _mod_kernel,
        grid=(n // tn,),
        in_specs=[pl.BlockSpec((bsz, d), lambda j: (0, 0)),
                  pl.BlockSpec((d, tn), lambda j: (0, j)),
                  pl.BlockSpec((1, tn), lambda j: (0, j))],
        out_specs=pl.BlockSpec((bsz, tn), lambda j: (0, j)),
        out_shape=jax.ShapeDtypeStruct((bsz, n), F32),
        compiler_params=pltpu.CompilerParams(vmem_limit_bytes=VMEM_LIMIT),
    )(c, w_ada, b_ada.reshape(1, n))


W_Q, W_K, W_V, W_GA, W_CB, W_CC, W_CX, W_GC, W_MA, W_MC = range(N_PROJ)
N_FUSED = 7
P_Q, P_K, P_V, P_GA, P_ZC, P_MA, P_MC = range(N_FUSED)
CONV_HALO = 8


def _inproj_kernel(x0_ref, xn_ref, mod0_ref, modn_ref, g_ref, cw_ref, w_ref, o_ref, h_ref, u_ref,
                   t_ref, *, q_scale, row_chunk):
    s = pl.program_id(1)
    step = pl.program_id(0) * pl.num_programs(1) + s
    slot = lax.rem(step, 2)
    ts, d = xn_ref.shape[1], xn_ref.shape[2]
    n_chunks = ts // row_chunk
    g = g_ref[...]

    def normalise(x_ref, mod_ref, dst_slot, r):
        rows = slice(r * row_chunk, (r + 1) * row_chunk)
        xr = x_ref[0, rows, :]
        inv = lax.rsqrt(jnp.mean(xr * xr, axis=-1, keepdims=True) + EPS)
        hr = (xr * inv * g) * (1.0 + mod_ref[0, 1:2, :]) + mod_ref[0, 0:1, :]
        h_ref[dst_slot, rows, :] = hr.astype(BF16)

    @pl.when(step == 0)
    def _():
        for r in range(n_chunks):
            normalise(x0_ref, mod0_ref, 0, r)

    def proj(j):
        return jnp.dot(h_ref[slot], w_ref[:, j * d:(j + 1) * d], preferred_element_type=F32)

    def put(k, val):
        o_ref[0, :, k * d:(k + 1) * d] = val.astype(o_ref.dtype)

    @pl.when(s == 0)
    def _():
        u_ref[0:CONV_HALO, :] = jnp.zeros((CONV_HALO, d), F32)

    def keep_cc(acc):
        t_ref[...] = acc

    def conv(acc):
        u_ref[CONV_HALO:CONV_HALO + ts, :] = t_ref[...] * acc
        t_ref[...] = (cw_ref[0:1, :] * u_ref[CONV_HALO - 2:CONV_HALO - 2 + ts, :]
                      + cw_ref[1:2, :] * u_ref[CONV_HALO - 1:CONV_HALO - 1 + ts, :]
                      + cw_ref[2:3, :] * u_ref[CONV_HALO:CONV_HALO + ts, :])
        u_ref[0:CONV_HALO, :] = u_ref[ts:ts + CONV_HALO, :]

    def gate_cb(acc):
        t_ref[...] = t_ref[...] * acc

    stages = [
        (W_GA, lambda acc: put(P_GA, _silu(acc))),
        (W_MA, lambda acc: put(P_MA, jax.nn.sigmoid(acc))),
        (W_MC, lambda acc: put(P_MC, jax.nn.sigmoid(acc))),
        (W_CC, keep_cc),
        (W_CX, conv),
        (W_CB, gate_cb),
        (W_GC, lambda acc: put(P_ZC, t_ref[...] * _silu(acc))),
        (W_Q, lambda acc: put(P_Q, acc * q_scale)),
        (W_K, lambda acc: put(P_K, acc)),
        (W_V, lambda acc: put(P_V, acc)),
    ]
    norm_after = list(range(len(stages) - 1 - n_chunks, len(stages) - 1))
    acc = proj(stages[0][0])
    for n, (_, epilogue) in enumerate(stages):
        nxt = proj(stages[n + 1][0]) if n + 1 < len(stages) else None
        epilogue(acc)
        if n in norm_after:
            normalise(xn_ref, modn_ref, 1 - slot, norm_after.index(n))
        acc = nxt


def _in_projection(x, mod3, norm_g, conv_w, w_in_bf16, ts=512, row_chunk=128):
    bsz, seq, d = x.shape
    n = w_in_bf16.shape[1]
    n_out = N_FUSED * d
    n_s = seq // ts
    assert ts // row_chunk <= N_PROJ
    kern = functools.partial(_inproj_kernel, q_scale=LOG2E * HEAD_DIM ** -0.5, row_chunk=row_chunk)

    def next_tile(b, s):
        flat = jnp.minimum(b * n_s + s + 1, bsz * n_s - 1)
        return flat // n_s, flat % n_s

    return pl.pallas_call(
        kern,
        grid=(bsz, n_s),
        in_specs=[pl.BlockSpec((1, ts, d), lambda b, s: (0, 0, 0), pipeline_mode=pl.Buffered(1)),
                  pl.BlockSpec((1, ts, d), lambda b, s: (*next_tile(b, s), 0)),
                  pl.BlockSpec((1, 3, d), lambda b, s: (0, 0, 0)),
                  pl.BlockSpec((1, 3, d), lambda b, s: (next_tile(b, s)[0], 0, 0)),
                  pl.BlockSpec((1, d), lambda b, s: (0, 0)),
                  pl.BlockSpec((CONV_K, d), lambda b, s: (0, 0)),
                  pl.BlockSpec((d, n), lambda b, s: (0, 0), pipeline_mode=pl.Buffered(1))],
        out_specs=pl.BlockSpec((1, ts, n_out), lambda b, s: (b, s, 0)),
        out_shape=jax.ShapeDtypeStruct((bsz, seq, n_out), BF16),
        scratch_shapes=[pltpu.VMEM((2, ts, d), BF16), pltpu.VMEM((CONV_HALO + ts, d), F32),
                        pltpu.VMEM((ts, d), F32)],
        compiler_params=pltpu.CompilerParams(
            dimension_semantics=("arbitrary", "arbitrary"),
            vmem_limit_bytes=VMEM_LIMIT),
    )(x, x, mod3, mod3, norm_g.reshape(1, d), conv_w, w_in_bf16)


def _t5_bucket(dist):
    n = jnp.maximum(dist, 0)
    max_exact = N_BUCKETS // 2
    nf = jnp.maximum(n, 1).astype(F32)
    large = max_exact + (jnp.log(nf / max_exact) / math.log(MAX_DISTANCE / max_exact)
                         * (N_BUCKETS - max_exact)).astype(jnp.int32)
    large = jnp.minimum(large, N_BUCKETS - 1)
    return jnp.where(n < max_exact, n, large)


def _table_kernel(rb_ref, o_ref):
    h = pl.program_id(0)
    blk = o_ref.shape[1]
    r = lax.broadcasted_iota(jnp.int32, (blk, blk), 0)
    c = lax.broadcasted_iota(jnp.int32, (blk, blk), 1)
    for t in range(2):
        dist = r - c + t * blk
        bucket = _t5_bucket(dist)
        bias = jnp.zeros((blk, blk), F32)
        for b in range(N_BUCKETS):
            bias = jnp.where(bucket == b, rb_ref[b, h] * LOG2E, bias)
        if t == 0:
            bias = jnp.where(dist >= 0, bias, NEG)
        o_ref[0, :, (1 - t) * blk:(2 - t) * blk] = bias


def _bias_tables(rel_bias):
    return pl.pallas_call(
        _table_kernel,
        grid=(N_HEADS,),
        in_specs=[pl.BlockSpec(memory_space=pltpu.SMEM)],
        out_specs=pl.BlockSpec((1, MOBA_BLOCK, 2 * MOBA_BLOCK), lambda h: (h, 0, 0)),
        out_shape=jax.ShapeDtypeStruct((N_HEADS, MOBA_BLOCK, 2 * MOBA_BLOCK), F32),
        compiler_params=pltpu.CompilerParams(vmem_limit_bytes=VMEM_LIMIT),
    )(rel_bias)


def _attn_kernel(rb_ref, q_ref, k_ref, v_ref, tab_ref, o_ref, qaug_ref, kaug_ref, vaug_ref, kmean_ref,
                 *, nb):
    h = pl.program_id(1)
    blk = MOBA_BLOCK
    hd = HEAD_DIM
    seq = nb * blk
    nt = (((1,), (1,)), ((), ()))

    lane = lax.broadcasted_iota(jnp.int32, (blk, LANES), 1)
    for n in range(nb):
        rows = slice(n * blk, (n + 1) * blk)
        kb = k_ref[0, rows, :]
        kaug_ref[rows, 0:hd] = kb
        kaug_ref[rows, hd:hd + LANES] = jnp.where((lane == n) | (lane == nb + n), 1.0, 0.0).astype(BF16)
        kmean_ref[n:n + 1, :] = jnp.mean(kb.astype(F32), axis=0, keepdims=True)
    vaug_ref[:, 0:hd] = v_ref[0]
    vaug_ref[:, hd:2 * hd] = jnp.ones((seq, hd), BF16)

    km = kmean_ref[...]
    km_hi = km.astype(BF16)
    rem = km - km_hi.astype(F32)
    km_mid = rem.astype(BF16)
    km_lo = (rem - km_mid.astype(F32)).astype(BF16)
    sc3 = lax.dot_general(jnp.concatenate([km_hi, km_mid, km_lo], axis=0), q_ref[0], nt,
                          preferred_element_type=F32)
    sc = (sc3[0:nb] + sc3[nb:2 * nb]) + sc3[2 * nb:3 * nb]

    n_iota = lax.broadcasted_iota(jnp.int32, (nb, seq), 0)
    q_blk = lax.shift_right_logical(lax.broadcasted_iota(jnp.int32, (nb, seq), 1),
                                    int(math.log2(blk)))
    rank = jnp.zeros((nb, seq), jnp.int32)
    for m in range(nb):
        row = sc[m:m + 1, :]
        beats = (row > sc) | ((row == sc) & (m < n_iota))
        rank = rank + jnp.where(beats & (m < q_blk), 1, 0)
    past = n_iota < q_blk
    chosen = past & (rank < MOBA_TOPK)
    far = chosen & (n_iota < q_blk - 1)

    b_far = jnp.full((nb, seq), rb_ref[N_BUCKETS - 1, h] * LOG2E, F32)
    b_hi = b_far.astype(BF16).astype(F32)
    add_hi = jnp.where(past, jnp.where(chosen, jnp.where(far, b_hi, 0.0), NEG), 0.0)
    add_lo = jnp.where(far, b_far - b_hi, 0.0)
    add_t = jnp.concatenate([add_hi, add_lo, jnp.zeros((LANES - 2 * nb, seq), F32)], axis=0)
    qaug_ref[:, 0:hd] = q_ref[0]
    qaug_ref[:, hd:hd + LANES] = add_t.T.astype(BF16)

    def logits(i):
        n_keys = (i + 1) * blk
        s = lax.dot_general(qaug_ref[i * blk:(i + 1) * blk, :], kaug_ref[0:n_keys, :], nt,
                            preferred_element_type=F32)
        if i == 0:
            return s + tab_ref[0, :, blk:2 * blk]
        if i == 1:
            return s + tab_ref[0]
        return jnp.concatenate([s[:, :n_keys - 2 * blk], s[:, n_keys - 2 * blk:] + tab_ref[0]], axis=1)

    s = logits(0)
    for i in range(nb):
        s_next = logits(i + 1) if i + 1 < nb else None
        mx = jnp.max(s, axis=-1, keepdims=True)
        p = jnp.exp2(s - mx)
        pv = jnp.dot(p.astype(BF16), vaug_ref[0:(i + 1) * blk, :], preferred_element_type=F32)
        o_ref[0, i * blk:(i + 1) * blk, :] = (pv[:, 0:hd] / pv[:, hd:2 * hd]).astype(o_ref.dtype)
        s = s_next


def _moba_attention(proj, tables, rel_bias, d_model):
    bsz, seq, _ = proj.shape
    nb = seq // MOBA_BLOCK
    nh = d_model // HEAD_DIM
    assert MAX_DISTANCE <= MOBA_BLOCK + 1 and 2 * nb <= LANES and MOBA_BLOCK & (MOBA_BLOCK - 1) == 0
    kern = functools.partial(_attn_kernel, nb=nb)

    def head_cols(first):
        return pl.BlockSpec((1, seq, HEAD_DIM), lambda b, h: (b, 0, first + h))

    return pl.pallas_call(
        kern,
        grid=(bsz, nh),
        in_specs=[pl.BlockSpec(memory_space=pltpu.SMEM),
                  head_cols(0), head_cols(nh), head_cols(2 * nh),
                  pl.BlockSpec((1, MOBA_BLOCK, 2 * MOBA_BLOCK), lambda b, h: (h, 0, 0))],
        out_specs=head_cols(0),
        out_shape=jax.ShapeDtypeStruct((bsz, seq, d_model), BF16),
        scratch_shapes=[pltpu.VMEM((seq, HEAD_DIM + LANES), BF16),
                        pltpu.VMEM((seq, HEAD_DIM + LANES), BF16),
                        pltpu.VMEM((seq, 2 * HEAD_DIM), BF16),
                        pltpu.VMEM((nb, HEAD_DIM), F32)],
        compiler_params=pltpu.CompilerParams(
            dimension_semantics=("arbitrary", "arbitrary"),
            vmem_limit_bytes=VMEM_LIMIT),
    )(rel_bias, proj, proj, proj, tables)


def _out_kernel(attn_ref, ga_ref, zc_ref, ma_ref, mc_ref, x_ref, mod_ref, woa_ref, woc_ref, wout_ref,
                fg_ref, o_ref):
    y_conv = jnp.dot(zc_ref[0], woc_ref[...], preferred_element_type=F32)
    z_attn = attn_ref[0].astype(F32) * ga_ref[0].astype(F32)
    y_attn = jnp.dot(z_attn.astype(BF16), woa_ref[...], preferred_element_type=F32)
    merged = ma_ref[0].astype(F32) * y_attn + mc_ref[0].astype(F32) * y_conv
    branch = jnp.dot(merged.astype(BF16), wout_ref[...], preferred_element_type=F32)
    res = x_ref[0] + mod_ref[0, 2:3, :] * branch
    inv = lax.rsqrt(jnp.mean(res * res, axis=-1, keepdims=True) + EPS)
    o_ref[0] = res * inv * fg_ref[...]


def _output_stage(attn, proj, x, mod3, woa, woc, wout, final_g, ts=1024):
    bsz, seq, d = x.shape

    def col(k):
        return pl.BlockSpec((1, ts, d), lambda b, s: (b, s, k))

    def whole(shape):
        return pl.BlockSpec(shape, lambda b, s: (0,) * len(shape), pipeline_mode=pl.Buffered(1))

    return pl.pallas_call(
        _out_kernel,
        grid=(bsz, seq // ts),
        in_specs=[col(0), col(P_GA), col(P_ZC), col(P_MA), col(P_MC),
                  col(0), pl.BlockSpec((1, 3, d), lambda b, s: (b, 0, 0)),
                  whole((d, d)), whole((d, d)), whole((d, d)), whole((1, d))],
        out_specs=col(0),
        out_shape=jax.ShapeDtypeStruct((bsz, seq, d), F32),
        compiler_params=pltpu.CompilerParams(
            dimension_semantics=("arbitrary", "arbitrary"),
            vmem_limit_bytes=VMEM_LIMIT),
    )(attn, proj, proj, proj, proj, x, mod3, woa, woc, wout, final_g.reshape(1, d))


def kernel(x, c, norm_g, w_ada, b_ada, w_in, conv_w, w_o_attn, w_o_conv, w_out, rel_bias, final_g):
    bsz, seq, d = x.shape
    depth = norm_g.shape[0]
    assert depth == 1, "the fused output stage applies the final RMSNorm after the single layer"
    assert d == N_HEADS * HEAD_DIM and w_in.shape[2] == N_PROJ * d and seq % MOBA_BLOCK == 0
    tables = _bias_tables(rel_bias)
    mod3 = _modulation(c, w_ada[0], b_ada[0]).reshape(bsz, 3, d)
    proj = _in_projection(x, mod3, norm_g[0], conv_w[0], w_in[0].astype(BF16))
    attn = _moba_attention(proj, tables, rel_bias, d)
    return _output_stage(attn, proj, x, mod3, w_o_attn[0].astype(BF16),
                         w_o_conv[0].astype(BF16), w_out[0].astype(BF16), final_g)
```

```python
import functools
import math

import jax
import jax.numpy as jnp
from jax import lax
from jax.experimental import pallas as pl
from jax.experimental.pallas import tpu as pltpu

N_HEADS = 8
HEAD_DIM = 128
CONV_K = 3
MOBA_BLOCK = 256
MOBA_TOPK = 3
N_BUCKETS = 32
MAX_DISTANCE = 128
EPS = 1e-6
N_PROJ = 10

LANES = 128
SUBLANES = 8
SEL_GROUP = 4
NEG = -1e30
LOG2E = math.log2(math.e)
VMEM_LIMIT = 56 * 1024 * 1024

F32 = jnp.float32
BF16 = jnp.bfloat16


def _silu(v):
    return v * jax.nn.sigmoid(v)


def _split_bf16(v, terms):
    parts = []
    for _ in range(terms - 1):
        hi = v.astype(BF16).astype(F32)
        parts.append(hi)
        v = v - hi
    return parts + [v]


def _mod_kernel(c_ref, w_ref, b_ref, o_ref):
    bsz = c_ref.shape[0]
    a3 = jnp.concatenate(_split_bf16(_silu(c_ref[...]), 3), axis=0).astype(BF16)
    w_hi, w_lo = _split_bf16(w_ref[...], 2)
    r = (jnp.dot(a3, w_hi.astype(BF16), preferred_element_type=F32)
         + jnp.dot(a3, w_lo.astype(BF16), preferred_element_type=F32))
    o_ref[...] = (r[0:bsz] + r[bsz:2 * bsz]) + r[2 * bsz:3 * bsz] + b_ref[...]


def _modulation(c, w_ada, b_ada):
    bsz, d = c.shape
    n = w_ada.shape[1]
    tn = d
    return pl.pallas_call(
        _mod_kernel,
        grid=(n // tn,),
        in_specs=[pl.BlockSpec((bsz, d), lambda j: (0, 0)),
                  pl.BlockSpec((d, tn), lambda j: (0, j)),
                  pl.BlockSpec((1, tn), lambda j: (0, j))],
        out_specs=pl.BlockSpec((bsz, tn), lambda j: (0, j)),
        out_shape=jax.ShapeDtypeStruct((bsz, n), F32),
        compiler_params=pltpu.CompilerParams(vmem_limit_bytes=VMEM_LIMIT),
    )(c, w_ada, b_ada.reshape(1, n))


W_Q, W_K, W_V, W_GA, W_CB, W_CC, W_CX, W_GC, W_MA, W_MC = range(N_PROJ)
N_FUSED = 7
P_Q, P_K, P_V, P_GA, P_ZC, P_MA, P_MC = range(N_FUSED)
CONV_HALO = 8


def _inproj_kernel(x0_ref, xn_ref, mod0_ref, modn_ref, g_ref, cw_ref, w_ref, o_ref, h_ref, u_ref,
                   t_ref, *, q_scale, row_chunk):
    s = pl.program_id(1)
    step = pl.program_id(0) * pl.num_programs(1) + s
    slot = lax.rem(step, 2)
    ts, d = xn_ref.shape[1], xn_ref.shape[2]
    n_chunks = ts // row_chunk
    g = g_ref[...]

    def normalise(x_ref, mod_ref, dst_slot, r):
        rows = slice(r * row_chunk, (r + 1) * row_chunk)
        xr = x_ref[0, rows, :]
        inv = lax.rsqrt(jnp.mean(xr * xr, axis=-1, keepdims=True) + EPS)
        hr = (xr * inv * g) * (1.0 + mod_ref[0, 1:2, :]) + mod_ref[0, 0:1, :]
        h_ref[dst_slot, rows, :] = hr.astype(BF16)

    @pl.when(step == 0)
    def _():
        for r in range(n_chunks):
            normalise(x0_ref, mod0_ref, 0, r)

    def proj(j):
        return jnp.dot(h_ref[slot], w_ref[:, j * d:(j + 1) * d], preferred_element_type=F32)

    def put(k, val):
        o_ref[0, :, k * d:(k + 1) * d] = val.astype(o_ref.dtype)

    @pl.when(s == 0)
    def _():
        u_ref[0:CONV_HALO, :] = jnp.zeros((CONV_HALO, d), F32)

    def keep_cc(acc):
        t_ref[...] = acc

    def conv(acc):
        u_ref[CONV_HALO:CONV_HALO + ts, :] = t_ref[...] * acc
        t_ref[...] = (cw_ref[0:1, :] * u_ref[CONV_HALO - 2:CONV_HALO - 2 + ts, :]
                      + cw_ref[1:2, :] * u_ref[CONV_HALO - 1:CONV_HALO - 1 + ts, :]
                      + cw_ref[2:3, :] * u_ref[CONV_HALO:CONV_HALO + ts, :])
        u_ref[0:CONV_HALO, :] = u_ref[ts:ts + CONV_HALO, :]

    def gate_cb(acc):
        t_ref[...] = t_ref[...] * acc

    stages = [
        (W_GA, lambda acc: put(P_GA, _silu(acc))),
        (W_MA, lambda acc: put(P_MA, jax.nn.sigmoid(acc))),
        (W_MC, lambda acc: put(P_MC, jax.nn.sigmoid(acc))),
        (W_CC, keep_cc),
        (W_CX, conv),
        (W_CB, gate_cb),
        (W_GC, lambda acc: put(P_ZC, t_ref[...] * _silu(acc))),
        (W_Q, lambda acc: put(P_Q, acc * q_scale)),
        (W_K, lambda acc: put(P_K, acc)),
        (W_V, lambda acc: put(P_V, acc)),
    ]
    norm_after = list(range(len(stages) - 1 - n_chunks, len(stages) - 1))
    acc = proj(stages[0][0])
    for n, (_, epilogue) in enumerate(stages):
        nxt = proj(stages[n + 1][0]) if n + 1 < len(stages) else None
        epilogue(acc)
        if n in norm_after:
            normalise(xn_ref, modn_ref, 1 - slot, norm_after.index(n))
        acc = nxt


def _in_projection(x, mod3, norm_g, conv_w, w_in_bf16, ts=512, row_chunk=128):
    bsz, seq, d = x.shape
    n = w_in_bf16.shape[1]
    n_out = N_FUSED * d
    n_s = seq // ts
    assert ts // row_chunk <= N_PROJ
    kern = functools.partial(_inproj_kernel, q_scale=LOG2E * HEAD_DIM ** -0.5, row_chunk=row_chunk)

    def next_tile(b, s):
        flat = jnp.minimum(b * n_s + s + 1, bsz * n_s - 1)
        return flat // n_s, flat % n_s

    return pl.pallas_call(
        kern,
        grid=(bsz, n_s),
        in_specs=[pl.BlockSpec((1, ts, d), lambda b, s: (0, 0, 0), pipeline_mode=pl.Buffered(1)),
                  pl.BlockSpec((1, ts, d), lambda b, s: (*next_tile(b, s), 0)),
                  pl.BlockSpec((1, 3, d), lambda b, s: (0, 0, 0)),
                  pl.BlockSpec((1, 3, d), lambda b, s: (next_tile(b, s)[0], 0, 0)),
                  pl.BlockSpec((1, d), lambda b, s: (0, 0)),
                  pl.BlockSpec((CONV_K, d), lambda b, s: (0, 0)),
                  pl.BlockSpec((d, n), lambda b, s: (0, 0), pipeline_mode=pl.Buffered(1))],
        out_specs=pl.BlockSpec((1, ts, n_out), lambda b, s: (b, s, 0)),
        out_shape=jax.ShapeDtypeStruct((bsz, seq, n_out), BF16),
        scratch_shapes=[pltpu.VMEM((2, ts, d), BF16), pltpu.VMEM((CONV_HALO + ts, d), F32),
                        pltpu.VMEM((ts, d), F32)],
        compiler_params=pltpu.CompilerParams(
            dimension_semantics=("arbitrary", "arbitrary"),
            vmem_limit_bytes=VMEM_LIMIT),
    )(x, x, mod3, mod3, norm_g.reshape(1, d), conv_w, w_in_bf16)


def _t5_bucket(dist):
    n = jnp.maximum(dist, 0)
    max_exact = N_BUCKETS // 2
    nf = jnp.maximum(n, 1).astype(F32)
    large = max_exact + (jnp.log(nf / max_exact) / math.log(MAX_DISTANCE / max_exact)
                         * (N_BUCKETS - max_exact)).astype(jnp.int32)
    large = jnp.minimum(large, N_BUCKETS - 1)
    return jnp.where(n < max_exact, n, large)


def _table_kernel(rb_ref, o_ref):
    h = pl.program_id(0)
    blk = o_ref.shape[1]
    r = lax.broadcasted_iota(jnp.int32, (blk, blk), 0)
    c = lax.broadcasted_iota(jnp.int32, (blk, blk), 1)
    for t in range(2):
        dist = r - c + t * blk
        bucket = _t5_bucket(dist)
        bias = jnp.zeros((blk, blk), F32)
        for b in range(N_BUCKETS):
            bias = jnp.where(bucket == b, rb_ref[b, h] * LOG2E, bias)
        if t == 0:
            bias = jnp.where(dist >= 0, bias, NEG)
        o_ref[0, :, (1 - t) * blk:(2 - t) * blk] = bias


def _bias_tables(rel_bias):
    return pl.pallas_call(
        _table_kernel,
        grid=(N_HEADS,),
        in_specs=[pl.BlockSpec(memory_space=pltpu.SMEM)],
        out_specs=pl.BlockSpec((1, MOBA_BLOCK, 2 * MOBA_BLOCK), lambda h: (h, 0, 0)),
        out_shape=jax.ShapeDtypeStruct((N_HEADS, MOBA_BLOCK, 2 * MOBA_BLOCK), F32),
        compiler_params=pltpu.CompilerParams(vmem_limit_bytes=VMEM_LIMIT),
    )(rel_bias)


def _attn_kernel(rb_ref, q_ref, k_ref, v_ref, tab_ref, o_ref, qaug_ref, kaug_ref, vaug_ref, kmean_ref,
                 *, nb):
    h = pl.program_id(1)
    blk = MOBA_BLOCK
    hd = HEAD_DIM
    seq = nb * blk
    nt = (((1,), (1,)), ((), ()))

    lane = lax.broadcasted_iota(jnp.int32, (blk, LANES), 1)
    for n in range(nb):
        rows = slice(n * blk, (n + 1) * blk)
        kb = k_ref[0, rows, :]
        kaug_ref[rows, 0:hd] = kb
        kaug_ref[rows, hd:hd + LANES] = jnp.where((lane == n) | (lane == nb + n), 1.0, 0.0).astype(BF16)
        kmean_ref[n:n + 1, :] = jnp.mean(kb.astype(F32), axis=0, keepdims=True)
    vaug_ref[:, 0:hd] = v_ref[0]
    vaug_ref[:, hd:2 * hd] = jnp.ones((seq, hd), BF16)

    def select(g):
        q_lo = g * SEL_GROUP * blk
        w = SEL_GROUP * blk
        last_blk = (g + 1) * SEL_GROUP - 1
        nk = min(nb, -(-(last_blk + 1) // SUBLANES) * SUBLANES)

        km3 = jnp.concatenate(_split_bf16(kmean_ref[0:nk, :], 3), axis=0).astype(BF16)
        sc3 = lax.dot_general(km3, q_ref[0, q_lo:q_lo + w, :], nt, preferred_element_type=F32)
        sc = (sc3[0:nk] + sc3[nk:2 * nk]) + sc3[2 * nk:3 * nk]

        n_iota = lax.broadcasted_iota(jnp.int32, (nk, w), 0)
        q_blk = g * SEL_GROUP + lax.shift_right_logical(
            lax.broadcasted_iota(jnp.int32, (nk, w), 1), int(math.log2(blk)))
        rank = jnp.zeros((nk, w), jnp.int32)
        for m in range(last_blk):
            row = sc[m:m + 1, :]
            beats = (row > sc) | ((row == sc) & (m < n_iota))
            rank = rank + jnp.where(beats & (m < q_blk), 1, 0)
        past = n_iota < q_blk
        chosen = past & (rank < MOBA_TOPK)
        far = chosen & (n_iota < q_blk - 1)

        b_far = jnp.full((nk, w), rb_ref[N_BUCKETS - 1, h] * LOG2E, F32)
        b_hi = b_far.astype(BF16).astype(F32)
        add_hi = jnp.where(past, jnp.where(chosen, jnp.where(far, b_hi, 0.0), NEG), 0.0)
        add_lo = jnp.where(far, b_far - b_hi, 0.0)
        pieces = [add_hi, jnp.zeros((nb - nk, w), F32), add_lo, jnp.zeros((LANES - nb - nk, w), F32)]
        add_t = jnp.concatenate([p for p in pieces if p.shape[0]], axis=0)
        qaug_ref[q_lo:q_lo + w, 0:hd] = q_ref[0, q_lo:q_lo + w, :]
        qaug_ref[q_lo:q_lo + w, hd:hd + LANES] = add_t.T.astype(BF16)

    def logits(i):
        n_keys = (i + 1) * blk
        s = lax.dot_general(qaug_ref[i * blk:(i + 1) * blk, :], kaug_ref[0:n_keys, :], nt,
                            preferred_element_type=F32)
        if i == 0:
            return s + tab_ref[0, :, blk:2 * blk]
        if i == 1:
            return s + tab_ref[0]
        return jnp.concatenate([s[:, :n_keys - 2 * blk], s[:, n_keys - 2 * blk:] + tab_ref[0]], axis=1)

    for g in range(nb // SEL_GROUP):
        select(g)
    s = logits(0)
    for i in range(nb):
        s_next = logits(i + 1) if i + 1 < nb else None
        mx = jnp.max(s, axis=-1, keepdims=True)
        p = jnp.exp2(s - mx)
        pv = jnp.dot(p.astype(BF16), vaug_ref[0:(i + 1) * blk, :], preferred_element_type=F32)
        o_ref[0, i * blk:(i + 1) * blk, :] = (pv[:, 0:hd] / pv[:, hd:2 * hd]).astype(o_ref.dtype)
        s = s_next


def _moba_attention(proj, tables, rel_bias, d_model):
    bsz, seq, _ = proj.shape
    nb = seq // MOBA_BLOCK
    nh = d_model // HEAD_DIM
    assert MAX_DISTANCE <= MOBA_BLOCK + 1 and 2 * nb <= LANES and MOBA_BLOCK & (MOBA_BLOCK - 1) == 0
    assert nb % SEL_GROUP == 0 and nb % SUBLANES == 0
    kern = functools.partial(_attn_kernel, nb=nb)

    def head_cols(first):
        return pl.BlockSpec((1, seq, HEAD_DIM), lambda b, h: (b, 0, first + h))

    return pl.pallas_call(
        kern,
        grid=(bsz, nh),
        in_specs=[pl.BlockSpec(memory_space=pltpu.SMEM),
                  head_cols(0), head_cols(nh), head_cols(2 * nh),
                  pl.BlockSpec((1, MOBA_BLOCK, 2 * MOBA_BLOCK), lambda b, h: (h, 0, 0))],
        out_specs=head_cols(0),
        out_shape=jax.ShapeDtypeStruct((bsz, seq, d_model), BF16),
        scratch_shapes=[pltpu.VMEM((seq, HEAD_DIM + LANES), BF16),
                        pltpu.VMEM((seq, HEAD_DIM + LANES), BF16),
                        pltpu.VMEM((seq, 2 * HEAD_DIM), BF16),
                        pltpu.VMEM((nb, HEAD_DIM), F32)],
        compiler_params=pltpu.CompilerParams(
            dimension_semantics=("arbitrary", "arbitrary"),
            vmem_limit_bytes=VMEM_LIMIT),
    )(rel_bias, proj, proj, proj, tables)


def _out_kernel(attn_ref, ga_ref, zc_ref, ma_ref, mc_ref, x_ref, mod_ref, woa_ref, woc_ref, wout_ref,
                fg_ref, o_ref):
    y_conv = jnp.dot(zc_ref[0], woc_ref[...], preferred_element_type=F32)
    z_attn = attn_ref[0].astype(F32) * ga_ref[0].astype(F32)
    y_attn = jnp.dot(z_attn.astype(BF16), woa_ref[...], preferred_element_type=F32)
    merged = ma_ref[0].astype(F32) * y_attn + mc_ref[0].astype(F32) * y_conv
    branch = jnp.dot(merged.astype(BF16), wout_ref[...], preferred_element_type=F32)
    res = x_ref[0] + mod_ref[0, 2:3, :] * branch
    inv = lax.rsqrt(jnp.mean(res * res, axis=-1, keepdims=True) + EPS)
    o_ref[0] = res * inv * fg_ref[...]


def _output_stage(attn, proj, x, mod3, woa, woc, wout, final_g, ts=1024):
    bsz, seq, d = x.shape

    def col(k):
        return pl.BlockSpec((1, ts, d), lambda b, s: (b, s, k))

    def whole(shape):
        return pl.BlockSpec(shape, lambda b, s: (0,) * len(shape), pipeline_mode=pl.Buffered(1))

    return pl.pallas_call(
        _out_kernel,
        grid=(bsz, seq // ts),
        in_specs=[col(0), col(P_GA), col(P_ZC), col(P_MA), col(P_MC),
                  col(0), pl.BlockSpec((1, 3, d), lambda b, s: (b, 0, 0)),
                  whole((d, d)), whole((d, d)), whole((d, d)), whole((1, d))],
        out_specs=col(0),
        out_shape=jax.ShapeDtypeStruct((bsz, seq, d), F32),
        compiler_params=pltpu.CompilerParams(
            dimension_semantics=("arbitrary", "arbitrary"),
            vmem_limit_bytes=VMEM_LIMIT),
    )(attn, proj, proj, proj, proj, x, mod3, woa, woc, wout, final_g.reshape(1, d))


def kernel(x, c, norm_g, w_ada, b_ada, w_in, conv_w, w_o_attn, w_o_conv, w_out, rel_bias, final_g):
    bsz, seq, d = x.shape
    depth = norm_g.shape[0]
    assert depth == 1, "the fused output stage applies the final RMSNorm after the single layer"
    assert d == N_HEADS * HEAD_DIM and w_in.shape[2] == N_PROJ * d and seq % MOBA_BLOCK == 0
    tables = _bias_tables(rel_bias)
    mod3 = _modulation(c, w_ada[0], b_ada[0]).reshape(bsz, 3, d)
    proj = _in_projection(x, mod3, norm_g[0], conv_w[0], w_in[0].astype(BF16))
    attn = _moba_attention(proj, tables, rel_bias, d)
    return _output_stage(attn, proj, x, mod3, w_o_attn[0].astype(BF16),
                         w_o_conv[0].astype(BF16), w_out[0].astype(BF16), final_g)
```

```python
import functools
import math

import jax
import jax.numpy as jnp
from jax import lax
from jax.experimental import pallas as pl
from jax.experimental.pallas import tpu as pltpu

N_HEADS = 8
HEAD_DIM = 128
CONV_K = 3
MOBA_BLOCK = 256
MOBA_TOPK = 3
N_BUCKETS = 32
MAX_DISTANCE = 128
EPS = 1e-6
N_PROJ = 10

LANES = 128
SUBLANES = 8
SEL_GROUP = 4
NEG = -1e30
LOG2E = math.log2(math.e)
VMEM_LIMIT = 56 * 1024 * 1024

F32 = jnp.float32
BF16 = jnp.bfloat16


def _silu(v):
    return v * jax.nn.sigmoid(v)


def _split_bf16(v, terms):
    parts = []
    for _ in range(terms - 1):
        hi = v.astype(BF16).astype(F32)
        parts.append(hi)
        v = v - hi
    return parts + [v]


def _mod_kernel(c_ref, w_ref, b_ref, o_ref):
    bsz = c_ref.shape[0]
    a3 = jnp.concatenate(_split_bf16(_silu(c_ref[...]), 3), axis=0).astype(BF16)
    w_hi, w_lo = _split_bf16(w_ref[...], 2)
    r = (jnp.dot(a3, w_hi.astype(BF16), preferred_element_type=F32)
         + jnp.dot(a3, w_lo.astype(BF16), preferred_element_type=F32))
    o_ref[...] = (r[0:bsz] + r[bsz:2 * bsz]) + r[2 * bsz:3 * bsz] + b_ref[...]


def _modulation(c, w_ada, b_ada):
    bsz, d = c.shape
    n = w_ada.shape[1]
    tn = d
    return pl.pallas_call(
        _mod_kernel,
        grid=(n // tn,),
        in_specs=[pl.BlockSpec((bsz, d), lambda j: (0, 0)),
                  pl.BlockSpec((d, tn), lambda j: (0, j)),
                  pl.BlockSpec((1, tn), lambda j: (0, j))],
        out_specs=pl.BlockSpec((bsz, tn), lambda j: (0, j)),
        out_shape=jax.ShapeDtypeStruct((bsz, n), F32),
        compiler_params=pltpu.CompilerParams(vmem_limit_bytes=VMEM_LIMIT),
    )(c, w_ada, b_ada.reshape(1, n))


W_Q, W_K, W_V, W_GA, W_CB, W_CC, W_CX, W_GC, W_MA, W_MC = range(N_PROJ)
N_FUSED = 7
P_Q, P_K, P_V, P_GA, P_ZC, P_MA, P_MC = range(N_FUSED)
CONV_HALO = 8


def _inproj_kernel(x0_ref, xn_ref, mod0_ref, modn_ref, g_ref, cw_ref, w_ref, o_ref, h_ref, u_ref,
                   t_ref, *, q_scale, row_chunk):
    s = pl.program_id(1)
    step = pl.program_id(0) * pl.num_programs(1) + s
    slot = lax.rem(step, 2)
    ts, d = xn_ref.shape[1], xn_ref.shape[2]
    n_chunks = ts // row_chunk
    g = g_ref[...]

    def normalise(x_ref, mod_ref, dst_slot, r):
        rows = slice(r * row_chunk, (r + 1) * row_chunk)
        xr = x_ref[0, rows, :]
        inv = lax.rsqrt(jnp.mean(xr * xr, axis=-1, keepdims=True) + EPS)
        hr = (xr * inv * g) * (1.0 + mod_ref[0, 1:2, :]) + mod_ref[0, 0:1, :]
        h_ref[dst_slot, rows, :] = hr.astype(BF16)

    @pl.when(step == 0)
    def _():
        for r in range(n_chunks):
            normalise(x0_ref, mod0_ref, 0, r)

    def proj(j):
        return jnp.dot(h_ref[slot], w_ref[:, j * d:(j + 1) * d], preferred_element_type=F32)

    def put(k, val):
        o_ref[0, :, k * d:(k + 1) * d] = val.astype(o_ref.dtype)

    @pl.when(s == 0)
    def _():
        u_ref[0:CONV_HALO, :] = jnp.zeros((CONV_HALO, d), F32)

    def keep_cc(acc):
        t_ref[...] = acc

    def conv(acc):
        u_ref[CONV_HALO:CONV_HALO + ts, :] = t_ref[...] * acc
        t_ref[...] = (cw_ref[0:1, :] * u_ref[CONV_HALO - 2:CONV_HALO - 2 + ts, :]
                      + cw_ref[1:2, :] * u_ref[CONV_HALO - 1:CONV_HALO - 1 + ts, :]
                      + cw_ref[2:3, :] * u_ref[CONV_HALO:CONV_HALO + ts, :])
        u_ref[0:CONV_HALO, :] = u_ref[ts:ts + CONV_HALO, :]

    def gate_cb(acc):
        t_ref[...] = t_ref[...] * acc

    stages = [
        (W_GA, lambda acc: put(P_GA, _silu(acc))),
        (W_MA, lambda acc: put(P_MA, jax.nn.sigmoid(acc))),
        (W_MC, lambda acc: put(P_MC, jax.nn.sigmoid(acc))),
        (W_CC, keep_cc),
        (W_CX, conv),
        (W_CB, gate_cb),
        (W_GC, lambda acc: put(P_ZC, t_ref[...] * _silu(acc))),
        (W_Q, lambda acc: put(P_Q, acc * q_scale)),
        (W_K, lambda acc: put(P_K, acc)),
        (W_V, lambda acc: put(P_V, acc)),
    ]
    norm_after = list(range(len(stages) - 1 - n_chunks, len(stages) - 1))
    acc = proj(stages[0][0])
    for n, (_, epilogue) in enumerate(stages):
        nxt = proj(stages[n + 1][0]) if n + 1 < len(stages) else None
        epilogue(acc)
        if n in norm_after:
            normalise(xn_ref, modn_ref, 1 - slot, norm_after.index(n))
        acc = nxt


def _in_projection(x, mod3, norm_g, conv_w, w_in_bf16, ts=512, row_chunk=128):
    bsz, seq, d = x.shape
    n = w_in_bf16.shape[1]
    n_out = N_FUSED * d
    n_s = seq // ts
    assert ts // row_chunk <= N_PROJ
    kern = functools.partial(_inproj_kernel, q_scale=LOG2E * HEAD_DIM ** -0.5, row_chunk=row_chunk)

    def next_tile(b, s):
        flat = jnp.minimum(b * n_s + s + 1, bsz * n_s - 1)
        return flat // n_s, flat % n_s

    return pl.pallas_call(
        kern,
        grid=(bsz, n_s),
        in_specs=[pl.BlockSpec((1, ts, d), lambda b, s: (0, 0, 0), pipeline_mode=pl.Buffered(1)),
                  pl.BlockSpec((1, ts, d), lambda b, s: (*next_tile(b, s), 0)),
                  pl.BlockSpec((1, 3, d), lambda b, s: (0, 0, 0)),
                  pl.BlockSpec((1, 3, d), lambda b, s: (next_tile(b, s)[0], 0, 0)),
                  pl.BlockSpec((1, d), lambda b, s: (0, 0)),
                  pl.BlockSpec((CONV_K, d), lambda b, s: (0, 0)),
                  pl.BlockSpec((d, n), lambda b, s: (0, 0), pipeline_mode=pl.Buffered(1))],
        out_specs=pl.BlockSpec((1, ts, n_out), lambda b, s: (b, s, 0)),
        out_shape=jax.ShapeDtypeStruct((bsz, seq, n_out), BF16),
        scratch_shapes=[pltpu.VMEM((2, ts, d), BF16), pltpu.VMEM((CONV_HALO + ts, d), F32),
                        pltpu.VMEM((ts, d), F32)],
        compiler_params=pltpu.CompilerParams(
            dimension_semantics=("arbitrary", "arbitrary"),
            vmem_limit_bytes=VMEM_LIMIT),
    )(x, x, mod3, mod3, norm_g.reshape(1, d), conv_w, w_in_bf16)


def _t5_bucket(dist):
    n = jnp.maximum(dist, 0)
    max_exact = N_BUCKETS // 2
    nf = jnp.maximum(n, 1).astype(F32)
    large = max_exact + (jnp.log(nf / max_exact) / math.log(MAX_DISTANCE / max_exact)
                         * (N_BUCKETS - max_exact)).astype(jnp.int32)
    large = jnp.minimum(large, N_BUCKETS - 1)
    return jnp.where(n < max_exact, n, large)


def _table_kernel(rb_ref, o_ref):
    h = pl.program_id(0)
    blk = o_ref.shape[1]
    width = 2 * blk
    u = lax.broadcasted_iota(jnp.int32, (SUBLANES, width), 1)
    for t in range(2):
        dist = jnp.where(u < blk, t * blk - u, t * blk + width - u)
        bucket = _t5_bucket(dist)
        g = jnp.zeros((SUBLANES, width), F32)
        for b in range(N_BUCKETS):
            g = jnp.where(bucket == b, rb_ref[b, h] * LOG2E, g)
        g = jnp.where(dist >= 0, g, NEG)
        skewed = pltpu.roll(jnp.broadcast_to(g[0:1, :], (blk, width)), 0, 1, stride=1, stride_axis=0)
        o_ref[0, :, (1 - t) * blk:(2 - t) * blk] = skewed[:, 0:blk]


def _bias_tables(rel_bias):
    return pl.pallas_call(
        _table_kernel,
        grid=(N_HEADS,),
        in_specs=[pl.BlockSpec(memory_space=pltpu.SMEM)],
        out_specs=pl.BlockSpec((1, MOBA_BLOCK, 2 * MOBA_BLOCK), lambda h: (h, 0, 0)),
        out_shape=jax.ShapeDtypeStruct((N_HEADS, MOBA_BLOCK, 2 * MOBA_BLOCK), F32),
        compiler_params=pltpu.CompilerParams(vmem_limit_bytes=VMEM_LIMIT),
    )(rel_bias)


def _attn_kernel(rb_ref, q_ref, k_ref, v_ref, tab_ref, o_ref, qaug_ref, kaug_ref, vaug_ref, kmean_ref,
                 *, nb):
    h = pl.program_id(1)
    blk = MOBA_BLOCK
    hd = HEAD_DIM
    seq = nb * blk
    nt = (((1,), (1,)), ((), ()))

    lane = lax.broadcasted_iota(jnp.int32, (blk, LANES), 1)
    for n in range(nb):
        rows = slice(n * blk, (n + 1) * blk)
        kb = k_ref[0, rows, :]
        kaug_ref[rows, 0:hd] = kb
        kaug_ref[rows, hd:hd + LANES] = jnp.where((lane == n) | (lane == nb + n), 1.0, 0.0).astype(BF16)
        kmean_ref[n:n + 1, :] = jnp.mean(kb.astype(F32), axis=0, keepdims=True)
    vaug_ref[:, 0:hd] = v_ref[0]
    vaug_ref[:, hd:2 * hd] = jnp.ones((seq, hd), BF16)

    def select(g):
        q_lo = g * SEL_GROUP * blk
        w = SEL_GROUP * blk
        last_blk = (g + 1) * SEL_GROUP - 1
        nk = min(nb, -(-(last_blk + 1) // SUBLANES) * SUBLANES)

        km3 = jnp.concatenate(_split_bf16(kmean_ref[0:nk, :], 3), axis=0).astype(BF16)
        sc3 = lax.dot_general(km3, q_ref[0, q_lo:q_lo + w, :], nt, preferred_element_type=F32)
        sc = (sc3[0:nk] + sc3[nk:2 * nk]) + sc3[2 * nk:3 * nk]

        n_iota = lax.broadcasted_iota(jnp.int32, (nk, w), 0)
        q_blk = g * SEL_GROUP + lax.shift_right_logical(
            lax.broadcasted_iota(jnp.int32, (nk, w), 1), int(math.log2(blk)))
        rank = jnp.zeros((nk, w), jnp.int32)
        for m in range(last_blk):
            row = sc[m:m + 1, :]
            beats = (row > sc) | ((row == sc) & (m < n_iota))
            rank = rank + jnp.where(beats & (m < q_blk), 1, 0)
        past = n_iota < q_blk
        chosen = past & (rank < MOBA_TOPK)
        far = chosen & (n_iota < q_blk - 1)

        b_far = jnp.full((nk, w), rb_ref[N_BUCKETS - 1, h] * LOG2E, F32)
        b_hi = b_far.astype(BF16).astype(F32)
        add_hi = jnp.where(past, jnp.where(chosen, jnp.where(far, b_hi, 0.0), NEG), 0.0)
        add_lo = jnp.where(far, b_far - b_hi, 0.0)
        pieces = [add_hi, jnp.zeros((nb - nk, w), F32), add_lo, jnp.zeros((LANES - nb - nk, w), F32)]
        add_t = jnp.concatenate([p for p in pieces if p.shape[0]], axis=0)
        qaug_ref[q_lo:q_lo + w, 0:hd] = q_ref[0, q_lo:q_lo + w, :]
        qaug_ref[q_lo:q_lo + w, hd:hd + LANES] = add_t.T.astype(BF16)

    def logits(i):
        n_keys = (i + 1) * blk
        s = lax.dot_general(qaug_ref[i * blk:(i + 1) * blk, :], kaug_ref[0:n_keys, :], nt,
                            preferred_element_type=F32)
        if i == 0:
            return s + tab_ref[0, :, blk:2 * blk]
        if i == 1:
            return s + tab_ref[0]
        return jnp.concatenate([s[:, :n_keys - 2 * blk], s[:, n_keys - 2 * blk:] + tab_ref[0]], axis=1)

    for g in range(nb // SEL_GROUP):
        select(g)
    s = logits(0)
    for i in range(nb):
        s_next = logits(i + 1) if i + 1 < nb else None
        mx = jnp.max(s, axis=-1, keepdims=True)
        p = jnp.exp2(s - mx)
        pv = jnp.dot(p.astype(BF16), vaug_ref[0:(i + 1) * blk, :], preferred_element_type=F32)
        o_ref[0, i * blk:(i + 1) * blk, :] = (pv[:, 0:hd] / pv[:, hd:2 * hd]).astype(o_ref.dtype)
        s = s_next


def _moba_attention(proj, tables, rel_bias, d_model):
    bsz, seq, _ = proj.shape
    nb = seq // MOBA_BLOCK
    nh = d_model // HEAD_DIM
    assert MAX_DISTANCE <= MOBA_BLOCK + 1 and 2 * nb <= LANES and MOBA_BLOCK & (MOBA_BLOCK - 1) == 0
    assert nb % SEL_GROUP == 0 and nb % SUBLANES == 0
    kern = functools.partial(_attn_kernel, nb=nb)

    def head_cols(first):
        return pl.BlockSpec((1, seq, HEAD_DIM), lambda b, h: (b, 0, first + h))

    return pl.pallas_call(
        kern,
        grid=(bsz, nh),
        in_specs=[pl.BlockSpec(memory_space=pltpu.SMEM),
                  head_cols(0), head_cols(nh), head_cols(2 * nh),
                  pl.BlockSpec((1, MOBA_BLOCK, 2 * MOBA_BLOCK), lambda b, h: (h, 0, 0))],
        out_specs=head_cols(0),
        out_shape=jax.ShapeDtypeStruct((bsz, seq, d_model), BF16),
        scratch_shapes=[pltpu.VMEM((seq, HEAD_DIM + LANES), BF16),
                        pltpu.VMEM((seq, HEAD_DIM + LANES), BF16),
                        pltpu.VMEM((seq, 2 * HEAD_DIM), BF16),
                        pltpu.VMEM((nb, HEAD_DIM), F32)],
        compiler_params=pltpu.CompilerParams(
            dimension_semantics=("arbitrary", "arbitrary"),
            vmem_limit_bytes=VMEM_LIMIT),
    )(rel_bias, proj, proj, proj, tables)


def _out_kernel(attn_ref, ga_ref, zc_ref, ma_ref, mc_ref, x_ref, mod_ref, woa_ref, woc_ref, wout_ref,
                fg_ref, o_ref):
    y_conv = jnp.dot(zc_ref[0], woc_ref[...], preferred_element_type=F32)
    z_attn = attn_ref[0].astype(F32) * ga_ref[0].astype(F32)
    y_attn = jnp.dot(z_attn.astype(BF16), woa_ref[...], preferred_element_type=F32)
    merged = ma_ref[0].astype(F32) * y_attn + mc_ref[0].astype(F32) * y_conv
    branch = jnp.dot(merged.astype(BF16), wout_ref[...], preferred_element_type=F32)
    res = x_ref[0] + mod_ref[0, 2:3, :] * branch
    inv = lax.rsqrt(jnp.mean(res * res, axis=-1, keepdims=True) + EPS)
    o_ref[0] = res * inv * fg_ref[...]


def _output_stage(attn, proj, x, mod3, woa, woc, wout, final_g, ts=1024):
    bsz, seq, d = x.shape

    def col(k):
        return pl.BlockSpec((1, ts, d), lambda b, s: (b, s, k))

    def whole(shape):
        return pl.BlockSpec(shape, lambda b, s: (0,) * len(shape), pipeline_mode=pl.Buffered(1))

    return pl.pallas_call(
        _out_kernel,
        grid=(bsz, seq // ts),
        in_specs=[col(0), col(P_GA), col(P_ZC), col(P_MA), col(P_MC),
                  col(0), pl.BlockSpec((1, 3, d), lambda b, s: (b, 0, 0)),
                  whole((d, d)), whole((d, d)), whole((d, d)), whole((1, d))],
        out_specs=col(0),
        out_shape=jax.ShapeDtypeStruct((bsz, seq, d), F32),
        compiler_params=pltpu.CompilerParams(
            dimension_semantics=("arbitrary", "arbitrary"),
            vmem_limit_bytes=VMEM_LIMIT),
    )(attn, proj, proj, proj, proj, x, mod3, woa, woc, wout, final_g.reshape(1, d))


def kernel(x, c, norm_g, w_ada, b_ada, w_in, conv_w, w_o_attn, w_o_conv, w_out, rel_bias, final_g):
    bsz, seq, d = x.shape
    depth = norm_g.shape[0]
    assert depth == 1, "the fused output stage applies the final RMSNorm after the single layer"
    assert d == N_HEADS * HEAD_DIM and w_in.shape[2] == N_PROJ * d and seq % MOBA_BLOCK == 0
    tables = _bias_tables(rel_bias)
    mod3 = _modulation(c, w_ada[0], b_ada[0]).reshape(bsz, 3, d)
    proj = _in_projection(x, mod3, norm_g[0], conv_w[0], w_in[0].astype(BF16))
    attn = _moba_attention(proj, tables, rel_bias, d)
    return _output_stage(attn, proj, x, mod3, w_o_attn[0].astype(BF16),
                         w_o_conv[0].astype(BF16), w_out[0].astype(BF16), final_g)
```

```python
import functools
import math

import jax
import jax.numpy as jnp
from jax import lax
from jax.experimental import pallas as pl
from jax.experimental.pallas import tpu as pltpu

N_HEADS = 8
HEAD_DIM = 128
CONV_K = 3
MOBA_BLOCK = 256
MOBA_TOPK = 3
N_BUCKETS = 32
MAX_DISTANCE = 128
EPS = 1e-6
N_PROJ = 10

LANES = 128
SUBLANES = 8
BF16_SUBLANES = 16
SEL_GROUP = 4
NEG = -1e30
LOG2E = math.log2(math.e)
VMEM_LIMIT = 56 * 1024 * 1024

F32 = jnp.float32
BF16 = jnp.bfloat16


def _silu(v):
    return v * jax.nn.sigmoid(v)


def _split_bf16(v, terms):
    parts = []
    for _ in range(terms - 1):
        hi = v.astype(BF16).astype(F32)
        parts.append(hi)
        v = v - hi
    return parts + [v]


def _mod_kernel(c_ref, w_ref, b_ref, o_ref):
    bsz = c_ref.shape[0]
    a3 = jnp.concatenate(_split_bf16(_silu(c_ref[...]), 3), axis=0).astype(BF16)
    w_hi, w_lo = _split_bf16(w_ref[...], 2)
    r = (jnp.dot(a3, w_hi.astype(BF16), preferred_element_type=F32)
         + jnp.dot(a3, w_lo.astype(BF16), preferred_element_type=F32))
    o_ref[...] = (r[0:bsz] + r[bsz:2 * bsz]) + r[2 * bsz:3 * bsz] + b_ref[...]


def _modulation(c, w_ada, b_ada):
    bsz, d = c.shape
    n = w_ada.shape[1]
    tn = d
    return pl.pallas_call(
        _mod_kernel,
        grid=(n // tn,),
        in_specs=[pl.BlockSpec((bsz, d), lambda j: (0, 0)),
                  pl.BlockSpec((d, tn), lambda j: (0, j)),
                  pl.BlockSpec((1, tn), lambda j: (0, j))],
        out_specs=pl.BlockSpec((bsz, tn), lambda j: (0, j)),
        out_shape=jax.ShapeDtypeStruct((bsz, n), F32),
        compiler_params=pltpu.CompilerParams(vmem_limit_bytes=VMEM_LIMIT),
    )(c, w_ada, b_ada.reshape(1, n))


W_Q, W_K, W_V, W_GA, W_CB, W_CC, W_CX, W_GC, W_MA, W_MC = range(N_PROJ)
N_FUSED = 7
P_Q, P_K, P_V, P_GA, P_ZC, P_MA, P_MC = range(N_FUSED)
CONV_HALO = 8


def _inproj_kernel(x0_ref, xn_ref, mod0_ref, modn_ref, g_ref, cw_ref, w_ref, *rest, q_scale,
                   row_chunk, n_late):
    late_in, o_ref, late_out = rest[:n_late], rest[n_late], rest[n_late + 1:2 * n_late + 1]
    h_ref, u_ref, t_ref = rest[2 * n_late + 1:]
    for src, dst in zip(late_in, late_out):
        dst[...] = src[...].astype(dst.dtype)
    s = pl.program_id(1)
    step = pl.program_id(0) * pl.num_programs(1) + s
    slot = lax.rem(step, 2)
    ts, d = xn_ref.shape[1], xn_ref.shape[2]
    n_chunks = ts // row_chunk
    g = g_ref[...]

    def normalise(x_ref, mod_ref, dst_slot, r):
        rows = slice(r * row_chunk, (r + 1) * row_chunk)
        xr = x_ref[0, rows, :]
        inv = lax.rsqrt(jnp.mean(xr * xr, axis=-1, keepdims=True) + EPS)
        hr = (xr * inv * g) * (1.0 + mod_ref[0, 1:2, :]) + mod_ref[0, 0:1, :]
        h_ref[dst_slot, rows, :] = hr.astype(BF16)

    @pl.when(step == 0)
    def _():
        for r in range(n_chunks):
            normalise(x0_ref, mod0_ref, 0, r)

    def proj(j):
        return jnp.dot(h_ref[slot], w_ref[:, j * d:(j + 1) * d], preferred_element_type=F32)

    def put(k, val):
        o_ref[0, :, k * d:(k + 1) * d] = val.astype(o_ref.dtype)

    @pl.when(s == 0)
    def _():
        u_ref[0:CONV_HALO, :] = jnp.zeros((CONV_HALO, d), F32)

    def keep_cc(acc):
        t_ref[...] = acc

    def conv(acc):
        u_ref[CONV_HALO:CONV_HALO + ts, :] = t_ref[...] * acc
        t_ref[...] = (cw_ref[0:1, :] * u_ref[CONV_HALO - 2:CONV_HALO - 2 + ts, :]
                      + cw_ref[1:2, :] * u_ref[CONV_HALO - 1:CONV_HALO - 1 + ts, :]
                      + cw_ref[2:3, :] * u_ref[CONV_HALO:CONV_HALO + ts, :])
        u_ref[0:CONV_HALO, :] = u_ref[ts:ts + CONV_HALO, :]

    def gate_cb(acc):
        t_ref[...] = t_ref[...] * acc

    stages = [
        (W_GA, lambda acc: put(P_GA, _silu(acc))),
        (W_MA, lambda acc: put(P_MA, jax.nn.sigmoid(acc))),
        (W_MC, lambda acc: put(P_MC, jax.nn.sigmoid(acc))),
        (W_CC, keep_cc),
        (W_CX, conv),
        (W_CB, gate_cb),
        (W_GC, lambda acc: put(P_ZC, t_ref[...] * _silu(acc))),
        (W_Q, lambda acc: put(P_Q, acc * q_scale)),
        (W_K, lambda acc: put(P_K, acc)),
        (W_V, lambda acc: put(P_V, acc)),
    ]
    norm_after = list(range(len(stages) - 1 - n_chunks, len(stages) - 1))
    acc = proj(stages[0][0])
    for n, (_, epilogue) in enumerate(stages):
        nxt = proj(stages[n + 1][0]) if n + 1 < len(stages) else None
        epilogue(acc)
        if n in norm_after:
            normalise(xn_ref, modn_ref, 1 - slot, norm_after.index(n))
        acc = nxt


def _in_projection(x, mod3, norm_g, conv_w, w_in_bf16, late_weights, ts=512, row_chunk=128):
    bsz, seq, d = x.shape
    n = w_in_bf16.shape[1]
    n_out = N_FUSED * d
    n_s = seq // ts
    n_steps = bsz * n_s
    assert ts // row_chunk <= N_PROJ
    slab = d // n_steps
    assert slab * n_steps == d and slab % BF16_SUBLANES == 0
    assert all(w.shape == (d, d) for w in late_weights)
    kern = functools.partial(_inproj_kernel, q_scale=LOG2E * HEAD_DIM ** -0.5, row_chunk=row_chunk,
                             n_late=len(late_weights))

    def next_tile(b, s):
        flat = jnp.minimum(b * n_s + s + 1, n_steps - 1)
        return flat // n_s, flat % n_s

    slab_spec = pl.BlockSpec((slab, d), lambda b, s: (b * n_s + s, 0))
    return pl.pallas_call(
        kern,
        grid=(bsz, n_s),
        in_specs=[pl.BlockSpec((1, ts, d), lambda b, s: (0, 0, 0), pipeline_mode=pl.Buffered(1)),
                  pl.BlockSpec((1, ts, d), lambda b, s: (*next_tile(b, s), 0)),
                  pl.BlockSpec((1, 3, d), lambda b, s: (0, 0, 0)),
                  pl.BlockSpec((1, 3, d), lambda b, s: (next_tile(b, s)[0], 0, 0)),
                  pl.BlockSpec((1, d), lambda b, s: (0, 0)),
                  pl.BlockSpec((CONV_K, d), lambda b, s: (0, 0)),
                  pl.BlockSpec((d, n), lambda b, s: (0, 0), pipeline_mode=pl.Buffered(1))]
                 + [slab_spec] * len(late_weights),
        out_specs=[pl.BlockSpec((1, ts, n_out), lambda b, s: (b, s, 0))]
                  + [slab_spec] * len(late_weights),
        out_shape=[jax.ShapeDtypeStruct((bsz, seq, n_out), BF16)]
                  + [jax.ShapeDtypeStruct((d, d), BF16)] * len(late_weights),
        scratch_shapes=[pltpu.VMEM((2, ts, d), BF16), pltpu.VMEM((CONV_HALO + ts, d), F32),
                        pltpu.VMEM((ts, d), F32)],
        compiler_params=pltpu.CompilerParams(
            dimension_semantics=("arbitrary", "arbitrary"),
            vmem_limit_bytes=VMEM_LIMIT),
    )(x, x, mod3, mod3, norm_g.reshape(1, d), conv_w, w_in_bf16, *late_weights)


def _t5_bucket(dist):
    n = jnp.maximum(dist, 0)
    max_exact = N_BUCKETS // 2
    nf = jnp.maximum(n, 1).astype(F32)
    large = max_exact + (jnp.log(nf / max_exact) / math.log(MAX_DISTANCE / max_exact)
                         * (N_BUCKETS - max_exact)).astype(jnp.int32)
    large = jnp.minimum(large, N_BUCKETS - 1)
    return jnp.where(n < max_exact, n, large)


def _table_kernel(rb_ref, o_ref):
    n_heads, blk, width = o_ref.shape
    u = lax.broadcasted_iota(jnp.int32, (SUBLANES, width), 1)
    for t in range(2):
        dist = jnp.where(u < blk, t * blk - u, t * blk + width - u)
        bucket = _t5_bucket(dist)
        for h in range(n_heads):
            g = jnp.zeros((SUBLANES, width), F32)
            for b in range(N_BUCKETS):
                g = jnp.where(bucket == b, rb_ref[b, h] * LOG2E, g)
            g = jnp.where(dist >= 0, g, NEG)
            skewed = pltpu.roll(jnp.broadcast_to(g[0:1, :], (blk, width)), 0, 1,
                                stride=1, stride_axis=0)
            o_ref[h, :, (1 - t) * blk:(2 - t) * blk] = skewed[:, 0:blk]


def _bias_tables(rel_bias):
    shape = (N_HEADS, MOBA_BLOCK, 2 * MOBA_BLOCK)
    return pl.pallas_call(
        _table_kernel,
        grid=(1,),
        in_specs=[pl.BlockSpec(memory_space=pltpu.SMEM)],
        out_specs=pl.BlockSpec(shape, lambda i: (0, 0, 0)),
        out_shape=jax.ShapeDtypeStruct(shape, F32),
        compiler_params=pltpu.CompilerParams(vmem_limit_bytes=VMEM_LIMIT),
    )(rel_bias)


def _attn_kernel(rb_ref, q_ref, k_ref, v_ref, tab_ref, o_ref, qaug_ref, kaug_ref, vaug_ref, kmean_ref,
                 *, nb):
    h = pl.program_id(1)
    blk = MOBA_BLOCK
    hd = HEAD_DIM
    seq = nb * blk
    nt = (((1,), (1,)), ((), ()))

    lane = lax.broadcasted_iota(jnp.int32, (blk, LANES), 1)
    for n in range(nb):
        rows = slice(n * blk, (n + 1) * blk)
        kb = k_ref[0, rows, :]
        kaug_ref[rows, 0:hd] = kb
        kaug_ref[rows, hd:hd + LANES] = jnp.where((lane == n) | (lane == nb + n), 1.0, 0.0).astype(BF16)
        kmean_ref[n:n + 1, :] = jnp.mean(kb.astype(F32), axis=0, keepdims=True)
    vaug_ref[:, 0:hd] = v_ref[0]
    vaug_ref[:, hd:2 * hd] = jnp.ones((seq, hd), BF16)

    def select(g):
        q_lo = g * SEL_GROUP * blk
        w = SEL_GROUP * blk
        last_blk = (g + 1) * SEL_GROUP - 1
        nk = min(nb, -(-(last_blk + 1) // SUBLANES) * SUBLANES)

        km3 = jnp.concatenate(_split_bf16(kmean_ref[0:nk, :], 3), axis=0).astype(BF16)
        sc3 = lax.dot_general(km3, q_ref[0, q_lo:q_lo + w, :], nt, preferred_element_type=F32)
        sc = (sc3[0:nk] + sc3[nk:2 * nk]) + sc3[2 * nk:3 * nk]

        n_iota = lax.broadcasted_iota(jnp.int32, (nk, w), 0)
        q_blk = g * SEL_GROUP + lax.shift_right_logical(
            lax.broadcasted_iota(jnp.int32, (nk, w), 1), int(math.log2(blk)))
        rank = jnp.zeros((nk, w), jnp.int32)
        for m in range(last_blk):
            row = sc[m:m + 1, :]
            beats = (row > sc) | ((row == sc) & (m < n_iota))
            rank = rank + jnp.where(beats & (m < q_blk), 1, 0)
        past = n_iota < q_blk
        chosen = past & (rank < MOBA_TOPK)
        far = chosen & (n_iota < q_blk - 1)

        b_far = jnp.full((nk, w), rb_ref[N_BUCKETS - 1, h] * LOG2E, F32)
        b_hi = b_far.astype(BF16).astype(F32)
        add_hi = jnp.where(past, jnp.where(chosen, jnp.where(far, b_hi, 0.0), NEG), 0.0)
        add_lo = jnp.where(far, b_far - b_hi, 0.0)
        pieces = [add_hi, jnp.zeros((nb - nk, w), F32), add_lo, jnp.zeros((LANES - nb - nk, w), F32)]
        add_t = jnp.concatenate([p for p in pieces if p.shape[0]], axis=0)
        qaug_ref[q_lo:q_lo + w, 0:hd] = q_ref[0, q_lo:q_lo + w, :]
        qaug_ref[q_lo:q_lo + w, hd:hd + LANES] = add_t.T.astype(BF16)

    def logits(i):
        n_keys = (i + 1) * blk
        s = lax.dot_general(qaug_ref[i * blk:(i + 1) * blk, :], kaug_ref[0:n_keys, :], nt,
                            preferred_element_type=F32)
        if i == 0:
            return s + tab_ref[0, :, blk:2 * blk]
        if i == 1:
            return s + tab_ref[0]
        return jnp.concatenate([s[:, :n_keys - 2 * blk], s[:, n_keys - 2 * blk:] + tab_ref[0]], axis=1)

    for g in range(nb // SEL_GROUP):
        select(g)
    s = logits(0)
    for i in range(nb):
        s_next = logits(i + 1) if i + 1 < nb else None
        mx = jnp.max(s, axis=-1, keepdims=True)
        p = jnp.exp2(s - mx)
        pv = jnp.dot(p.astype(BF16), vaug_ref[0:(i + 1) * blk, :], preferred_element_type=F32)
        o_ref[0, i * blk:(i + 1) * blk, :] = (pv[:, 0:hd] / pv[:, hd:2 * hd]).astype(o_ref.dtype)
        s = s_next


def _moba_attention(proj, tables, rel_bias, d_model):
    bsz, seq, _ = proj.shape
    nb = seq // MOBA_BLOCK
    nh = d_model // HEAD_DIM
    assert MAX_DISTANCE <= MOBA_BLOCK + 1 and 2 * nb <= LANES and MOBA_BLOCK & (MOBA_BLOCK - 1) == 0
    assert nb % SEL_GROUP == 0 and nb % SUBLANES == 0
    kern = functools.partial(_attn_kernel, nb=nb)

    def head_cols(first):
        return pl.BlockSpec((1, seq, HEAD_DIM), lambda b, h: (b, 0, first + h))

    return pl.pallas_call(
        kern,
        grid=(bsz, nh),
        in_specs=[pl.BlockSpec(memory_space=pltpu.SMEM),
                  head_cols(0), head_cols(nh), head_cols(2 * nh),
                  pl.BlockSpec((1, MOBA_BLOCK, 2 * MOBA_BLOCK), lambda b, h: (h, 0, 0))],
        out_specs=head_cols(0),
        out_shape=jax.ShapeDtypeStruct((bsz, seq, d_model), BF16),
        scratch_shapes=[pltpu.VMEM((seq, HEAD_DIM + LANES), BF16),
                        pltpu.VMEM((seq, HEAD_DIM + LANES), BF16),
                        pltpu.VMEM((seq, 2 * HEAD_DIM), BF16),
                        pltpu.VMEM((nb, HEAD_DIM), F32)],
        compiler_params=pltpu.CompilerParams(
            dimension_semantics=("arbitrary", "arbitrary"),
            vmem_limit_bytes=VMEM_LIMIT),
    )(rel_bias, proj, proj, proj, tables)


def _out_kernel(attn_ref, ga_ref, zc_ref, ma_ref, mc_ref, x_ref, mod_ref, woa_ref, woc_ref, wout_ref,
                fg_ref, o_ref):
    y_conv = jnp.dot(zc_ref[0], woc_ref[...], preferred_element_type=F32)
    z_attn = attn_ref[0].astype(F32) * ga_ref[0].astype(F32)
    y_attn = jnp.dot(z_attn.astype(BF16), woa_ref[...], preferred_element_type=F32)
    merged = ma_ref[0].astype(F32) * y_attn + mc_ref[0].astype(F32) * y_conv
    branch = jnp.dot(merged.astype(BF16), wout_ref[...], preferred_element_type=F32)
    res = x_ref[0] + mod_ref[0, 2:3, :] * branch
    inv = lax.rsqrt(jnp.mean(res * res, axis=-1, keepdims=True) + EPS)
    o_ref[0] = res * inv * fg_ref[...]


def _output_stage(attn, proj, x, mod3, woa, woc, wout, final_g, ts=1024):
    bsz, seq, d = x.shape

    def col(k):
        return pl.BlockSpec((1, ts, d), lambda b, s: (b, s, k))

    def whole(shape):
        return pl.BlockSpec(shape, lambda b, s: (0,) * len(shape), pipeline_mode=pl.Buffered(1))

    return pl.pallas_call(
        _out_kernel,
        grid=(bsz, seq // ts),
        in_specs=[col(0), col(P_GA), col(P_ZC), col(P_MA), col(P_MC),
                  col(0), pl.BlockSpec((1, 3, d), lambda b, s: (b, 0, 0)),
                  whole((d, d)), whole((d, d)), whole((d, d)), whole((1, d))],
        out_specs=col(0),
        out_shape=jax.ShapeDtypeStruct((bsz, seq, d), F32),
        compiler_params=pltpu.CompilerParams(
            dimension_semantics=("arbitrary", "arbitrary"),
            vmem_limit_bytes=VMEM_LIMIT),
    )(attn, proj, proj, proj, proj, x, mod3, woa, woc, wout, final_g.reshape(1, d))


def kernel(x, c, norm_g, w_ada, b_ada, w_in, conv_w, w_o_attn, w_o_conv, w_out, rel_bias, final_g):
    bsz, seq, d = x.shape
    depth = norm_g.shape[0]
    assert depth == 1, "the fused output stage applies the final RMSNorm after the single layer"
    assert d == N_HEADS * HEAD_DIM and w_in.shape[2] == N_PROJ * d and seq % MOBA_BLOCK == 0
    tables = _bias_tables(rel_bias)
    mod3 = _modulation(c, w_ada[0], b_ada[0]).reshape(bsz, 3, d)
    proj, woa, woc, wout = _in_projection(x, mod3, norm_g[0], conv_w[0], w_in[0].astype(BF16),
                                          (w_o_attn[0], w_o_conv[0], w_out[0]))
    attn = _moba_attention(proj, tables, rel_bias, d)
    return _output_stage(attn, proj, x, mod3, woa, woc, wout, final_g)
```

```python
import functools
import math

import jax
import jax.numpy as jnp
from jax import lax
from jax.experimental import pallas as pl
from jax.experimental.pallas import tpu as pltpu

N_HEADS = 8
HEAD_DIM = 128
CONV_K = 3
MOBA_BLOCK = 256
MOBA_TOPK = 3
N_BUCKETS = 32
MAX_DISTANCE = 128
EPS = 1e-6
N_PROJ = 10

LANES = 128
SUBLANES = 8
BF16_SUBLANES = 16
SEL_GROUP = 4
NEG = -1e30
LOG2E = math.log2(math.e)
VMEM_LIMIT = 56 * 1024 * 1024

F32 = jnp.float32
BF16 = jnp.bfloat16


def _sigmoid(v):
    return 0.5 * jnp.tanh(0.5 * v) + 0.5


def _silu(v):
    return v * _sigmoid(v)


def _split_bf16(v, terms):
    parts = []
    for _ in range(terms - 1):
        hi = v.astype(BF16).astype(F32)
        parts.append(hi)
        v = v - hi
    return parts + [v]


def _mod_kernel(c_ref, w_ref, b_ref, o_ref):
    bsz = c_ref.shape[0]
    a3 = jnp.concatenate(_split_bf16(_silu(c_ref[...]), 3), axis=0).astype(BF16)
    w_hi, w_lo = _split_bf16(w_ref[...], 2)
    r = (jnp.dot(a3, w_hi.astype(BF16), preferred_element_type=F32)
         + jnp.dot(a3, w_lo.astype(BF16), preferred_element_type=F32))
    o_ref[...] = (r[0:bsz] + r[bsz:2 * bsz]) + r[2 * bsz:3 * bsz] + b_ref[...]


def _modulation(c, w_ada, b_ada):
    bsz, d = c.shape
    n = w_ada.shape[1]
    tn = d
    return pl.pallas_call(
        _mod_kernel,
        grid=(n // tn,),
        in_specs=[pl.BlockSpec((bsz, d), lambda j: (0, 0)),
                  pl.BlockSpec((d, tn), lambda j: (0, j)),
                  pl.BlockSpec((1, tn), lambda j: (0, j))],
        out_specs=pl.BlockSpec((bsz, tn), lambda j: (0, j)),
        out_shape=jax.ShapeDtypeStruct((bsz, n), F32),
        compiler_params=pltpu.CompilerParams(vmem_limit_bytes=VMEM_LIMIT),
    )(c, w_ada, b_ada.reshape(1, n))


W_Q, W_K, W_V, W_GA, W_CB, W_CC, W_CX, W_GC, W_MA, W_MC = range(N_PROJ)
N_FUSED = 7
P_Q, P_K, P_V, P_GA, P_ZC, P_MA, P_MC = range(N_FUSED)
CONV_HALO = 8


def _inproj_kernel(x0_ref, xn_ref, mod0_ref, modn_ref, g_ref, cw_ref, w_ref, *rest, q_scale,
                   row_chunk, n_late):
    late_in, o_ref, late_out = rest[:n_late], rest[n_late], rest[n_late + 1:2 * n_late + 1]
    h_ref, u_ref, t_ref = rest[2 * n_late + 1:]
    for src, dst in zip(late_in, late_out):
        dst[...] = src[...].astype(dst.dtype)
    s = pl.program_id(1)
    step = pl.program_id(0) * pl.num_programs(1) + s
    slot = lax.rem(step, 2)
    ts, d = xn_ref.shape[1], xn_ref.shape[2]
    n_chunks = ts // row_chunk
    g = g_ref[...]

    def normalise(x_ref, mod_ref, dst_slot, r):
        rows = slice(r * row_chunk, (r + 1) * row_chunk)
        xr = x_ref[0, rows, :]
        inv = lax.rsqrt(jnp.mean(xr * xr, axis=-1, keepdims=True) + EPS)
        hr = (xr * inv * g) * (1.0 + mod_ref[0, 1:2, :]) + mod_ref[0, 0:1, :]
        h_ref[dst_slot, rows, :] = hr.astype(BF16)

    @pl.when(step == 0)
    def _():
        for r in range(n_chunks):
            normalise(x0_ref, mod0_ref, 0, r)

    def proj(j):
        return jnp.dot(h_ref[slot], w_ref[:, j * d:(j + 1) * d], preferred_element_type=F32)

    def put(k, val):
        o_ref[0, :, k * d:(k + 1) * d] = val.astype(o_ref.dtype)

    @pl.when(s == 0)
    def _():
        u_ref[0:CONV_HALO, :] = jnp.zeros((CONV_HALO, d), F32)

    def keep_cc(acc):
        t_ref[...] = acc

    def conv(acc):
        u_ref[CONV_HALO:CONV_HALO + ts, :] = t_ref[...] * acc
        t_ref[...] = (cw_ref[0:1, :] * u_ref[CONV_HALO - 2:CONV_HALO - 2 + ts, :]
                      + cw_ref[1:2, :] * u_ref[CONV_HALO - 1:CONV_HALO - 1 + ts, :]
                      + cw_ref[2:3, :] * u_ref[CONV_HALO:CONV_HALO + ts, :])
        u_ref[0:CONV_HALO, :] = u_ref[ts:ts + CONV_HALO, :]

    def gate_cb(acc):
        t_ref[...] = t_ref[...] * acc

    stages = [
        (W_GA, lambda acc: put(P_GA, _silu(acc))),
        (W_MA, lambda acc: put(P_MA, _sigmoid(acc))),
        (W_MC, lambda acc: put(P_MC, _sigmoid(acc))),
        (W_CC, keep_cc),
        (W_CX, conv),
        (W_CB, gate_cb),
        (W_GC, lambda acc: put(P_ZC, t_ref[...] * _silu(acc))),
        (W_Q, lambda acc: put(P_Q, acc * q_scale)),
        (W_K, lambda acc: put(P_K, acc)),
        (W_V, lambda acc: put(P_V, acc)),
    ]
    norm_after = list(range(len(stages) - 1 - n_chunks, len(stages) - 1))
    acc = proj(stages[0][0])
    for n, (_, epilogue) in enumerate(stages):
        nxt = proj(stages[n + 1][0]) if n + 1 < len(stages) else None
        epilogue(acc)
        if n in norm_after:
            normalise(xn_ref, modn_ref, 1 - slot, norm_after.index(n))
        acc = nxt


def _in_projection(x, mod3, norm_g, conv_w, w_in_bf16, late_weights, ts=512, row_chunk=128):
    bsz, seq, d = x.shape
    n = w_in_bf16.shape[1]
    n_out = N_FUSED * d
    n_s = seq // ts
    n_steps = bsz * n_s
    assert ts // row_chunk <= N_PROJ
    slab = d // n_steps
    assert slab * n_steps == d and slab % BF16_SUBLANES == 0
    assert all(w.shape == (d, d) for w in late_weights)
    kern = functools.partial(_inproj_kernel, q_scale=LOG2E * HEAD_DIM ** -0.5, row_chunk=row_chunk,
                             n_late=len(late_weights))

    def next_tile(b, s):
        flat = jnp.minimum(b * n_s + s + 1, n_steps - 1)
        return flat // n_s, flat % n_s

    slab_spec = pl.BlockSpec((slab, d), lambda b, s: (b * n_s + s, 0))
    return pl.pallas_call(
        kern,
        grid=(bsz, n_s),
        in_specs=[pl.BlockSpec((1, ts, d), lambda b, s: (0, 0, 0), pipeline_mode=pl.Buffered(1)),
                  pl.BlockSpec((1, ts, d), lambda b, s: (*next_tile(b, s), 0)),
                  pl.BlockSpec((1, 3, d), lambda b, s: (0, 0, 0)),
                  pl.BlockSpec((1, 3, d), lambda b, s: (next_tile(b, s)[0], 0, 0)),
                  pl.BlockSpec((1, d), lambda b, s: (0, 0)),
                  pl.BlockSpec((CONV_K, d), lambda b, s: (0, 0)),
                  pl.BlockSpec((d, n), lambda b, s: (0, 0), pipeline_mode=pl.Buffered(1))]
                 + [slab_spec] * len(late_weights),
        out_specs=[pl.BlockSpec((1, ts, n_out), lambda b, s: (b, s, 0))]
                  + [slab_spec] * len(late_weights),
        out_shape=[jax.ShapeDtypeStruct((bsz, seq, n_out), BF16)]
                  + [jax.ShapeDtypeStruct((d, d), BF16)] * len(late_weights),
        scratch_shapes=[pltpu.VMEM((2, ts, d), BF16), pltpu.VMEM((CONV_HALO + ts, d), F32),
                        pltpu.VMEM((ts, d), F32)],
        compiler_params=pltpu.CompilerParams(
            dimension_semantics=("arbitrary", "arbitrary"),
            vmem_limit_bytes=VMEM_LIMIT),
    )(x, x, mod3, mod3, norm_g.reshape(1, d), conv_w, w_in_bf16, *late_weights)


def _t5_bucket(dist):
    n = jnp.maximum(dist, 0)
    max_exact = N_BUCKETS // 2
    nf = jnp.maximum(n, 1).astype(F32)
    large = max_exact + (jnp.log(nf / max_exact) / math.log(MAX_DISTANCE / max_exact)
                         * (N_BUCKETS - max_exact)).astype(jnp.int32)
    large = jnp.minimum(large, N_BUCKETS - 1)
    return jnp.where(n < max_exact, n, large)


def _table_kernel(rb_ref, o_ref):
    n_heads, blk, width = o_ref.shape
    u = lax.broadcasted_iota(jnp.int32, (SUBLANES, width), 1)
    for t in range(2):
        dist = jnp.where(u < blk, t * blk - u, t * blk + width - u)
        bucket = _t5_bucket(dist)
        for h in range(n_heads):
            g = jnp.zeros((SUBLANES, width), F32)
            for b in range(N_BUCKETS):
                g = jnp.where(bucket == b, rb_ref[b, h] * LOG2E, g)
            g = jnp.where(dist >= 0, g, NEG)
            skewed = pltpu.roll(jnp.broadcast_to(g[0:1, :], (blk, width)), 0, 1,
                                stride=1, stride_axis=0)
            o_ref[h, :, (1 - t) * blk:(2 - t) * blk] = skewed[:, 0:blk]


def _bias_tables(rel_bias):
    shape = (N_HEADS, MOBA_BLOCK, 2 * MOBA_BLOCK)
    return pl.pallas_call(
        _table_kernel,
        grid=(1,),
        in_specs=[pl.BlockSpec(memory_space=pltpu.SMEM)],
        out_specs=pl.BlockSpec(shape, lambda i: (0, 0, 0)),
        out_shape=jax.ShapeDtypeStruct(shape, F32),
        compiler_params=pltpu.CompilerParams(vmem_limit_bytes=VMEM_LIMIT),
    )(rel_bias)


def _attn_kernel(rb_ref, q_ref, k_ref, v_ref, tab_ref, o_ref, qaug_ref, kaug_ref, vaug_ref, kmean_ref,
                 *, nb):
    h = pl.program_id(1)
    blk = MOBA_BLOCK
    hd = HEAD_DIM
    seq = nb * blk
    nt = (((1,), (1,)), ((), ()))

    lane = lax.broadcasted_iota(jnp.int32, (blk, LANES), 1)
    for n in range(nb):
        rows = slice(n * blk, (n + 1) * blk)
        kb = k_ref[0, rows, :]
        kaug_ref[rows, 0:hd] = kb
        kaug_ref[rows, hd:hd + LANES] = jnp.where((lane == n) | (lane == nb + n), 1.0, 0.0).astype(BF16)
        kmean_ref[n:n + 1, :] = jnp.mean(kb.astype(F32), axis=0, keepdims=True)
    vaug_ref[:, 0:hd] = v_ref[0]
    vaug_ref[:, hd:2 * hd] = jnp.ones((seq, hd), BF16)

    def select(g):
        q_lo = g * SEL_GROUP * blk
        w = SEL_GROUP * blk
        last_blk = (g + 1) * SEL_GROUP - 1
        nk = min(nb, -(-(last_blk + 1) // SUBLANES) * SUBLANES)

        km3 = jnp.concatenate(_split_bf16(kmean_ref[0:nk, :], 3), axis=0).astype(BF16)
        sc3 = lax.dot_general(km3, q_ref[0, q_lo:q_lo + w, :], nt, preferred_element_type=F32)
        sc = (sc3[0:nk] + sc3[nk:2 * nk]) + sc3[2 * nk:3 * nk]

        n_iota = lax.broadcasted_iota(jnp.int32, (nk, w), 0)
        q_blk = g * SEL_GROUP + lax.shift_right_logical(
            lax.broadcasted_iota(jnp.int32, (nk, w), 1), int(math.log2(blk)))
        rank = jnp.zeros((nk, w), jnp.int32)
        for m in range(last_blk):
            row = sc[m:m + 1, :]
            beats = (row > sc) | ((row == sc) & (m < n_iota))
            rank = rank + jnp.where(beats & (m < q_blk), 1, 0)
        past = n_iota < q_blk
        chosen = past & (rank < MOBA_TOPK)
        far = chosen & (n_iota < q_blk - 1)

        b_far = jnp.full((nk, w), rb_ref[N_BUCKETS - 1, h] * LOG2E, F32)
        b_hi = b_far.astype(BF16).astype(F32)
        add_hi = jnp.where(past, jnp.where(chosen, jnp.where(far, b_hi, 0.0), NEG), 0.0)
        add_lo = jnp.where(far, b_far - b_hi, 0.0)
        pieces = [add_hi, jnp.zeros((nb - nk, w), F32), add_lo, jnp.zeros((LANES - nb - nk, w), F32)]
        add_t = jnp.concatenate([p for p in pieces if p.shape[0]], axis=0)
        qaug_ref[q_lo:q_lo + w, 0:hd] = q_ref[0, q_lo:q_lo + w, :]
        qaug_ref[q_lo:q_lo + w, hd:hd + LANES] = add_t.T.astype(BF16)

    def logits(i):
        n_keys = (i + 1) * blk
        s = lax.dot_general(qaug_ref[i * blk:(i + 1) * blk, :], kaug_ref[0:n_keys, :], nt,
                            preferred_element_type=F32)
        if i == 0:
            return s + tab_ref[0, :, blk:2 * blk]
        if i == 1:
            return s + tab_ref[0]
        return jnp.concatenate([s[:, :n_keys - 2 * blk], s[:, n_keys - 2 * blk:] + tab_ref[0]], axis=1)

    for g in range(nb // SEL_GROUP):
        select(g)
    s = logits(0)
    for i in range(nb):
        s_next = logits(i + 1) if i + 1 < nb else None
        mx = jnp.max(s, axis=-1, keepdims=True)
        p = jnp.exp2(s - mx)
        pv = jnp.dot(p.astype(BF16), vaug_ref[0:(i + 1) * blk, :], preferred_element_type=F32)
        o_ref[0, i * blk:(i + 1) * blk, :] = (pv[:, 0:hd] / pv[:, hd:2 * hd]).astype(o_ref.dtype)
        s = s_next


def _moba_attention(proj, tables, rel_bias, d_model):
    bsz, seq, _ = proj.shape
    nb = seq // MOBA_BLOCK
    nh = d_model // HEAD_DIM
    assert MAX_DISTANCE <= MOBA_BLOCK + 1 and 2 * nb <= LANES and MOBA_BLOCK & (MOBA_BLOCK - 1) == 0
    assert nb % SEL_GROUP == 0 and nb % SUBLANES == 0
    kern = functools.partial(_attn_kernel, nb=nb)

    def head_cols(tile):
        return pl.BlockSpec((1, seq, HEAD_DIM), lambda b, h: (b, 0, tile * nh + h))

    return pl.pallas_call(
        kern,
        grid=(bsz, nh),
        in_specs=[pl.BlockSpec(memory_space=pltpu.SMEM),
                  head_cols(P_Q), head_cols(P_K), head_cols(P_V),
                  pl.BlockSpec((1, MOBA_BLOCK, 2 * MOBA_BLOCK), lambda b, h: (h, 0, 0))],
        out_specs=head_cols(0),
        out_shape=jax.ShapeDtypeStruct((bsz, seq, d_model), BF16),
        scratch_shapes=[pltpu.VMEM((seq, HEAD_DIM + LANES), BF16),
                        pltpu.VMEM((seq, HEAD_DIM + LANES), BF16),
                        pltpu.VMEM((seq, 2 * HEAD_DIM), BF16),
                        pltpu.VMEM((nb, HEAD_DIM), F32)],
        compiler_params=pltpu.CompilerParams(
            dimension_semantics=("arbitrary", "arbitrary"),
            vmem_limit_bytes=VMEM_LIMIT),
    )(rel_bias, proj, proj, proj, tables)


def _out_kernel(attn_ref, ga_ref, zc_ref, ma_ref, mc_ref, x_ref, mod_ref, woa_ref, woc_ref, wout_ref,
                fg_ref, o_ref):
    y_conv = jnp.dot(zc_ref[0], woc_ref[...], preferred_element_type=F32)
    z_attn = attn_ref[0].astype(F32) * ga_ref[0].astype(F32)
    y_attn = jnp.dot(z_attn.astype(BF16), woa_ref[...], preferred_element_type=F32)
    merged = ma_ref[0].astype(F32) * y_attn + mc_ref[0].astype(F32) * y_conv
    branch = jnp.dot(merged.astype(BF16), wout_ref[...], preferred_element_type=F32)
    res = x_ref[0] + mod_ref[0, 2:3, :] * branch
    inv = lax.rsqrt(jnp.mean(res * res, axis=-1, keepdims=True) + EPS)
    o_ref[0] = res * inv * fg_ref[...]


def _output_stage(attn, proj, x, mod3, woa, woc, wout, final_g, ts=1024):
    bsz, seq, d = x.shape

    def col(k):
        return pl.BlockSpec((1, ts, d), lambda b, s: (b, s, k))

    def whole(shape):
        return pl.BlockSpec(shape, lambda b, s: (0,) * len(shape), pipeline_mode=pl.Buffered(1))

    return pl.pallas_call(
        _out_kernel,
        grid=(bsz, seq // ts),
        in_specs=[col(0), col(P_GA), col(P_ZC), col(P_MA), col(P_MC),
                  col(0), pl.BlockSpec((1, 3, d), lambda b, s: (b, 0, 0)),
                  whole((d, d)), whole((d, d)), whole((d, d)), whole((1, d))],
        out_specs=col(0),
        out_shape=jax.ShapeDtypeStruct((bsz, seq, d), F32),
        compiler_params=pltpu.CompilerParams(
            dimension_semantics=("arbitrary", "arbitrary"),
            vmem_limit_bytes=VMEM_LIMIT),
    )(attn, proj, proj, proj, proj, x, mod3, woa, woc, wout, final_g.reshape(1, d))


def kernel(x, c, norm_g, w_ada, b_ada, w_in, conv_w, w_o_attn, w_o_conv, w_out, rel_bias, final_g):
    bsz, seq, d = x.shape
    depth = norm_g.shape[0]
    assert depth == 1, "the fused output stage applies the final RMSNorm after the single layer"
    assert d == N_HEADS * HEAD_DIM and w_in.shape[2] == N_PROJ * d and seq % MOBA_BLOCK == 0
    tables = _bias_tables(rel_bias)
    mod3 = _modulation(c, w_ada[0], b_ada[0]).reshape(bsz, 3, d)
    proj, woa, woc, wout = _in_projection(x, mod3, norm_g[0], conv_w[0], w_in[0].astype(BF16),
                                          (w_o_attn[0], w_o_conv[0], w_out[0]))
    attn = _moba_attention(proj, tables, rel_bias, d)
    return _output_stage(attn, proj, x, mod3, woa, woc, wout, final_g)
```

```python
import functools
import math

import jax
import jax.numpy as jnp
from jax import lax
from jax.experimental import pallas as pl
from jax.experimental.pallas import tpu as pltpu

N_HEADS = 8
HEAD_DIM = 128
CONV_K = 3
MOBA_BLOCK = 256
MOBA_TOPK = 3
N_BUCKETS = 32
MAX_DISTANCE = 128
EPS = 1e-6
N_PROJ = 10

LANES = 128
SUBLANES = 8
BF16_SUBLANES = 16
SEL_GROUP = 4
NEG = -1e30
LOG2E = math.log2(math.e)
VMEM_LIMIT = 56 * 1024 * 1024

F32 = jnp.float32
BF16 = jnp.bfloat16


def _sigmoid(v):
    return 0.5 * jnp.tanh(0.5 * v) + 0.5


def _silu(v):
    return v * _sigmoid(v)


def _split_bf16(v, terms):
    parts = []
    for _ in range(terms - 1):
        hi = v.astype(BF16).astype(F32)
        parts.append(hi)
        v = v - hi
    return parts + [v]


def _mod_kernel(c_ref, w_ref, b_ref, o_ref):
    bsz = c_ref.shape[0]
    a3 = jnp.concatenate(_split_bf16(_silu(c_ref[...]), 3), axis=0).astype(BF16)
    w_hi, w_lo = _split_bf16(w_ref[...], 2)
    r = (jnp.dot(a3, w_hi.astype(BF16), preferred_element_type=F32)
         + jnp.dot(a3, w_lo.astype(BF16), preferred_element_type=F32))
    o_ref[...] = (r[0:bsz] + r[bsz:2 * bsz]) + r[2 * bsz:3 * bsz] + b_ref[...]


def _modulation(c, w_ada, b_ada):
    bsz, d = c.shape
    n = w_ada.shape[1]
    tn = d
    return pl.pallas_call(
        _mod_kernel,
        grid=(n // tn,),
        in_specs=[pl.BlockSpec((bsz, d), lambda j: (0, 0)),
                  pl.BlockSpec((d, tn), lambda j: (0, j)),
                  pl.BlockSpec((1, tn), lambda j: (0, j))],
        out_specs=pl.BlockSpec((bsz, tn), lambda j: (0, j)),
        out_shape=jax.ShapeDtypeStruct((bsz, n), F32),
        compiler_params=pltpu.CompilerParams(vmem_limit_bytes=VMEM_LIMIT),
    )(c, w_ada, b_ada.reshape(1, n))


W_Q, W_K, W_V, W_GA, W_CB, W_CC, W_CX, W_GC, W_MA, W_MC = range(N_PROJ)
N_FUSED = 7
P_Q, P_K, P_V, P_GA, P_ZC, P_MA, P_MC = range(N_FUSED)
CONV_HALO = 8


def _inproj_kernel(x0_ref, xn_ref, mod0_ref, modn_ref, g_ref, cw_ref, w_hbm, *rest, q_scale,
                   row_chunk, n_late):
    late_in, o_ref, late_out = rest[:n_late], rest[n_late], rest[n_late + 1:2 * n_late + 1]
    h_ref, u_ref, t_ref, w_ref, stage_ref, sem = rest[2 * n_late + 1:]
    for src, dst in zip(late_in, late_out):
        dst[...] = src[...].astype(dst.dtype)
    s = pl.program_id(1)
    step = pl.program_id(0) * pl.num_programs(1) + s
    slot = lax.rem(step, 2)
    ts, d = xn_ref.shape[1], xn_ref.shape[2]
    n_chunks = ts // row_chunk
    g = g_ref[...]

    def normalise(x_ref, mod_ref, dst_slot, r):
        rows = slice(r * row_chunk, (r + 1) * row_chunk)
        xr = x_ref[0, rows, :]
        inv = lax.rsqrt(jnp.mean(xr * xr, axis=-1, keepdims=True) + EPS)
        hr = (xr * inv * g) * (1.0 + mod_ref[0, 1:2, :]) + mod_ref[0, 0:1, :]
        h_ref[dst_slot, rows, :] = hr.astype(BF16)

    def weight_chunk_copy(c):
        width = stage_ref.shape[2]
        return pltpu.make_async_copy(w_hbm.at[:, c * width:(c + 1) * width], stage_ref.at[c % 2],
                                     sem.at[c % 2])

    @pl.when(step == 0)
    def _():
        width = stage_ref.shape[2]
        n_w = w_ref.shape[1] // width
        weight_chunk_copy(0).start()
        for c in range(n_w):
            if c + 1 < n_w:
                weight_chunk_copy(c + 1).start()
            weight_chunk_copy(c).wait()
            w_ref[:, c * width:(c + 1) * width] = stage_ref[c % 2].astype(BF16)
        for r in range(n_chunks):
            normalise(x0_ref, mod0_ref, 0, r)

    def proj(j):
        return jnp.dot(h_ref[slot], w_ref[:, j * d:(j + 1) * d], preferred_element_type=F32)

    def put(k, val):
        o_ref[0, :, k * d:(k + 1) * d] = val.astype(o_ref.dtype)

    @pl.when(s == 0)
    def _():
        u_ref[0:CONV_HALO, :] = jnp.zeros((CONV_HALO, d), F32)

    def keep_cc(acc):
        t_ref[...] = acc

    def conv(acc):
        u_ref[CONV_HALO:CONV_HALO + ts, :] = t_ref[...] * acc
        t_ref[...] = (cw_ref[0:1, :] * u_ref[CONV_HALO - 2:CONV_HALO - 2 + ts, :]
                      + cw_ref[1:2, :] * u_ref[CONV_HALO - 1:CONV_HALO - 1 + ts, :]
                      + cw_ref[2:3, :] * u_ref[CONV_HALO:CONV_HALO + ts, :])
        u_ref[0:CONV_HALO, :] = u_ref[ts:ts + CONV_HALO, :]

    def gate_cb(acc):
        t_ref[...] = t_ref[...] * acc

    stages = [
        (W_GA, lambda acc: put(P_GA, _silu(acc))),
        (W_MA, lambda acc: put(P_MA, _sigmoid(acc))),
        (W_MC, lambda acc: put(P_MC, _sigmoid(acc))),
        (W_CC, keep_cc),
        (W_CX, conv),
        (W_CB, gate_cb),
        (W_GC, lambda acc: put(P_ZC, t_ref[...] * _silu(acc))),
        (W_Q, lambda acc: put(P_Q, acc * q_scale)),
        (W_K, lambda acc: put(P_K, acc)),
        (W_V, lambda acc: put(P_V, acc)),
    ]
    norm_after = list(range(len(stages) - 1 - n_chunks, len(stages) - 1))
    acc = proj(stages[0][0])
    for n, (_, epilogue) in enumerate(stages):
        nxt = proj(stages[n + 1][0]) if n + 1 < len(stages) else None
        epilogue(acc)
        if n in norm_after:
            normalise(xn_ref, modn_ref, 1 - slot, norm_after.index(n))
        acc = nxt


def _in_projection(x, mod3, norm_g, conv_w, w_in, late_weights, ts=512, row_chunk=128,
                   w_chunk=256):
    bsz, seq, d = x.shape
    n = w_in.shape[1]
    assert n % w_chunk == 0 and w_chunk % LANES == 0
    n_out = N_FUSED * d
    n_s = seq // ts
    n_steps = bsz * n_s
    assert ts // row_chunk <= N_PROJ
    slab = d // n_steps
    assert slab * n_steps == d and slab % BF16_SUBLANES == 0
    assert all(w.shape == (d, d) for w in late_weights)
    kern = functools.partial(_inproj_kernel, q_scale=LOG2E * HEAD_DIM ** -0.5, row_chunk=row_chunk,
                             n_late=len(late_weights))

    def next_tile(b, s):
        flat = jnp.minimum(b * n_s + s + 1, n_steps - 1)
        return flat // n_s, flat % n_s

    slab_spec = pl.BlockSpec((slab, d), lambda b, s: (b * n_s + s, 0))
    return pl.pallas_call(
        kern,
        grid=(bsz, n_s),
        in_specs=[pl.BlockSpec((1, ts, d), lambda b, s: (0, 0, 0), pipeline_mode=pl.Buffered(1)),
                  pl.BlockSpec((1, ts, d), lambda b, s: (*next_tile(b, s), 0)),
                  pl.BlockSpec((1, 3, d), lambda b, s: (0, 0, 0)),
                  pl.BlockSpec((1, 3, d), lambda b, s: (next_tile(b, s)[0], 0, 0)),
                  pl.BlockSpec((1, d), lambda b, s: (0, 0)),
                  pl.BlockSpec((CONV_K, d), lambda b, s: (0, 0)),
                  pl.BlockSpec(memory_space=pl.ANY)]
                 + [slab_spec] * len(late_weights),
        out_specs=[pl.BlockSpec((1, ts, n_out), lambda b, s: (b, s, 0))]
                  + [slab_spec] * len(late_weights),
        out_shape=[jax.ShapeDtypeStruct((bsz, seq, n_out), BF16)]
                  + [jax.ShapeDtypeStruct((d, d), BF16)] * len(late_weights),
        scratch_shapes=[pltpu.VMEM((2, ts, d), BF16), pltpu.VMEM((CONV_HALO + ts, d), F32),
                        pltpu.VMEM((ts, d), F32), pltpu.VMEM((d, n), BF16),
                        pltpu.VMEM((2, d, w_chunk), F32), pltpu.SemaphoreType.DMA((2,))],
        compiler_params=pltpu.CompilerParams(
            dimension_semantics=("arbitrary", "arbitrary"),
            vmem_limit_bytes=VMEM_LIMIT),
    )(x, x, mod3, mod3, norm_g.reshape(1, d), conv_w, w_in, *late_weights)


def _t5_bucket(dist):
    n = jnp.maximum(dist, 0)
    max_exact = N_BUCKETS // 2
    nf = jnp.maximum(n, 1).astype(F32)
    large = max_exact + (jnp.log(nf / max_exact) / math.log(MAX_DISTANCE / max_exact)
                         * (N_BUCKETS - max_exact)).astype(jnp.int32)
    large = jnp.minimum(large, N_BUCKETS - 1)
    return jnp.where(n < max_exact, n, large)


def _table_kernel(rb_ref, o_ref):
    n_heads, blk, width = o_ref.shape
    u = lax.broadcasted_iota(jnp.int32, (SUBLANES, width), 1)
    for t in range(2):
        dist = jnp.where(u < blk, t * blk - u, t * blk + width - u)
        bucket = _t5_bucket(dist)
        for h in range(n_heads):
            g = jnp.zeros((SUBLANES, width), F32)
            for b in range(N_BUCKETS):
                g = jnp.where(bucket == b, rb_ref[b, h] * LOG2E, g)
            g = jnp.where(dist >= 0, g, NEG)
            skewed = pltpu.roll(jnp.broadcast_to(g[0:1, :], (blk, width)), 0, 1,
                                stride=1, stride_axis=0)
            o_ref[h, :, (1 - t) * blk:(2 - t) * blk] = skewed[:, 0:blk]


def _bias_tables(rel_bias):
    shape = (N_HEADS, MOBA_BLOCK, 2 * MOBA_BLOCK)
    return pl.pallas_call(
        _table_kernel,
        grid=(1,),
        in_specs=[pl.BlockSpec(memory_space=pltpu.SMEM)],
        out_specs=pl.BlockSpec(shape, lambda i: (0, 0, 0)),
        out_shape=jax.ShapeDtypeStruct(shape, F32),
        compiler_params=pltpu.CompilerParams(vmem_limit_bytes=VMEM_LIMIT),
    )(rel_bias)


def _attn_kernel(rb_ref, q_ref, k_ref, v_ref, tab_ref, o_ref, qaug_ref, kaug_ref, vaug_ref, kmean_ref,
                 *, nb):
    h = pl.program_id(1)
    blk = MOBA_BLOCK
    hd = HEAD_DIM
    seq = nb * blk
    nt = (((1,), (1,)), ((), ()))

    lane = lax.broadcasted_iota(jnp.int32, (blk, LANES), 1)
    for n in range(nb):
        rows = slice(n * blk, (n + 1) * blk)
        kb = k_ref[0, rows, :]
        kaug_ref[rows, 0:hd] = kb
        kaug_ref[rows, hd:hd + LANES] = jnp.where((lane == n) | (lane == nb + n), 1.0, 0.0).astype(BF16)
        kmean_ref[n:n + 1, :] = jnp.mean(kb.astype(F32), axis=0, keepdims=True)
    vaug_ref[:, 0:hd] = v_ref[0]
    vaug_ref[:, hd:2 * hd] = jnp.ones((seq, hd), BF16)

    def select(g):
        q_lo = g * SEL_GROUP * blk
        w = SEL_GROUP * blk
        last_blk = (g + 1) * SEL_GROUP - 1
        nk = min(nb, -(-(last_blk + 1) // SUBLANES) * SUBLANES)

        km3 = jnp.concatenate(_split_bf16(kmean_ref[0:nk, :], 3), axis=0).astype(BF16)
        sc3 = lax.dot_general(km3, q_ref[0, q_lo:q_lo + w, :], nt, preferred_element_type=F32)
        sc = (sc3[0:nk] + sc3[nk:2 * nk]) + sc3[2 * nk:3 * nk]

        n_iota = lax.broadcasted_iota(jnp.int32, (nk, w), 0)
        q_blk = g * SEL_GROUP + lax.shift_right_logical(
            lax.broadcasted_iota(jnp.int32, (nk, w), 1), int(math.log2(blk)))
        rank = jnp.zeros((nk, w), jnp.int32)
        for m in range(last_blk):
            row = sc[m:m + 1, :]
            beats = (row > sc) | ((row == sc) & (m < n_iota))
            rank = rank + jnp.where(beats & (m < q_blk), 1, 0)
        past = n_iota < q_blk
        chosen = past & (rank < MOBA_TOPK)
        far = chosen & (n_iota < q_blk - 1)

        b_far = jnp.full((nk, w), rb_ref[N_BUCKETS - 1, h] * LOG2E, F32)
        b_hi = b_far.astype(BF16).astype(F32)
        add_hi = jnp.where(past, jnp.where(chosen, jnp.where(far, b_hi, 0.0), NEG), 0.0)
        add_lo = jnp.where(far, b_far - b_hi, 0.0)
        pieces = [add_hi, jnp.zeros((nb - nk, w), F32), add_lo, jnp.zeros((LANES - nb - nk, w), F32)]
        add_t = jnp.concatenate([p for p in pieces if p.shape[0]], axis=0)
        qaug_ref[q_lo:q_lo + w, 0:hd] = q_ref[0, q_lo:q_lo + w, :]
        qaug_ref[q_lo:q_lo + w, hd:hd + LANES] = add_t.T.astype(BF16)

    def logits(i):
        n_keys = (i + 1) * blk
        s = lax.dot_general(qaug_ref[i * blk:(i + 1) * blk, :], kaug_ref[0:n_keys, :], nt,
                            preferred_element_type=F32)
        if i == 0:
            return s + tab_ref[0, :, blk:2 * blk]
        if i == 1:
            return s + tab_ref[0]
        return jnp.concatenate([s[:, :n_keys - 2 * blk], s[:, n_keys - 2 * blk:] + tab_ref[0]], axis=1)

    for g in range(nb // SEL_GROUP):
        select(g)
    s = logits(0)
    for i in range(nb):
        s_next = logits(i + 1) if i + 1 < nb else None
        mx = jnp.max(s, axis=-1, keepdims=True)
        p = jnp.exp2(s - mx)
        pv = jnp.dot(p.astype(BF16), vaug_ref[0:(i + 1) * blk, :], preferred_element_type=F32)
        o_ref[0, i * blk:(i + 1) * blk, :] = (pv[:, 0:hd] / pv[:, hd:2 * hd]).astype(o_ref.dtype)
        s = s_next


def _moba_attention(proj, tables, rel_bias, d_model):
    bsz, seq, _ = proj.shape
    nb = seq // MOBA_BLOCK
    nh = d_model // HEAD_DIM
    assert MAX_DISTANCE <= MOBA_BLOCK + 1 and 2 * nb <= LANES and MOBA_BLOCK & (MOBA_BLOCK - 1) == 0
    assert nb % SEL_GROUP == 0 and nb % SUBLANES == 0
    kern = functools.partial(_attn_kernel, nb=nb)

    def head_cols(tile):
        return pl.BlockSpec((1, seq, HEAD_DIM), lambda b, h: (b, 0, tile * nh + h))

    return pl.pallas_call(
        kern,
        grid=(bsz, nh),
        in_specs=[pl.BlockSpec(memory_space=pltpu.SMEM),
                  head_cols(P_Q), head_cols(P_K), head_cols(P_V),
                  pl.BlockSpec((1, MOBA_BLOCK, 2 * MOBA_BLOCK), lambda b, h: (h, 0, 0))],
        out_specs=head_cols(0),
        out_shape=jax.ShapeDtypeStruct((bsz, seq, d_model), BF16),
        scratch_shapes=[pltpu.VMEM((seq, HEAD_DIM + LANES), BF16),
                        pltpu.VMEM((seq, HEAD_DIM + LANES), BF16),
                        pltpu.VMEM((seq, 2 * HEAD_DIM), BF16),
                        pltpu.VMEM((nb, HEAD_DIM), F32)],
        compiler_params=pltpu.CompilerParams(
            dimension_semantics=("arbitrary", "arbitrary"),
            vmem_limit_bytes=VMEM_LIMIT),
    )(rel_bias, proj, proj, proj, tables)


def _out_kernel(attn_ref, ga_ref, zc_ref, ma_ref, mc_ref, x_ref, mod_ref, woa_ref, woc_ref, wout_ref,
                fg_ref, o_ref):
    y_conv = jnp.dot(zc_ref[0], woc_ref[...], preferred_element_type=F32)
    z_attn = attn_ref[0].astype(F32) * ga_ref[0].astype(F32)
    y_attn = jnp.dot(z_attn.astype(BF16), woa_ref[...], preferred_element_type=F32)
    merged = ma_ref[0].astype(F32) * y_attn + mc_ref[0].astype(F32) * y_conv
    branch = jnp.dot(merged.astype(BF16), wout_ref[...], preferred_element_type=F32)
    res = x_ref[0] + mod_ref[0, 2:3, :] * branch
    inv = lax.rsqrt(jnp.mean(res * res, axis=-1, keepdims=True) + EPS)
    o_ref[0] = res * inv * fg_ref[...]


def _output_stage(attn, proj, x, mod3, woa, woc, wout, final_g, ts=1024):
    bsz, seq, d = x.shape

    def col(k):
        return pl.BlockSpec((1, ts, d), lambda b, s: (b, s, k))

    def whole(shape):
        return pl.BlockSpec(shape, lambda b, s: (0,) * len(shape), pipeline_mode=pl.Buffered(1))

    return pl.pallas_call(
        _out_kernel,
        grid=(bsz, seq // ts),
        in_specs=[col(0), col(P_GA), col(P_ZC), col(P_MA), col(P_MC),
                  col(0), pl.BlockSpec((1, 3, d), lambda b, s: (b, 0, 0)),
                  whole((d, d)), whole((d, d)), whole((d, d)), whole((1, d))],
        out_specs=col(0),
        out_shape=jax.ShapeDtypeStruct((bsz, seq, d), F32),
        compiler_params=pltpu.CompilerParams(
            dimension_semantics=("arbitrary", "arbitrary"),
            vmem_limit_bytes=VMEM_LIMIT),
    )(attn, proj, proj, proj, proj, x, mod3, woa, woc, wout, final_g.reshape(1, d))


def kernel(x, c, norm_g, w_ada, b_ada, w_in, conv_w, w_o_attn, w_o_conv, w_out, rel_bias, final_g):
    bsz, seq, d = x.shape
    depth = norm_g.shape[0]
    assert depth == 1, "the fused output stage applies the final RMSNorm after the single layer"
    assert d == N_HEADS * HEAD_DIM and w_in.shape[2] == N_PROJ * d and seq % MOBA_BLOCK == 0
    tables = _bias_tables(rel_bias)
    mod3 = _modulation(c, w_ada[0], b_ada[0]).reshape(bsz, 3, d)
    proj, woa, woc, wout = _in_projection(x, mod3, norm_g[0], conv_w[0], w_in[0],
                                          (w_o_attn[0], w_o_conv[0], w_out[0]))
    attn = _moba_attention(proj, tables, rel_bias, d)
    return _output_stage(attn, proj, x, mod3, woa, woc, wout, final_g)
```

```python
import functools
import math

import jax
import jax.numpy as jnp
from jax import lax
from jax.experimental import pallas as pl
from jax.experimental.pallas import tpu as pltpu

N_HEADS = 8
HEAD_DIM = 128
CONV_K = 3
MOBA_BLOCK = 256
MOBA_TOPK = 3
N_BUCKETS = 32
MAX_DISTANCE = 128
EPS = 1e-6
N_PROJ = 10

LANES = 128
SUBLANES = 8
BF16_SUBLANES = 16
SEL_GROUP = 4
NEG = -1e30
LOG2E = math.log2(math.e)
VMEM_LIMIT = 56 * 1024 * 1024

F32 = jnp.float32
BF16 = jnp.bfloat16


def _sigmoid(v):
    return 0.5 * jnp.tanh(0.5 * v) + 0.5


def _silu(v):
    return v * _sigmoid(v)


def _split_bf16(v, terms):
    parts = []
    for _ in range(terms - 1):
        hi = v.astype(BF16).astype(F32)
        parts.append(hi)
        v = v - hi
    return parts + [v]


def _mod_kernel(c_ref, w_ref, b_ref, o_ref):
    bsz = c_ref.shape[0]
    a3 = jnp.concatenate(_split_bf16(_silu(c_ref[...]), 3), axis=0).astype(BF16)
    w_hi, w_lo = _split_bf16(w_ref[...], 2)
    r = (jnp.dot(a3, w_hi.astype(BF16), preferred_element_type=F32)
         + jnp.dot(a3, w_lo.astype(BF16), preferred_element_type=F32))
    o_ref[...] = (r[0:bsz] + r[bsz:2 * bsz]) + r[2 * bsz:3 * bsz] + b_ref[...]


def _modulation(c, w_ada, b_ada):
    bsz, d = c.shape
    n = w_ada.shape[1]
    tn = d
    return pl.pallas_call(
        _mod_kernel,
        grid=(n // tn,),
        in_specs=[pl.BlockSpec((bsz, d), lambda j: (0, 0)),
                  pl.BlockSpec((d, tn), lambda j: (0, j)),
                  pl.BlockSpec((1, tn), lambda j: (0, j))],
        out_specs=pl.BlockSpec((bsz, tn), lambda j: (0, j)),
        out_shape=jax.ShapeDtypeStruct((bsz, n), F32),
        compiler_params=pltpu.CompilerParams(vmem_limit_bytes=VMEM_LIMIT),
    )(c, w_ada, b_ada.reshape(1, n))


W_Q, W_K, W_V, W_GA, W_CB, W_CC, W_CX, W_GC, W_MA, W_MC = range(N_PROJ)
N_FUSED = 7
P_Q, P_K, P_V, P_GA, P_ZC, P_MA, P_MC = range(N_FUSED)
CONV_HALO = 8


def _inproj_kernel(x0_ref, xn_ref, mod0_ref, modn_ref, g_ref, cw_ref, w_hbm, *rest, q_scale,
                   row_chunk, n_late):
    late_in, o_ref, late_out = rest[:n_late], rest[n_late], rest[n_late + 1:2 * n_late + 1]
    h_ref, u_ref, t_ref, w_ref, stage_ref, sem = rest[2 * n_late + 1:]
    for src, dst in zip(late_in, late_out):
        dst[...] = src[...].astype(dst.dtype)
    s = pl.program_id(1)
    step = pl.program_id(0) * pl.num_programs(1) + s
    slot = lax.rem(step, 2)
    ts, d = xn_ref.shape[1], xn_ref.shape[2]
    n_chunks = ts // row_chunk
    g = g_ref[...]

    def normalise(x_ref, mod_ref, dst_slot, r):
        rows = slice(r * row_chunk, (r + 1) * row_chunk)
        xr = x_ref[0, rows, :]
        inv = lax.rsqrt(jnp.mean(xr * xr, axis=-1, keepdims=True) + EPS)
        hr = (xr * inv * g) * (1.0 + mod_ref[0, 1:2, :]) + mod_ref[0, 0:1, :]
        h_ref[dst_slot, rows, :] = hr.astype(BF16)

    depth, _, width = stage_ref.shape

    def weight_chunk_copy(c):
        return pltpu.make_async_copy(w_hbm.at[:, c * width:(c + 1) * width],
                                     stage_ref.at[c % depth], sem.at[c % depth])

    @pl.when(step == 0)
    def _():
        n_w = w_ref.shape[1] // width
        for c in range(min(depth, n_w)):
            weight_chunk_copy(c).start()
        for c in range(n_w):
            weight_chunk_copy(c).wait()
            w_ref[:, c * width:(c + 1) * width] = stage_ref[c % depth].astype(BF16)
            if c + depth < n_w:
                weight_chunk_copy(c + depth).start()
        for r in range(n_chunks):
            normalise(x0_ref, mod0_ref, 0, r)

    def proj(j):
        return jnp.dot(h_ref[slot], w_ref[:, j * d:(j + 1) * d], preferred_element_type=F32)

    def put(k, val):
        o_ref[0, :, k * d:(k + 1) * d] = val.astype(o_ref.dtype)

    @pl.when(s == 0)
    def _():
        u_ref[0:CONV_HALO, :] = jnp.zeros((CONV_HALO, d), F32)

    def keep_cc(acc):
        t_ref[...] = acc

    def conv(acc):
        u_ref[CONV_HALO:CONV_HALO + ts, :] = t_ref[...] * acc
        t_ref[...] = (cw_ref[0:1, :] * u_ref[CONV_HALO - 2:CONV_HALO - 2 + ts, :]
                      + cw_ref[1:2, :] * u_ref[CONV_HALO - 1:CONV_HALO - 1 + ts, :]
                      + cw_ref[2:3, :] * u_ref[CONV_HALO:CONV_HALO + ts, :])
        u_ref[0:CONV_HALO, :] = u_ref[ts:ts + CONV_HALO, :]

    def gate_cb(acc):
        t_ref[...] = t_ref[...] * acc

    stages = [
        (W_GA, lambda acc: put(P_GA, _silu(acc))),
        (W_MA, lambda acc: put(P_MA, _sigmoid(acc))),
        (W_MC, lambda acc: put(P_MC, _sigmoid(acc))),
        (W_CC, keep_cc),
        (W_CX, conv),
        (W_CB, gate_cb),
        (W_GC, lambda acc: put(P_ZC, t_ref[...] * _silu(acc))),
        (W_Q, lambda acc: put(P_Q, acc * q_scale)),
        (W_K, lambda acc: put(P_K, acc)),
        (W_V, lambda acc: put(P_V, acc)),
    ]
    norm_after = list(range(len(stages) - 1 - n_chunks, len(stages) - 1))
    acc = proj(stages[0][0])
    for n, (_, epilogue) in enumerate(stages):
        nxt = proj(stages[n + 1][0]) if n + 1 < len(stages) else None
        epilogue(acc)
        if n in norm_after:
            normalise(xn_ref, modn_ref, 1 - slot, norm_after.index(n))
        acc = nxt


def _in_projection(x, mod3, norm_g, conv_w, w_in, late_weights, ts=512, row_chunk=128,
                   w_chunk=256, w_depth=4):
    bsz, seq, d = x.shape
    n = w_in.shape[1]
    assert n % w_chunk == 0 and w_chunk % LANES == 0
    n_out = N_FUSED * d
    n_s = seq // ts
    n_steps = bsz * n_s
    assert ts // row_chunk <= N_PROJ
    slab = d // n_steps
    assert slab * n_steps == d and slab % BF16_SUBLANES == 0
    assert all(w.shape == (d, d) for w in late_weights)
    kern = functools.partial(_inproj_kernel, q_scale=LOG2E * HEAD_DIM ** -0.5, row_chunk=row_chunk,
                             n_late=len(late_weights))

    def next_tile(b, s):
        flat = jnp.minimum(b * n_s + s + 1, n_steps - 1)
        return flat // n_s, flat % n_s

    slab_spec = pl.BlockSpec((slab, d), lambda b, s: (b * n_s + s, 0))
    return pl.pallas_call(
        kern,
        grid=(bsz, n_s),
        in_specs=[pl.BlockSpec((1, ts, d), lambda b, s: (0, 0, 0), pipeline_mode=pl.Buffered(1)),
                  pl.BlockSpec((1, ts, d), lambda b, s: (*next_tile(b, s), 0)),
                  pl.BlockSpec((1, 3, d), lambda b, s: (0, 0, 0)),
                  pl.BlockSpec((1, 3, d), lambda b, s: (next_tile(b, s)[0], 0, 0)),
                  pl.BlockSpec((1, d), lambda b, s: (0, 0)),
                  pl.BlockSpec((CONV_K, d), lambda b, s: (0, 0)),
                  pl.BlockSpec(memory_space=pl.ANY)]
                 + [slab_spec] * len(late_weights),
        out_specs=[pl.BlockSpec((1, ts, n_out), lambda b, s: (b, s, 0))]
                  + [slab_spec] * len(late_weights),
        out_shape=[jax.ShapeDtypeStruct((bsz, seq, n_out), BF16)]
                  + [jax.ShapeDtypeStruct((d, d), BF16)] * len(late_weights),
        scratch_shapes=[pltpu.VMEM((2, ts, d), BF16), pltpu.VMEM((CONV_HALO + ts, d), F32),
                        pltpu.VMEM((ts, d), F32), pltpu.VMEM((d, n), BF16),
                        pltpu.VMEM((w_depth, d, w_chunk), F32),
                        pltpu.SemaphoreType.DMA((w_depth,))],
        compiler_params=pltpu.CompilerParams(
            dimension_semantics=("arbitrary", "arbitrary"),
            vmem_limit_bytes=VMEM_LIMIT),
    )(x, x, mod3, mod3, norm_g.reshape(1, d), conv_w, w_in, *late_weights)


def _t5_bucket(dist):
    n = jnp.maximum(dist, 0)
    max_exact = N_BUCKETS // 2
    nf = jnp.maximum(n, 1).astype(F32)
    large = max_exact + (jnp.log(nf / max_exact) / math.log(MAX_DISTANCE / max_exact)
                         * (N_BUCKETS - max_exact)).astype(jnp.int32)
    large = jnp.minimum(large, N_BUCKETS - 1)
    return jnp.where(n < max_exact, n, large)


def _table_kernel(rb_ref, o_ref):
    n_heads, blk, width = o_ref.shape
    u = lax.broadcasted_iota(jnp.int32, (SUBLANES, width), 1)
    for t in range(2):
        dist = jnp.where(u < blk, t * blk - u, t * blk + width - u)
        bucket = _t5_bucket(dist)
        for h in range(n_heads):
            g = jnp.zeros((SUBLANES, width), F32)
            for b in range(N_BUCKETS):
                g = jnp.where(bucket == b, rb_ref[b, h] * LOG2E, g)
            g = jnp.where(dist >= 0, g, NEG)
            skewed = pltpu.roll(jnp.broadcast_to(g[0:1, :], (blk, width)), 0, 1,
                                stride=1, stride_axis=0)
            o_ref[h, :, (1 - t) * blk:(2 - t) * blk] = skewed[:, 0:blk]


def _bias_tables(rel_bias):
    shape = (N_HEADS, MOBA_BLOCK, 2 * MOBA_BLOCK)
    return pl.pallas_call(
        _table_kernel,
        grid=(1,),
        in_specs=[pl.BlockSpec(memory_space=pltpu.SMEM)],
        out_specs=pl.BlockSpec(shape, lambda i: (0, 0, 0)),
        out_shape=jax.ShapeDtypeStruct(shape, F32),
        compiler_params=pltpu.CompilerParams(vmem_limit_bytes=VMEM_LIMIT),
    )(rel_bias)


def _attn_kernel(rb_ref, q_ref, k_ref, v_ref, tab_ref, o_ref, qaug_ref, kaug_ref, vaug_ref, kmean_ref,
                 *, nb):
    h = pl.program_id(1)
    blk = MOBA_BLOCK
    hd = HEAD_DIM
    seq = nb * blk
    nt = (((1,), (1,)), ((), ()))

    lane = lax.broadcasted_iota(jnp.int32, (blk, LANES), 1)
    for n in range(nb):
        rows = slice(n * blk, (n + 1) * blk)
        kb = k_ref[0, rows, :]
        kaug_ref[rows, 0:hd] = kb
        kaug_ref[rows, hd:hd + LANES] = jnp.where((lane == n) | (lane == nb + n), 1.0, 0.0).astype(BF16)
        kmean_ref[n:n + 1, :] = jnp.mean(kb.astype(F32), axis=0, keepdims=True)
    vaug_ref[:, 0:hd] = v_ref[0]
    vaug_ref[:, hd:2 * hd] = jnp.ones((seq, hd), BF16)

    def select(g):
        q_lo = g * SEL_GROUP * blk
        w = SEL_GROUP * blk
        last_blk = (g + 1) * SEL_GROUP - 1
        nk = min(nb, -(-(last_blk + 1) // SUBLANES) * SUBLANES)

        km3 = jnp.concatenate(_split_bf16(kmean_ref[0:nk, :], 3), axis=0).astype(BF16)
        sc3 = lax.dot_general(km3, q_ref[0, q_lo:q_lo + w, :], nt, preferred_element_type=F32)
        sc = (sc3[0:nk] + sc3[nk:2 * nk]) + sc3[2 * nk:3 * nk]

        n_iota = lax.broadcasted_iota(jnp.int32, (nk, w), 0)
        q_blk = g * SEL_GROUP + lax.shift_right_logical(
            lax.broadcasted_iota(jnp.int32, (nk, w), 1), int(math.log2(blk)))
        rank = jnp.zeros((nk, w), jnp.int32)
        for m in range(last_blk):
            row = sc[m:m + 1, :]
            beats = (row > sc) | ((row == sc) & (m < n_iota))
            rank = rank + jnp.where(beats & (m < q_blk), 1, 0)
        past = n_iota < q_blk
        chosen = past & (rank < MOBA_TOPK)
        far = chosen & (n_iota < q_blk - 1)

        b_far = jnp.full((nk, w), rb_ref[N_BUCKETS - 1, h] * LOG2E, F32)
        b_hi = b_far.astype(BF16).astype(F32)
        add_hi = jnp.where(past, jnp.where(chosen, jnp.where(far, b_hi, 0.0), NEG), 0.0)
        add_lo = jnp.where(far, b_far - b_hi, 0.0)
        pieces = [add_hi, jnp.zeros((nb - nk, w), F32), add_lo, jnp.zeros((LANES - nb - nk, w), F32)]
        add_t = jnp.concatenate([p for p in pieces if p.shape[0]], axis=0)
        qaug_ref[q_lo:q_lo + w, 0:hd] = q_ref[0, q_lo:q_lo + w, :]
        qaug_ref[q_lo:q_lo + w, hd:hd + LANES] = add_t.T.astype(BF16)

    def logits(i):
        n_keys = (i + 1) * blk
        s = lax.dot_general(qaug_ref[i * blk:(i + 1) * blk, :], kaug_ref[0:n_keys, :], nt,
                            preferred_element_type=F32)
        if i == 0:
            return s + tab_ref[0, :, blk:2 * blk]
        if i == 1:
            return s + tab_ref[0]
        return jnp.concatenate([s[:, :n_keys - 2 * blk], s[:, n_keys - 2 * blk:] + tab_ref[0]], axis=1)

    for g in range(nb // SEL_GROUP):
        select(g)
    s = logits(0)
    for i in range(nb):
        s_next = logits(i + 1) if i + 1 < nb else None
        mx = jnp.max(s, axis=-1, keepdims=True)
        p = jnp.exp2(s - mx)
        pv = jnp.dot(p.astype(BF16), vaug_ref[0:(i + 1) * blk, :], preferred_element_type=F32)
        o_ref[0, i * blk:(i + 1) * blk, :] = (pv[:, 0:hd] / pv[:, hd:2 * hd]).astype(o_ref.dtype)
        s = s_next


def _moba_attention(proj, tables, rel_bias, d_model):
    bsz, seq, _ = proj.shape
    nb = seq // MOBA_BLOCK
    nh = d_model // HEAD_DIM
    assert MAX_DISTANCE <= MOBA_BLOCK + 1 and 2 * nb <= LANES and MOBA_BLOCK & (MOBA_BLOCK - 1) == 0
    assert nb % SEL_GROUP == 0 and nb % SUBLANES == 0
    kern = functools.partial(_attn_kernel, nb=nb)

    def head_cols(tile):
        return pl.BlockSpec((1, seq, HEAD_DIM), lambda b, h: (b, 0, tile * nh + h))

    return pl.pallas_call(
        kern,
        grid=(bsz, nh),
        in_specs=[pl.BlockSpec(memory_space=pltpu.SMEM),
                  head_cols(P_Q), head_cols(P_K), head_cols(P_V),
                  pl.BlockSpec((1, MOBA_BLOCK, 2 * MOBA_BLOCK), lambda b, h: (h, 0, 0))],
        out_specs=head_cols(0),
        out_shape=jax.ShapeDtypeStruct((bsz, seq, d_model), BF16),
        scratch_shapes=[pltpu.VMEM((seq, HEAD_DIM + LANES), BF16),
                        pltpu.VMEM((seq, HEAD_DIM + LANES), BF16),
                        pltpu.VMEM((seq, 2 * HEAD_DIM), BF16),
                        pltpu.VMEM((nb, HEAD_DIM), F32)],
        compiler_params=pltpu.CompilerParams(
            dimension_semantics=("arbitrary", "arbitrary"),
            vmem_limit_bytes=VMEM_LIMIT),
    )(rel_bias, proj, proj, proj, tables)


def _out_kernel(attn_ref, ga_ref, zc_ref, ma_ref, mc_ref, x_ref, mod_ref, woa_ref, woc_ref, wout_ref,
                fg_ref, o_ref):
    y_conv = jnp.dot(zc_ref[0], woc_ref[...], preferred_element_type=F32)
    z_attn = attn_ref[0].astype(F32) * ga_ref[0].astype(F32)
    y_attn = jnp.dot(z_attn.astype(BF16), woa_ref[...], preferred_element_type=F32)
    merged = ma_ref[0].astype(F32) * y_attn + mc_ref[0].astype(F32) * y_conv
    branch = jnp.dot(merged.astype(BF16), wout_ref[...], preferred_element_type=F32)
    res = x_ref[0] + mod_ref[0, 2:3, :] * branch
    inv = lax.rsqrt(jnp.mean(res * res, axis=-1, keepdims=True) + EPS)
    o_ref[0] = res * inv * fg_ref[...]


def _output_stage(attn, proj, x, mod3, woa, woc, wout, final_g, ts=1024):
    bsz, seq, d = x.shape

    def col(k):
        return pl.BlockSpec((1, ts, d), lambda b, s: (b, s, k))

    def whole(shape):
        return pl.BlockSpec(shape, lambda b, s: (0,) * len(shape), pipeline_mode=pl.Buffered(1))

    return pl.pallas_call(
        _out_kernel,
        grid=(bsz, seq // ts),
        in_specs=[col(0), col(P_GA), col(P_ZC), col(P_MA), col(P_MC),
                  col(0), pl.BlockSpec((1, 3, d), lambda b, s: (b, 0, 0)),
                  whole((d, d)), whole((d, d)), whole((d, d)), whole((1, d))],
        out_specs=col(0),
        out_shape=jax.ShapeDtypeStruct((bsz, seq, d), F32),
        compiler_params=pltpu.CompilerParams(
            dimension_semantics=("arbitrary", "arbitrary"),
            vmem_limit_bytes=VMEM_LIMIT),
    )(attn, proj, proj, proj, proj, x, mod3, woa, woc, wout, final_g.reshape(1, d))


def kernel(x, c, norm_g, w_ada, b_ada, w_in, conv_w, w_o_attn, w_o_conv, w_out, rel_bias, final_g):
    bsz, seq, d = x.shape
    depth = norm_g.shape[0]
    assert depth == 1, "the fused output stage applies the final RMSNorm after the single layer"
    assert d == N_HEADS * HEAD_DIM and w_in.shape[2] == N_PROJ * d and seq % MOBA_BLOCK == 0
    tables = _bias_tables(rel_bias)
    mod3 = _modulation(c, w_ada[0], b_ada[0]).reshape(bsz, 3, d)
    proj, woa, woc, wout = _in_projection(x, mod3, norm_g[0], conv_w[0], w_in[0],
                                          (w_o_attn[0], w_o_conv[0], w_out[0]))
    attn = _moba_attention(proj, tables, rel_bias, d)
    return _output_stage(attn, proj, x, mod3, woa, woc, wout, final_g)
```

```python
import functools
import math

import jax
import jax.numpy as jnp
from jax import lax
from jax.experimental import pallas as pl
from jax.experimental.pallas import tpu as pltpu

N_HEADS = 8
HEAD_DIM = 128
CONV_K = 3
MOBA_BLOCK = 256
MOBA_TOPK = 3
N_BUCKETS = 32
MAX_DISTANCE = 128
EPS = 1e-6
N_PROJ = 10

LANES = 128
SUBLANES = 8
BF16_SUBLANES = 16
SEL_GROUP = 4
NEG = -1e30
LOG2E = math.log2(math.e)
VMEM_LIMIT = 56 * 1024 * 1024

F32 = jnp.float32
BF16 = jnp.bfloat16


def _sigmoid(v):
    return 0.5 * jnp.tanh(0.5 * v) + 0.5


def _silu(v):
    return v * _sigmoid(v)


def _split_bf16(v, terms):
    parts = []
    for _ in range(terms - 1):
        hi = v.astype(BF16).astype(F32)
        parts.append(hi)
        v = v - hi
    return parts + [v]


def _mod_kernel(c_ref, w_ref, b_ref, o_ref):
    bsz = c_ref.shape[0]
    a3 = jnp.concatenate(_split_bf16(_silu(c_ref[...]), 3), axis=0).astype(BF16)
    w_hi, w_lo = _split_bf16(w_ref[...], 2)
    r = (jnp.dot(a3, w_hi.astype(BF16), preferred_element_type=F32)
         + jnp.dot(a3, w_lo.astype(BF16), preferred_element_type=F32))
    o_ref[...] = (r[0:bsz] + r[bsz:2 * bsz]) + r[2 * bsz:3 * bsz] + b_ref[...]


def _modulation(c, w_ada, b_ada):
    bsz, d = c.shape
    n = w_ada.shape[1]
    tn = d
    return pl.pallas_call(
        _mod_kernel,
        grid=(n // tn,),
        in_specs=[pl.BlockSpec((bsz, d), lambda j: (0, 0)),
                  pl.BlockSpec((d, tn), lambda j: (0, j)),
                  pl.BlockSpec((1, tn), lambda j: (0, j))],
        out_specs=pl.BlockSpec((bsz, tn), lambda j: (0, j)),
        out_shape=jax.ShapeDtypeStruct((bsz, n), F32),
        compiler_params=pltpu.CompilerParams(vmem_limit_bytes=VMEM_LIMIT),
    )(c, w_ada, b_ada.reshape(1, n))


W_Q, W_K, W_V, W_GA, W_CB, W_CC, W_CX, W_GC, W_MA, W_MC = range(N_PROJ)
N_FUSED = 7
P_Q, P_K, P_V, P_GA, P_ZC, P_MA, P_MC = range(N_FUSED)
CONV_HALO = 8


def _inproj_kernel(x0_ref, xn_ref, mod0_ref, modn_ref, g_ref, cw_ref, w_hbm, *rest, q_scale,
                   row_chunk, n_late):
    late_in, o_ref, late_out = rest[:n_late], rest[n_late], rest[n_late + 1:2 * n_late + 1]
    h_ref, u_ref, t_ref, w_ref, stage_ref, sem = rest[2 * n_late + 1:]
    for src, dst in zip(late_in, late_out):
        dst[...] = src[...].astype(dst.dtype)
    s = pl.program_id(1)
    step = pl.program_id(0) * pl.num_programs(1) + s
    slot = lax.rem(step, 2)
    ts, d = xn_ref.shape[1], xn_ref.shape[2]
    n_chunks = ts // row_chunk
    g = g_ref[...]

    def normalise(x_ref, mod_ref, dst_slot, r):
        rows = slice(r * row_chunk, (r + 1) * row_chunk)
        xr = x_ref[0, rows, :]
        inv = lax.rsqrt(jnp.mean(xr * xr, axis=-1, keepdims=True) + EPS)
        hr = (xr * inv * g) * (1.0 + mod_ref[0, 1:2, :]) + mod_ref[0, 0:1, :]
        h_ref[dst_slot, rows, :] = hr.astype(BF16)

    depth, _, width = stage_ref.shape

    def weight_chunk_copy(c):
        return pltpu.make_async_copy(w_hbm.at[:, c * width:(c + 1) * width],
                                     stage_ref.at[c % depth], sem.at[c % depth])

    @pl.when(step == 0)
    def _():
        n_w = w_ref.shape[1] // width
        for c in range(min(depth, n_w)):
            weight_chunk_copy(c).start()
        for c in range(n_w):
            weight_chunk_copy(c).wait()
            w_ref[:, c * width:(c + 1) * width] = stage_ref[c % depth].astype(BF16)
            if c + depth < n_w:
                weight_chunk_copy(c + depth).start()
        for r in range(n_chunks):
            normalise(x0_ref, mod0_ref, 0, r)

    def proj(j):
        return jnp.dot(h_ref[slot], w_ref[:, j * d:(j + 1) * d], preferred_element_type=F32)

    def put(k, val):
        o_ref[0, :, k * d:(k + 1) * d] = val.astype(o_ref.dtype)

    @pl.when(s == 0)
    def _():
        u_ref[0:CONV_HALO, :] = jnp.zeros((CONV_HALO, d), F32)

    def keep_cc(acc):
        t_ref[...] = acc

    def conv(acc):
        u_ref[CONV_HALO:CONV_HALO + ts, :] = t_ref[...] * acc
        t_ref[...] = (cw_ref[0:1, :] * u_ref[CONV_HALO - 2:CONV_HALO - 2 + ts, :]
                      + cw_ref[1:2, :] * u_ref[CONV_HALO - 1:CONV_HALO - 1 + ts, :]
                      + cw_ref[2:3, :] * u_ref[CONV_HALO:CONV_HALO + ts, :])
        u_ref[0:CONV_HALO, :] = u_ref[ts:ts + CONV_HALO, :]

    def gate_cb(acc):
        t_ref[...] = t_ref[...] * acc

    stages = [
        (W_GA, lambda acc: put(P_GA, _silu(acc))),
        (W_MA, lambda acc: put(P_MA, _sigmoid(acc))),
        (W_MC, lambda acc: put(P_MC, _sigmoid(acc))),
        (W_CC, keep_cc),
        (W_CX, conv),
        (W_CB, gate_cb),
        (W_GC, lambda acc: put(P_ZC, t_ref[...] * _silu(acc))),
        (W_Q, lambda acc: put(P_Q, acc * q_scale)),
        (W_K, lambda acc: put(P_K, acc)),
        (W_V, lambda acc: put(P_V, acc)),
    ]
    norm_after = list(range(len(stages) - 1 - n_chunks, len(stages) - 1))
    acc = proj(stages[0][0])
    for n, (_, epilogue) in enumerate(stages):
        nxt = proj(stages[n + 1][0]) if n + 1 < len(stages) else None
        epilogue(acc)
        if n in norm_after:
            normalise(xn_ref, modn_ref, 1 - slot, norm_after.index(n))
        acc = nxt


def _in_projection(x, mod3, norm_g, conv_w, w_in, late_weights, ts=512, row_chunk=128,
                   w_chunk=256, w_depth=4):
    bsz, seq, d = x.shape
    n = w_in.shape[1]
    assert n % w_chunk == 0 and w_chunk % LANES == 0
    n_out = N_FUSED * d
    n_s = seq // ts
    n_steps = bsz * n_s
    assert ts // row_chunk <= N_PROJ
    slab = d // n_steps
    assert slab * n_steps == d and slab % BF16_SUBLANES == 0
    assert all(w.shape == (d, d) for w in late_weights)
    kern = functools.partial(_inproj_kernel, q_scale=LOG2E * HEAD_DIM ** -0.5, row_chunk=row_chunk,
                             n_late=len(late_weights))

    def next_tile(b, s):
        flat = jnp.minimum(b * n_s + s + 1, n_steps - 1)
        return flat // n_s, flat % n_s

    slab_spec = pl.BlockSpec((slab, d), lambda b, s: (b * n_s + s, 0))
    return pl.pallas_call(
        kern,
        grid=(bsz, n_s),
        in_specs=[pl.BlockSpec((1, ts, d), lambda b, s: (0, 0, 0), pipeline_mode=pl.Buffered(1)),
                  pl.BlockSpec((1, ts, d), lambda b, s: (*next_tile(b, s), 0)),
                  pl.BlockSpec((1, 3, d), lambda b, s: (0, 0, 0)),
                  pl.BlockSpec((1, 3, d), lambda b, s: (next_tile(b, s)[0], 0, 0)),
                  pl.BlockSpec((1, d), lambda b, s: (0, 0)),
                  pl.BlockSpec((CONV_K, d), lambda b, s: (0, 0)),
                  pl.BlockSpec(memory_space=pl.ANY)]
                 + [slab_spec] * len(late_weights),
        out_specs=[pl.BlockSpec((1, ts, n_out), lambda b, s: (b, s, 0))]
                  + [slab_spec] * len(late_weights),
        out_shape=[jax.ShapeDtypeStruct((bsz, seq, n_out), BF16)]
                  + [jax.ShapeDtypeStruct((d, d), BF16)] * len(late_weights),
        scratch_shapes=[pltpu.VMEM((2, ts, d), BF16), pltpu.VMEM((CONV_HALO + ts, d), F32),
                        pltpu.VMEM((ts, d), F32), pltpu.VMEM((d, n), BF16),
                        pltpu.VMEM((w_depth, d, w_chunk), F32),
                        pltpu.SemaphoreType.DMA((w_depth,))],
        compiler_params=pltpu.CompilerParams(
            dimension_semantics=("arbitrary", "arbitrary"),
            vmem_limit_bytes=VMEM_LIMIT),
    )(x, x, mod3, mod3, norm_g.reshape(1, d), conv_w, w_in, *late_weights)


def _t5_bucket(dist):
    n = jnp.maximum(dist, 0)
    max_exact = N_BUCKETS // 2
    nf = jnp.maximum(n, 1).astype(F32)
    large = max_exact + (jnp.log(nf / max_exact) / math.log(MAX_DISTANCE / max_exact)
                         * (N_BUCKETS - max_exact)).astype(jnp.int32)
    large = jnp.minimum(large, N_BUCKETS - 1)
    return jnp.where(n < max_exact, n, large)


def _table_kernel(rb_ref, o_ref):
    n_heads, blk, width = o_ref.shape
    u = lax.broadcasted_iota(jnp.int32, (SUBLANES, width), 1)
    for t in range(2):
        dist = jnp.where(u < blk, t * blk - u, t * blk + width - u)
        bucket = _t5_bucket(dist)
        for h in range(n_heads):
            g = jnp.zeros((SUBLANES, width), F32)
            for b in range(N_BUCKETS):
                g = jnp.where(bucket == b, rb_ref[b, h] * LOG2E, g)
            g = jnp.where(dist >= 0, g, NEG)
            skewed = pltpu.roll(jnp.broadcast_to(g[0:1, :], (blk, width)), 0, 1,
                                stride=1, stride_axis=0)
            o_ref[h, :, (1 - t) * blk:(2 - t) * blk] = skewed[:, 0:blk]


def _bias_tables(rel_bias):
    shape = (N_HEADS, MOBA_BLOCK, 2 * MOBA_BLOCK)
    return pl.pallas_call(
        _table_kernel,
        grid=(1,),
        in_specs=[pl.BlockSpec(memory_space=pltpu.SMEM)],
        out_specs=pl.BlockSpec(shape, lambda i: (0, 0, 0)),
        out_shape=jax.ShapeDtypeStruct(shape, F32),
        compiler_params=pltpu.CompilerParams(vmem_limit_bytes=VMEM_LIMIT),
    )(rel_bias)


def _attn_kernel(rb_ref, q_ref, k_ref, v_ref, tab_ref, o_ref, qaug_ref, kaug_ref, vaug_ref, kmean_ref,
                 *, nb):
    h = pl.program_id(1)
    blk = MOBA_BLOCK
    hd = HEAD_DIM
    seq = nb * blk
    nt = (((1,), (1,)), ((), ()))

    lane = lax.broadcasted_iota(jnp.int32, (blk, LANES), 1)
    for n in range(nb):
        rows = slice(n * blk, (n + 1) * blk)
        kb = k_ref[0, rows, :]
        kaug_ref[rows, 0:hd] = kb
        kaug_ref[rows, hd:hd + LANES] = jnp.where((lane == n) | (lane == nb + n), 1.0, 0.0).astype(BF16)
        kmean_ref[n:n + 1, :] = jnp.mean(kb.astype(F32), axis=0, keepdims=True)
    vaug_ref[:, 0:hd] = v_ref[0]
    vaug_ref[:, hd:2 * hd] = jnp.ones((seq, hd), BF16)

    def select(g):
        q_lo = g * SEL_GROUP * blk
        w = SEL_GROUP * blk
        last_blk = (g + 1) * SEL_GROUP - 1
        nk = min(nb, -(-(last_blk + 1) // SUBLANES) * SUBLANES)

        km3 = jnp.concatenate(_split_bf16(kmean_ref[0:nk, :], 3), axis=0).astype(BF16)
        sc3 = lax.dot_general(km3, q_ref[0, q_lo:q_lo + w, :], nt, preferred_element_type=F32)
        sc = (sc3[0:nk] + sc3[nk:2 * nk]) + sc3[2 * nk:3 * nk]

        n_iota = lax.broadcasted_iota(jnp.int32, (nk, w), 0)
        q_blk = g * SEL_GROUP + lax.shift_right_logical(
            lax.broadcasted_iota(jnp.int32, (nk, w), 1), int(math.log2(blk)))
        rank = jnp.zeros((nk, w), jnp.int32)
        for m in range(last_blk):
            row = sc[m:m + 1, :]
            beats = (row > sc) | ((row == sc) & (m < n_iota))
            rank = rank + jnp.where(beats & (m < q_blk), 1, 0)
        past = n_iota < q_blk
        chosen = past & (rank < MOBA_TOPK)
        far = chosen & (n_iota < q_blk - 1)

        b_far = jnp.full((nk, w), rb_ref[N_BUCKETS - 1, h] * LOG2E, F32)
        b_hi = b_far.astype(BF16).astype(F32)
        add_hi = jnp.where(past, jnp.where(chosen, jnp.where(far, b_hi, 0.0), NEG), 0.0)
        add_lo = jnp.where(far, b_far - b_hi, 0.0)
        pieces = [add_hi, jnp.zeros((nb - nk, w), F32), add_lo, jnp.zeros((LANES - nb - nk, w), F32)]
        add_t = jnp.concatenate([p for p in pieces if p.shape[0]], axis=0)
        qaug_ref[q_lo:q_lo + w, 0:hd] = q_ref[0, q_lo:q_lo + w, :]
        qaug_ref[q_lo:q_lo + w, hd:hd + LANES] = add_t.T.astype(BF16)

    def logits(i):
        n_keys = (i + 1) * blk
        s = lax.dot_general(qaug_ref[i * blk:(i + 1) * blk, :], kaug_ref[0:n_keys, :], nt,
                            preferred_element_type=F32)
        if i == 0:
            return s + tab_ref[0, :, blk:2 * blk]
        if i == 1:
            return s + tab_ref[0]
        return jnp.concatenate([s[:, :n_keys - 2 * blk], s[:, n_keys - 2 * blk:] + tab_ref[0]], axis=1)

    for g in range(nb // SEL_GROUP):
        select(g)
    s = logits(0)
    for i in range(nb):
        s_next = logits(i + 1) if i + 1 < nb else None
        mx = jnp.max(s, axis=-1, keepdims=True)
        p = jnp.exp2(s - mx)
        pv = jnp.dot(p.astype(BF16), vaug_ref[0:(i + 1) * blk, :], preferred_element_type=F32)
        o_ref[0, i * blk:(i + 1) * blk, :] = (pv[:, 0:hd] / pv[:, hd:2 * hd]).astype(o_ref.dtype)
        s = s_next


def _moba_attention(proj, tables, rel_bias, d_model):
    bsz, seq, _ = proj.shape
    nb = seq // MOBA_BLOCK
    nh = d_model // HEAD_DIM
    assert MAX_DISTANCE <= MOBA_BLOCK + 1 and 2 * nb <= LANES and MOBA_BLOCK & (MOBA_BLOCK - 1) == 0
    assert nb % SEL_GROUP == 0 and nb % SUBLANES == 0
    kern = functools.partial(_attn_kernel, nb=nb)

    def head_cols(tile):
        return pl.BlockSpec((1, seq, HEAD_DIM), lambda b, h: (b, 0, tile * nh + h))

    return pl.pallas_call(
        kern,
        grid=(bsz, nh),
        in_specs=[pl.BlockSpec(memory_space=pltpu.SMEM),
                  head_cols(P_Q), head_cols(P_K), head_cols(P_V),
                  pl.BlockSpec((1, MOBA_BLOCK, 2 * MOBA_BLOCK), lambda b, h: (h, 0, 0))],
        out_specs=head_cols(0),
        out_shape=jax.ShapeDtypeStruct((bsz, seq, d_model), BF16),
        scratch_shapes=[pltpu.VMEM((seq, HEAD_DIM + LANES), BF16),
                        pltpu.VMEM((seq, HEAD_DIM + LANES), BF16),
                        pltpu.VMEM((seq, 2 * HEAD_DIM), BF16),
                        pltpu.VMEM((nb, HEAD_DIM), F32)],
        compiler_params=pltpu.CompilerParams(
            dimension_semantics=("arbitrary", "arbitrary"),
            vmem_limit_bytes=VMEM_LIMIT),
    )(rel_bias, proj, proj, proj, tables)


def _out_kernel(attn_ref, ga_ref, zc_ref, ma_ref, mc_ref, x_ref, mod_ref, woa_ref, woc_ref, wout_ref,
                fg_ref, o_ref, *, row_chunk):
    def merged_branches(r):
        rows = slice(r * row_chunk, (r + 1) * row_chunk)
        y_conv = jnp.dot(zc_ref[0, rows, :], woc_ref[...], preferred_element_type=F32)
        z_attn = attn_ref[0, rows, :].astype(F32) * ga_ref[0, rows, :].astype(F32)
        y_attn = jnp.dot(z_attn.astype(BF16), woa_ref[...], preferred_element_type=F32)
        merged = (ma_ref[0, rows, :].astype(F32) * y_attn
                  + mc_ref[0, rows, :].astype(F32) * y_conv)
        return merged.astype(BF16)

    def residual_norm(r, branch):
        rows = slice(r * row_chunk, (r + 1) * row_chunk)
        res = x_ref[0, rows, :] + mod_ref[0, 2:3, :] * branch
        inv = lax.rsqrt(jnp.mean(res * res, axis=-1, keepdims=True) + EPS)
        o_ref[0, rows, :] = res * inv * fg_ref[...]

    n_chunks = x_ref.shape[1] // row_chunk
    merged = merged_branches(0)
    for r in range(n_chunks):
        branch = jnp.dot(merged, wout_ref[...], preferred_element_type=F32)
        merged = merged_branches(r + 1) if r + 1 < n_chunks else None
        residual_norm(r, branch)


def _output_stage(attn, proj, x, mod3, woa, woc, wout, final_g, ts=1024, row_chunk=256):
    bsz, seq, d = x.shape
    assert ts % row_chunk == 0

    def col(k):
        return pl.BlockSpec((1, ts, d), lambda b, s: (b, s, k))

    def whole(shape):
        return pl.BlockSpec(shape, lambda b, s: (0,) * len(shape), pipeline_mode=pl.Buffered(1))

    return pl.pallas_call(
        functools.partial(_out_kernel, row_chunk=row_chunk),
        grid=(bsz, seq // ts),
        in_specs=[col(0), col(P_GA), col(P_ZC), col(P_MA), col(P_MC),
                  col(0), pl.BlockSpec((1, 3, d), lambda b, s: (b, 0, 0)),
                  whole((d, d)), whole((d, d)), whole((d, d)), whole((1, d))],
        out_specs=col(0),
        out_shape=jax.ShapeDtypeStruct((bsz, seq, d), F32),
        compiler_params=pltpu.CompilerParams(
            dimension_semantics=("arbitrary", "arbitrary"),
            vmem_limit_bytes=VMEM_LIMIT),
    )(attn, proj, proj, proj, proj, x, mod3, woa, woc, wout, final_g.reshape(1, d))


def kernel(x, c, norm_g, w_ada, b_ada, w_in, conv_w, w_o_attn, w_o_conv, w_out, rel_bias, final_g):
    bsz, seq, d = x.shape
    depth = norm_g.shape[0]
    assert depth == 1, "the fused output stage applies the final RMSNorm after the single layer"
    assert d == N_HEADS * HEAD_DIM and w_in.shape[2] == N_PROJ * d and seq % MOBA_BLOCK == 0
    tables = _bias_tables(rel_bias)
    mod3 = _modulation(c, w_ada[0], b_ada[0]).reshape(bsz, 3, d)
    proj, woa, woc, wout = _in_projection(x, mod3, norm_g[0], conv_w[0], w_in[0],
                                          (w_o_attn[0], w_o_conv[0], w_out[0]))
    attn = _moba_attention(proj, tables, rel_bias, d)
    return _output_stage(attn, proj, x, mod3, woa, woc, wout, final_g)
```

```python
import functools
import math

import jax
import jax.numpy as jnp
from jax import lax
from jax.experimental import pallas as pl
from jax.experimental.pallas import tpu as pltpu

N_HEADS = 8
HEAD_DIM = 128
CONV_K = 3
MOBA_BLOCK = 256
MOBA_TOPK = 3
N_BUCKETS = 32
MAX_DISTANCE = 128
EPS = 1e-6
N_PROJ = 10

LANES = 128
SUBLANES = 8
BF16_SUBLANES = 16
SEL_GROUP = 4
NEG = -1e30
LOG2E = math.log2(math.e)
VMEM_LIMIT = 56 * 1024 * 1024

F32 = jnp.float32
BF16 = jnp.bfloat16


def _sigmoid(v):
    return 0.5 * jnp.tanh(0.5 * v) + 0.5


def _silu(v):
    return v * _sigmoid(v)


def _split_bf16(v, terms):
    parts = []
    for _ in range(terms - 1):
        hi = v.astype(BF16).astype(F32)
        parts.append(hi)
        v = v - hi
    return parts + [v]


def _mod_kernel(c_ref, w_ref, b_ref, o_ref):
    bsz = c_ref.shape[0]
    a3 = jnp.concatenate(_split_bf16(_silu(c_ref[...]), 3), axis=0).astype(BF16)
    w_hi, w_lo = _split_bf16(w_ref[...], 2)
    r = (jnp.dot(a3, w_hi.astype(BF16), preferred_element_type=F32)
         + jnp.dot(a3, w_lo.astype(BF16), preferred_element_type=F32))
    o_ref[...] = (r[0:bsz] + r[bsz:2 * bsz]) + r[2 * bsz:3 * bsz] + b_ref[...]


def _modulation(c, w_ada, b_ada):
    bsz, d = c.shape
    n = w_ada.shape[1]
    tn = d
    return pl.pallas_call(
        _mod_kernel,
        grid=(n // tn,),
        in_specs=[pl.BlockSpec((bsz, d), lambda j: (0, 0)),
                  pl.BlockSpec((d, tn), lambda j: (0, j)),
                  pl.BlockSpec((1, tn), lambda j: (0, j))],
        out_specs=pl.BlockSpec((bsz, tn), lambda j: (0, j)),
        out_shape=jax.ShapeDtypeStruct((bsz, n), F32),
        compiler_params=pltpu.CompilerParams(vmem_limit_bytes=VMEM_LIMIT),
    )(c, w_ada, b_ada.reshape(1, n))


W_Q, W_K, W_V, W_GA, W_CB, W_CC, W_CX, W_GC, W_MA, W_MC = range(N_PROJ)
N_FUSED = 7
P_Q, P_K, P_V, P_GA, P_ZC, P_MA, P_MC = range(N_FUSED)
CONV_HALO = 8


def _inproj_kernel(x0_ref, xn_ref, mod0_ref, modn_ref, g_ref, cw_ref, w_hbm, *rest, q_scale,
                   row_chunk, n_late):
    late_in, o_ref, late_out = rest[:n_late], rest[n_late], rest[n_late + 1:2 * n_late + 1]
    h_ref, u_ref, t_ref, w_ref, stage_ref, sem = rest[2 * n_late + 1:]
    for src, dst in zip(late_in, late_out):
        dst[...] = src[...].astype(dst.dtype)
    s = pl.program_id(1)
    step = pl.program_id(0) * pl.num_programs(1) + s
    slot = lax.rem(step, 2)
    ts, d = xn_ref.shape[1], xn_ref.shape[2]
    n_chunks = ts // row_chunk
    g = g_ref[...]

    def normalise(x_ref, mod_ref, dst_slot, r):
        rows = slice(r * row_chunk, (r + 1) * row_chunk)
        xr = x_ref[0, rows, :]
        inv = lax.rsqrt(jnp.mean(xr * xr, axis=-1, keepdims=True) + EPS)
        hr = (xr * inv * g) * (1.0 + mod_ref[0, 1:2, :]) + mod_ref[0, 0:1, :]
        h_ref[dst_slot, rows, :] = hr.astype(BF16)

    depth, _, width = stage_ref.shape

    def weight_chunk_copy(c):
        return pltpu.make_async_copy(w_hbm.at[:, c * width:(c + 1) * width],
                                     stage_ref.at[c % depth], sem.at[c % depth])

    @pl.when(step == 0)
    def _():
        n_w = w_ref.shape[1] // width
        for c in range(min(depth, n_w)):
            weight_chunk_copy(c).start()
        for c in range(n_w):
            weight_chunk_copy(c).wait()
            w_ref[:, c * width:(c + 1) * width] = stage_ref[c % depth].astype(BF16)
            if c + depth < n_w:
                weight_chunk_copy(c + depth).start()
        for r in range(n_chunks):
            normalise(x0_ref, mod0_ref, 0, r)

    def proj(j):
        return jnp.dot(h_ref[slot], w_ref[:, j * d:(j + 1) * d], preferred_element_type=F32)

    def put(k, val):
        o_ref[0, :, k * d:(k + 1) * d] = val.astype(o_ref.dtype)

    @pl.when(s == 0)
    def _():
        u_ref[0:CONV_HALO, :] = jnp.zeros((CONV_HALO, d), F32)

    def keep_cc(acc):
        t_ref[...] = acc

    def conv(acc):
        u_ref[CONV_HALO:CONV_HALO + ts, :] = t_ref[...] * acc
        t_ref[...] = (cw_ref[0:1, :] * u_ref[CONV_HALO - 2:CONV_HALO - 2 + ts, :]
                      + cw_ref[1:2, :] * u_ref[CONV_HALO - 1:CONV_HALO - 1 + ts, :]
                      + cw_ref[2:3, :] * u_ref[CONV_HALO:CONV_HALO + ts, :])
        u_ref[0:CONV_HALO, :] = u_ref[ts:ts + CONV_HALO, :]

    def gate_cb(acc):
        t_ref[...] = t_ref[...] * acc

    stages = [
        (W_GA, lambda acc: put(P_GA, _silu(acc))),
        (W_MA, lambda acc: put(P_MA, _sigmoid(acc))),
        (W_MC, lambda acc: put(P_MC, _sigmoid(acc))),
        (W_CC, keep_cc),
        (W_CX, conv),
        (W_CB, gate_cb),
        (W_GC, lambda acc: put(P_ZC, t_ref[...] * _silu(acc))),
        (W_Q, lambda acc: put(P_Q, acc * q_scale)),
        (W_K, lambda acc: put(P_K, acc)),
        (W_V, lambda acc: put(P_V, acc)),
    ]
    norm_after = list(range(len(stages) - 1 - n_chunks, len(stages) - 1))
    acc = proj(stages[0][0])
    for n, (_, epilogue) in enumerate(stages):
        nxt = proj(stages[n + 1][0]) if n + 1 < len(stages) else None
        epilogue(acc)
        if n in norm_after:
            normalise(xn_ref, modn_ref, 1 - slot, norm_after.index(n))
        acc = nxt


def _in_projection(x, mod3, norm_g, conv_w, w_in, late_weights, ts=512, row_chunk=128,
                   w_chunk=256, w_depth=4):
    bsz, seq, d = x.shape
    n = w_in.shape[1]
    assert n % w_chunk == 0 and w_chunk % LANES == 0
    n_out = N_FUSED * d
    n_s = seq // ts
    n_steps = bsz * n_s
    assert ts // row_chunk <= N_PROJ
    slab = d // n_steps
    assert slab * n_steps == d and slab % BF16_SUBLANES == 0
    assert all(w.shape == (d, d) for w in late_weights)
    kern = functools.partial(_inproj_kernel, q_scale=LOG2E * HEAD_DIM ** -0.5, row_chunk=row_chunk,
                             n_late=len(late_weights))

    def next_tile(b, s):
        flat = jnp.minimum(b * n_s + s + 1, n_steps - 1)
        return flat // n_s, flat % n_s

    slab_spec = pl.BlockSpec((slab, d), lambda b, s: (b * n_s + s, 0))
    return pl.pallas_call(
        kern,
        grid=(bsz, n_s),
        in_specs=[pl.BlockSpec((1, ts, d), lambda b, s: (0, 0, 0), pipeline_mode=pl.Buffered(1)),
                  pl.BlockSpec((1, ts, d), lambda b, s: (*next_tile(b, s), 0)),
                  pl.BlockSpec((1, 3, d), lambda b, s: (0, 0, 0)),
                  pl.BlockSpec((1, 3, d), lambda b, s: (next_tile(b, s)[0], 0, 0)),
                  pl.BlockSpec((1, d), lambda b, s: (0, 0)),
                  pl.BlockSpec((CONV_K, d), lambda b, s: (0, 0)),
                  pl.BlockSpec(memory_space=pl.ANY)]
                 + [slab_spec] * len(late_weights),
        out_specs=[pl.BlockSpec((1, ts, n_out), lambda b, s: (b, s, 0))]
                  + [slab_spec] * len(late_weights),
        out_shape=[jax.ShapeDtypeStruct((bsz, seq, n_out), BF16)]
                  + [jax.ShapeDtypeStruct((d, d), BF16)] * len(late_weights),
        scratch_shapes=[pltpu.VMEM((2, ts, d), BF16), pltpu.VMEM((CONV_HALO + ts, d), F32),
                        pltpu.VMEM((ts, d), F32), pltpu.VMEM((d, n), BF16),
                        pltpu.VMEM((w_depth, d, w_chunk), F32),
                        pltpu.SemaphoreType.DMA((w_depth,))],
        compiler_params=pltpu.CompilerParams(
            dimension_semantics=("arbitrary", "arbitrary"),
            vmem_limit_bytes=VMEM_LIMIT),
    )(x, x, mod3, mod3, norm_g.reshape(1, d), conv_w, w_in, *late_weights)


def _t5_bucket(dist):
    n = jnp.maximum(dist, 0)
    max_exact = N_BUCKETS // 2
    nf = jnp.maximum(n, 1).astype(F32)
    large = max_exact + (jnp.log(nf / max_exact) / math.log(MAX_DISTANCE / max_exact)
                         * (N_BUCKETS - max_exact)).astype(jnp.int32)
    large = jnp.minimum(large, N_BUCKETS - 1)
    return jnp.where(n < max_exact, n, large)


def _table_kernel(rb_ref, o_ref):
    n_heads, blk, width = o_ref.shape
    u = lax.broadcasted_iota(jnp.int32, (SUBLANES, width), 1)
    for t in range(2):
        dist = jnp.where(u < blk, t * blk - u, t * blk + width - u)
        bucket = _t5_bucket(dist)
        for h in range(n_heads):
            g = jnp.zeros((SUBLANES, width), F32)
            for b in range(N_BUCKETS):
                g = jnp.where(bucket == b, rb_ref[b, h] * LOG2E, g)
            g = jnp.where(dist >= 0, g, NEG)
            skewed = pltpu.roll(jnp.broadcast_to(g[0:1, :], (blk, width)), 0, 1,
                                stride=1, stride_axis=0)
            o_ref[h, :, (1 - t) * blk:(2 - t) * blk] = skewed[:, 0:blk]


def _bias_tables(rel_bias):
    shape = (N_HEADS, MOBA_BLOCK, 2 * MOBA_BLOCK)
    return pl.pallas_call(
        _table_kernel,
        grid=(1,),
        in_specs=[pl.BlockSpec(memory_space=pltpu.SMEM)],
        out_specs=pl.BlockSpec(shape, lambda i: (0, 0, 0)),
        out_shape=jax.ShapeDtypeStruct(shape, F32),
        compiler_params=pltpu.CompilerParams(vmem_limit_bytes=VMEM_LIMIT),
    )(rel_bias)


def _attn_kernel(rb_ref, q_ref, k_ref, v_ref, tab_ref, o_ref, qaug_ref, kaug_ref, vaug_ref, kmean_ref,
                 *, nb):
    h = pl.program_id(1)
    blk = MOBA_BLOCK
    hd = HEAD_DIM
    seq = nb * blk
    nt = (((1,), (1,)), ((), ()))

    lane = lax.broadcasted_iota(jnp.int32, (blk, LANES), 1)
    for n in range(nb):
        rows = slice(n * blk, (n + 1) * blk)
        kb = k_ref[0, rows, :]
        kaug_ref[rows, 0:hd] = kb
        kaug_ref[rows, hd:hd + LANES] = jnp.where((lane == n) | (lane == nb + n), 1.0, 0.0).astype(BF16)
        kmean_ref[n:n + 1, :] = jnp.mean(kb.astype(F32), axis=0, keepdims=True)
    vaug_ref[:, 0:hd] = v_ref[0]
    vaug_ref[:, hd:2 * hd] = jnp.ones((seq, hd), BF16)

    def select(g):
        q_lo = g * SEL_GROUP * blk
        w = SEL_GROUP * blk
        last_blk = (g + 1) * SEL_GROUP - 1
        nk = min(nb, -(-(last_blk + 1) // SUBLANES) * SUBLANES)

        km3 = jnp.concatenate(_split_bf16(kmean_ref[0:nk, :], 3), axis=0).astype(BF16)
        sc3 = lax.dot_general(km3, q_ref[0, q_lo:q_lo + w, :], nt, preferred_element_type=F32)
        sc = (sc3[0:nk] + sc3[nk:2 * nk]) + sc3[2 * nk:3 * nk]

        n_iota = lax.broadcasted_iota(jnp.int32, (nk, w), 0)
        q_blk = g * SEL_GROUP + lax.shift_right_logical(
            lax.broadcasted_iota(jnp.int32, (nk, w), 1), int(math.log2(blk)))
        rank = jnp.zeros((nk, w), jnp.int32)
        for m in range(last_blk):
            row = sc[m:m + 1, :]
            beats = (row > sc) | ((row == sc) & (m < n_iota))
            rank = rank + jnp.where(beats & (m < q_blk), 1, 0)
        past = n_iota < q_blk
        chosen = past & (rank < MOBA_TOPK)
        far = chosen & (n_iota < q_blk - 1)

        b_far = jnp.full((nk, w), rb_ref[N_BUCKETS - 1, h] * LOG2E, F32)
        b_hi = b_far.astype(BF16).astype(F32)
        add_hi = jnp.where(past, jnp.where(chosen, jnp.where(far, b_hi, 0.0), NEG), 0.0)
        add_lo = jnp.where(far, b_far - b_hi, 0.0)
        pieces = [add_hi, jnp.zeros((nb - nk, w), F32), add_lo, jnp.zeros((LANES - nb - nk, w), F32)]
        add_t = jnp.concatenate([p for p in pieces if p.shape[0]], axis=0)
        qaug_ref[q_lo:q_lo + w, 0:hd] = q_ref[0, q_lo:q_lo + w, :]
        qaug_ref[q_lo:q_lo + w, hd:hd + LANES] = add_t.T.astype(BF16)

    def logits(i):
        n_keys = (i + 1) * blk
        s = lax.dot_general(qaug_ref[i * blk:(i + 1) * blk, :], kaug_ref[0:n_keys, :], nt,
                            preferred_element_type=F32)
        if i == 0:
            return s + tab_ref[0, :, blk:2 * blk]
        if i == 1:
            return s + tab_ref[0]
        return jnp.concatenate([s[:, :n_keys - 2 * blk], s[:, n_keys - 2 * blk:] + tab_ref[0]], axis=1)

    for g in range(nb // SEL_GROUP):
        select(g)
    s = logits(0)
    for i in range(nb):
        s_next = logits(i + 1) if i + 1 < nb else None
        mx = jnp.max(s, axis=-1, keepdims=True)
        p = jnp.exp2(s.astype(BF16) - mx.astype(BF16))
        pv = jnp.dot(p, vaug_ref[0:(i + 1) * blk, :], preferred_element_type=F32)
        o_ref[0, i * blk:(i + 1) * blk, :] = (pv[:, 0:hd] / pv[:, hd:2 * hd]).astype(o_ref.dtype)
        s = s_next


def _moba_attention(proj, tables, rel_bias, d_model):
    bsz, seq, _ = proj.shape
    nb = seq // MOBA_BLOCK
    nh = d_model // HEAD_DIM
    assert MAX_DISTANCE <= MOBA_BLOCK + 1 and 2 * nb <= LANES and MOBA_BLOCK & (MOBA_BLOCK - 1) == 0
    assert nb % SEL_GROUP == 0 and nb % SUBLANES == 0
    kern = functools.partial(_attn_kernel, nb=nb)

    def head_cols(tile):
        return pl.BlockSpec((1, seq, HEAD_DIM), lambda b, h: (b, 0, tile * nh + h))

    return pl.pallas_call(
        kern,
        grid=(bsz, nh),
        in_specs=[pl.BlockSpec(memory_space=pltpu.SMEM),
                  head_cols(P_Q), head_cols(P_K), head_cols(P_V),
                  pl.BlockSpec((1, MOBA_BLOCK, 2 * MOBA_BLOCK), lambda b, h: (h, 0, 0))],
        out_specs=head_cols(0),
        out_shape=jax.ShapeDtypeStruct((bsz, seq, d_model), BF16),
        scratch_shapes=[pltpu.VMEM((seq, HEAD_DIM + LANES), BF16),
                        pltpu.VMEM((seq, HEAD_DIM + LANES), BF16),
                        pltpu.VMEM((seq, 2 * HEAD_DIM), BF16),
                        pltpu.VMEM((nb, HEAD_DIM), F32)],
        compiler_params=pltpu.CompilerParams(
            dimension_semantics=("arbitrary", "arbitrary"),
            vmem_limit_bytes=VMEM_LIMIT),
    )(rel_bias, proj, proj, proj, tables)


def _out_kernel(attn_ref, ga_ref, zc_ref, ma_ref, mc_ref, x_ref, mod_ref, woa_ref, woc_ref, wout_ref,
                fg_ref, o_ref, *, row_chunk):
    def merged_branches(r):
        rows = slice(r * row_chunk, (r + 1) * row_chunk)
        y_conv = jnp.dot(zc_ref[0, rows, :], woc_ref[...], preferred_element_type=F32)
        z_attn = attn_ref[0, rows, :].astype(F32) * ga_ref[0, rows, :].astype(F32)
        y_attn = jnp.dot(z_attn.astype(BF16), woa_ref[...], preferred_element_type=F32)
        merged = (ma_ref[0, rows, :].astype(F32) * y_attn
                  + mc_ref[0, rows, :].astype(F32) * y_conv)
        return merged.astype(BF16)

    def residual_norm(r, branch):
        rows = slice(r * row_chunk, (r + 1) * row_chunk)
        res = x_ref[0, rows, :] + mod_ref[0, 2:3, :] * branch
        inv = lax.rsqrt(jnp.mean(res * res, axis=-1, keepdims=True) + EPS)
        o_ref[0, rows, :] = res * inv * fg_ref[...]

    n_chunks = x_ref.shape[1] // row_chunk
    merged = merged_branches(0)
    for r in range(n_chunks):
        branch = jnp.dot(merged, wout_ref[...], preferred_element_type=F32)
        merged = merged_branches(r + 1) if r + 1 < n_chunks else None
        residual_norm(r, branch)


def _output_stage(attn, proj, x, mod3, woa, woc, wout, final_g, ts=1024, row_chunk=256):
    bsz, seq, d = x.shape
    assert ts % row_chunk == 0

    def col(k):
        return pl.BlockSpec((1, ts, d), lambda b, s: (b, s, k))

    def whole(shape):
        return pl.BlockSpec(shape, lambda b, s: (0,) * len(shape), pipeline_mode=pl.Buffered(1))

    return pl.pallas_call(
        functools.partial(_out_kernel, row_chunk=row_chunk),
        grid=(bsz, seq // ts),
        in_specs=[col(0), col(P_GA), col(P_ZC), col(P_MA), col(P_MC),
                  col(0), pl.BlockSpec((1, 3, d), lambda b, s: (b, 0, 0)),
                  whole((d, d)), whole((d, d)), whole((d, d)), whole((1, d))],
        out_specs=col(0),
        out_shape=jax.ShapeDtypeStruct((bsz, seq, d), F32),
        compiler_params=pltpu.CompilerParams(
            dimension_semantics=("arbitrary", "arbitrary"),
            vmem_limit_bytes=VMEM_LIMIT),
    )(attn, proj, proj, proj, proj, x, mod3, woa, woc, wout, final_g.reshape(1, d))


def kernel(x, c, norm_g, w_ada, b_ada, w_in, conv_w, w_o_attn, w_o_conv, w_out, rel_bias, final_g):
    bsz, seq, d = x.shape
    depth = norm_g.shape[0]
    assert depth == 1, "the fused output stage applies the final RMSNorm after the single layer"
    assert d == N_HEADS * HEAD_DIM and w_in.shape[2] == N_PROJ * d and seq % MOBA_BLOCK == 0
    tables = _bias_tables(rel_bias)
    mod3 = _modulation(c, w_ada[0], b_ada[0]).reshape(bsz, 3, d)
    proj, woa, woc, wout = _in_projection(x, mod3, norm_g[0], conv_w[0], w_in[0],
                                          (w_o_attn[0], w_o_conv[0], w_out[0]))
    attn = _moba_attention(proj, tables, rel_bias, d)
    return _output_stage(attn, proj, x, mod3, woa, woc, wout, final_g)
```

```python
import functools
import math

import jax
import jax.numpy as jnp
from jax import lax
from jax.experimental import pallas as pl
from jax.experimental.pallas import tpu as pltpu

N_HEADS = 8
HEAD_DIM = 128
CONV_K = 3
MOBA_BLOCK = 256
MOBA_TOPK = 3
N_BUCKETS = 32
MAX_DISTANCE = 128
EPS = 1e-6
N_PROJ = 10

LANES = 128
SUBLANES = 8
BF16_SUBLANES = 16
SEL_GROUP = 4
NEG = -1e30
LOG2E = math.log2(math.e)
VMEM_LIMIT = 56 * 1024 * 1024

F32 = jnp.float32
BF16 = jnp.bfloat16


def _sigmoid(v):
    return 0.5 * jnp.tanh(0.5 * v) + 0.5


def _silu(v):
    return v * _sigmoid(v)


def _split_bf16(v, terms):
    parts = []
    for _ in range(terms - 1):
        hi = v.astype(BF16).astype(F32)
        parts.append(hi)
        v = v - hi
    return parts + [v]


def _mod_kernel(c_ref, w_ref, b_ref, o_ref):
    bsz = c_ref.shape[0]
    a3 = jnp.concatenate(_split_bf16(_silu(c_ref[...]), 3), axis=0).astype(BF16)
    w_hi, w_lo = _split_bf16(w_ref[...], 2)
    r = (jnp.dot(a3, w_hi.astype(BF16), preferred_element_type=F32)
         + jnp.dot(a3, w_lo.astype(BF16), preferred_element_type=F32))
    o_ref[...] = (r[0:bsz] + r[bsz:2 * bsz]) + r[2 * bsz:3 * bsz] + b_ref[...]


def _modulation(c, w_ada, b_ada):
    bsz, d = c.shape
    n = w_ada.shape[1]
    tn = d
    return pl.pallas_call(
        _mod_kernel,
        grid=(n // tn,),
        in_specs=[pl.BlockSpec((bsz, d), lambda j: (0, 0)),
                  pl.BlockSpec((d, tn), lambda j: (0, j)),
                  pl.BlockSpec((1, tn), lambda j: (0, j))],
        out_specs=pl.BlockSpec((bsz, tn), lambda j: (0, j)),
        out_shape=jax.ShapeDtypeStruct((bsz, n), F32),
        compiler_params=pltpu.CompilerParams(vmem_limit_bytes=VMEM_LIMIT),
    )(c, w_ada, b_ada.reshape(1, n))


W_Q, W_K, W_V, W_GA, W_CB, W_CC, W_CX, W_GC, W_MA, W_MC = range(N_PROJ)
N_FUSED = 7
P_Q, P_K, P_V, P_GA, P_ZC, P_MA, P_MC = range(N_FUSED)
CONV_HALO = 8


def _inproj_kernel(x0_ref, xn_ref, mod0_ref, modn_ref, g_ref, cw_ref, w_hbm, *rest, q_scale,
                   row_chunk, n_late):
    late_in, o_ref, late_out = rest[:n_late], rest[n_late], rest[n_late + 1:2 * n_late + 1]
    h_ref, u_ref, t_ref, w_ref, stage_ref, sem = rest[2 * n_late + 1:]
    for src, dst in zip(late_in, late_out):
        dst[...] = src[...].astype(dst.dtype)
    s = pl.program_id(1)
    step = pl.program_id(0) * pl.num_programs(1) + s
    slot = lax.rem(step, 2)
    ts, d = xn_ref.shape[1], xn_ref.shape[2]
    n_chunks = ts // row_chunk
    g = g_ref[...]

    def normalise(x_ref, mod_ref, dst_slot, r):
        rows = slice(r * row_chunk, (r + 1) * row_chunk)
        xr = x_ref[0, rows, :]
        inv = lax.rsqrt(jnp.mean(xr * xr, axis=-1, keepdims=True) + EPS)
        hr = (xr * inv * g) * (1.0 + mod_ref[0, 1:2, :]) + mod_ref[0, 0:1, :]
        h_ref[dst_slot, rows, :] = hr.astype(BF16)

    depth, _, width = stage_ref.shape

    def weight_chunk_copy(c):
        return pltpu.make_async_copy(w_hbm.at[:, c * width:(c + 1) * width],
                                     stage_ref.at[c % depth], sem.at[c % depth])

    @pl.when(step == 0)
    def _():
        n_w = w_ref.shape[1] // width
        for c in range(min(depth, n_w)):
            weight_chunk_copy(c).start()
        for c in range(n_w):
            weight_chunk_copy(c).wait()
            w_ref[:, c * width:(c + 1) * width] = stage_ref[c % depth].astype(BF16)
            if c + depth < n_w:
                weight_chunk_copy(c + depth).start()
        for r in range(n_chunks):
            normalise(x0_ref, mod0_ref, 0, r)

    def proj(j):
        return jnp.dot(h_ref[slot], w_ref[:, j * d:(j + 1) * d], preferred_element_type=F32)

    def put(k, val):
        o_ref[0, :, k * d:(k + 1) * d] = val.astype(o_ref.dtype)

    @pl.when(s == 0)
    def _():
        u_ref[0:CONV_HALO, :] = jnp.zeros((CONV_HALO, d), F32)

    def keep_cc(acc):
        t_ref[...] = acc

    def conv(acc):
        u_ref[CONV_HALO:CONV_HALO + ts, :] = t_ref[...] * acc
        t_ref[...] = (cw_ref[0:1, :] * u_ref[CONV_HALO - 2:CONV_HALO - 2 + ts, :]
                      + cw_ref[1:2, :] * u_ref[CONV_HALO - 1:CONV_HALO - 1 + ts, :]
                      + cw_ref[2:3, :] * u_ref[CONV_HALO:CONV_HALO + ts, :])
        u_ref[0:CONV_HALO, :] = u_ref[ts:ts + CONV_HALO, :]

    def gate_cb(acc):
        t_ref[...] = t_ref[...] * acc

    stages = [
        (W_GA, lambda acc: put(P_GA, _silu(acc))),
        (W_MA, lambda acc: put(P_MA, _sigmoid(acc))),
        (W_MC, lambda acc: put(P_MC, _sigmoid(acc))),
        (W_CC, keep_cc),
        (W_CX, conv),
        (W_CB, gate_cb),
        (W_GC, lambda acc: put(P_ZC, t_ref[...] * _silu(acc))),
        (W_Q, lambda acc: put(P_Q, acc * q_scale)),
        (W_K, lambda acc: put(P_K, acc)),
        (W_V, lambda acc: put(P_V, acc)),
    ]
    norm_after = list(range(len(stages) - 1 - n_chunks, len(stages) - 1))
    acc = proj(stages[0][0])
    for n, (_, epilogue) in enumerate(stages):
        nxt = proj(stages[n + 1][0]) if n + 1 < len(stages) else None
        epilogue(acc)
        if n in norm_after:
            normalise(xn_ref, modn_ref, 1 - slot, norm_after.index(n))
        acc = nxt


def _in_projection(x, mod3, norm_g, conv_w, w_in, late_weights, ts=512, row_chunk=128,
                   w_chunk=256, w_depth=4):
    bsz, seq, d = x.shape
    n = w_in.shape[1]
    assert n % w_chunk == 0 and w_chunk % LANES == 0
    n_out = N_FUSED * d
    n_s = seq // ts
    n_steps = bsz * n_s
    assert ts // row_chunk <= N_PROJ
    slab = d // n_steps
    assert slab * n_steps == d and slab % BF16_SUBLANES == 0
    assert all(w.shape == (d, d) for w in late_weights)
    kern = functools.partial(_inproj_kernel, q_scale=LOG2E * HEAD_DIM ** -0.5, row_chunk=row_chunk,
                             n_late=len(late_weights))

    def next_tile(b, s):
        flat = jnp.minimum(b * n_s + s + 1, n_steps - 1)
        return flat // n_s, flat % n_s

    slab_spec = pl.BlockSpec((slab, d), lambda b, s: (b * n_s + s, 0))
    return pl.pallas_call(
        kern,
        grid=(bsz, n_s),
        in_specs=[pl.BlockSpec((1, ts, d), lambda b, s: (0, 0, 0), pipeline_mode=pl.Buffered(1)),
                  pl.BlockSpec((1, ts, d), lambda b, s: (*next_tile(b, s), 0)),
                  pl.BlockSpec((1, 3, d), lambda b, s: (0, 0, 0)),
                  pl.BlockSpec((1, 3, d), lambda b, s: (next_tile(b, s)[0], 0, 0)),
                  pl.BlockSpec((1, d), lambda b, s: (0, 0)),
                  pl.BlockSpec((CONV_K, d), lambda b, s: (0, 0)),
                  pl.BlockSpec(memory_space=pl.ANY)]
                 + [slab_spec] * len(late_weights),
        out_specs=[pl.BlockSpec((1, ts, n_out), lambda b, s: (b, s, 0))]
                  + [slab_spec] * len(late_weights),
        out_shape=[jax.ShapeDtypeStruct((bsz, seq, n_out), BF16)]
                  + [jax.ShapeDtypeStruct((d, d), BF16)] * len(late_weights),
        scratch_shapes=[pltpu.VMEM((2, ts, d), BF16), pltpu.VMEM((CONV_HALO + ts, d), F32),
                        pltpu.VMEM((ts, d), F32), pltpu.VMEM((d, n), BF16),
                        pltpu.VMEM((w_depth, d, w_chunk), F32),
                        pltpu.SemaphoreType.DMA((w_depth,))],
        compiler_params=pltpu.CompilerParams(
            dimension_semantics=("arbitrary", "arbitrary"),
            vmem_limit_bytes=VMEM_LIMIT),
    )(x, x, mod3, mod3, norm_g.reshape(1, d), conv_w, w_in, *late_weights)


def _t5_bucket(dist):
    n = jnp.maximum(dist, 0)
    max_exact = N_BUCKETS // 2
    nf = jnp.maximum(n, 1).astype(F32)
    large = max_exact + (jnp.log(nf / max_exact) / math.log(MAX_DISTANCE / max_exact)
                         * (N_BUCKETS - max_exact)).astype(jnp.int32)
    large = jnp.minimum(large, N_BUCKETS - 1)
    return jnp.where(n < max_exact, n, large)


def _table_kernel(rb_ref, o_ref):
    n_heads, blk, width = o_ref.shape
    u = lax.broadcasted_iota(jnp.int32, (SUBLANES, width), 1)
    for t in range(2):
        dist = jnp.where(u < blk, t * blk - u, t * blk + width - u)
        bucket = _t5_bucket(dist)
        for h in range(n_heads):
            g = jnp.zeros((SUBLANES, width), F32)
            for b in range(N_BUCKETS):
                g = jnp.where(bucket == b, rb_ref[b, h] * LOG2E, g)
            g = jnp.where(dist >= 0, g, NEG)
            skewed = pltpu.roll(jnp.broadcast_to(g[0:1, :], (blk, width)), 0, 1,
                                stride=1, stride_axis=0)
            o_ref[h, :, (1 - t) * blk:(2 - t) * blk] = skewed[:, 0:blk]


def _bias_tables(rel_bias):
    shape = (N_HEADS, MOBA_BLOCK, 2 * MOBA_BLOCK)
    return pl.pallas_call(
        _table_kernel,
        grid=(1,),
        in_specs=[pl.BlockSpec(memory_space=pltpu.SMEM)],
        out_specs=pl.BlockSpec(shape, lambda i: (0, 0, 0)),
        out_shape=jax.ShapeDtypeStruct(shape, F32),
        compiler_params=pltpu.CompilerParams(vmem_limit_bytes=VMEM_LIMIT),
    )(rel_bias)


def _attn_kernel(rb_ref, q_ref, k_ref, v_ref, tab_ref, o_ref, qaug_ref, kaug_ref, vaug_ref, kmean_ref,
                 *, nb):
    h = pl.program_id(1)
    blk = MOBA_BLOCK
    hd = HEAD_DIM
    seq = nb * blk
    nt = (((1,), (1,)), ((), ()))

    lane = lax.broadcasted_iota(jnp.int32, (blk, LANES), 1)
    for n in range(nb):
        rows = slice(n * blk, (n + 1) * blk)
        kb = k_ref[0, rows, :]
        kaug_ref[rows, 0:hd] = kb
        kaug_ref[rows, hd:hd + LANES] = jnp.where((lane == n) | (lane == nb + n), 1.0, 0.0).astype(BF16)
        kmean_ref[n:n + 1, :] = jnp.mean(kb.astype(F32), axis=0, keepdims=True)
    vaug_ref[:, 0:hd] = v_ref[0]
    vaug_ref[:, hd:2 * hd] = jnp.ones((seq, hd), BF16)

    def select(g):
        q_lo = g * SEL_GROUP * blk
        w = SEL_GROUP * blk
        last_blk = (g + 1) * SEL_GROUP - 1
        nk = min(nb, -(-(last_blk + 1) // SUBLANES) * SUBLANES)

        km3 = jnp.concatenate(_split_bf16(kmean_ref[0:nk, :], 3), axis=0).astype(BF16)
        sc3 = lax.dot_general(km3, q_ref[0, q_lo:q_lo + w, :], nt, preferred_element_type=F32)
        sc = (sc3[0:nk] + sc3[nk:2 * nk]) + sc3[2 * nk:3 * nk]

        n_iota = lax.broadcasted_iota(jnp.int32, (nk, w), 0)
        q_blk = g * SEL_GROUP + lax.shift_right_logical(
            lax.broadcasted_iota(jnp.int32, (nk, w), 1), int(math.log2(blk)))
        rank = jnp.zeros((nk, w), jnp.int32)
        for m in range(last_blk):
            row = sc[m:m + 1, :]
            beats = (row > sc) | ((row == sc) & (m < n_iota))
            rank = rank + jnp.where(beats & (m < q_blk), 1, 0)
        past = n_iota < q_blk
        chosen = past & (rank < MOBA_TOPK)
        far = chosen & (n_iota < q_blk - 1)

        b_far = jnp.full((nk, w), rb_ref[N_BUCKETS - 1, h] * LOG2E, F32)
        b_hi = b_far.astype(BF16).astype(F32)
        add_hi = jnp.where(past, jnp.where(chosen, jnp.where(far, b_hi, 0.0), NEG), 0.0)
        add_lo = jnp.where(far, b_far - b_hi, 0.0)
        pieces = [add_hi, jnp.zeros((nb - nk, w), F32), add_lo, jnp.zeros((LANES - nb - nk, w), F32)]
        add_t = jnp.concatenate([p for p in pieces if p.shape[0]], axis=0)
        qaug_ref[q_lo:q_lo + w, 0:hd] = q_ref[0, q_lo:q_lo + w, :]
        qaug_ref[q_lo:q_lo + w, hd:hd + LANES] = add_t.T.astype(BF16)

    def logits(i):
        n_keys = (i + 1) * blk
        s = lax.dot_general(qaug_ref[i * blk:(i + 1) * blk, :], kaug_ref[0:n_keys, :], nt,
                            preferred_element_type=F32)
        if i == 0:
            return s + tab_ref[0, :, blk:2 * blk]
        if i == 1:
            return s + tab_ref[0]
        return jnp.concatenate([s[:, :n_keys - 2 * blk], s[:, n_keys - 2 * blk:] + tab_ref[0]], axis=1)

    for g in range(nb // SEL_GROUP):
        select(g)
    s = logits(0)
    for i in range(nb):
        s_next = logits(i + 1) if i + 1 < nb else None
        s16 = s.astype(BF16)
        p = jnp.exp2(s16 - jnp.max(s16, axis=-1, keepdims=True))
        pv = jnp.dot(p, vaug_ref[0:(i + 1) * blk, :], preferred_element_type=F32)
        o_ref[0, i * blk:(i + 1) * blk, :] = (pv[:, 0:hd] / pv[:, hd:2 * hd]).astype(o_ref.dtype)
        s = s_next


def _moba_attention(proj, tables, rel_bias, d_model):
    bsz, seq, _ = proj.shape
    nb = seq // MOBA_BLOCK
    nh = d_model // HEAD_DIM
    assert MAX_DISTANCE <= MOBA_BLOCK + 1 and 2 * nb <= LANES and MOBA_BLOCK & (MOBA_BLOCK - 1) == 0
    assert nb % SEL_GROUP == 0 and nb % SUBLANES == 0
    kern = functools.partial(_attn_kernel, nb=nb)

    def head_cols(tile):
        return pl.BlockSpec((1, seq, HEAD_DIM), lambda b, h: (b, 0, tile * nh + h))

    return pl.pallas_call(
        kern,
        grid=(bsz, nh),
        in_specs=[pl.BlockSpec(memory_space=pltpu.SMEM),
                  head_cols(P_Q), head_cols(P_K), head_cols(P_V),
                  pl.BlockSpec((1, MOBA_BLOCK, 2 * MOBA_BLOCK), lambda b, h: (h, 0, 0))],
        out_specs=head_cols(0),
        out_shape=jax.ShapeDtypeStruct((bsz, seq, d_model), BF16),
        scratch_shapes=[pltpu.VMEM((seq, HEAD_DIM + LANES), BF16),
                        pltpu.VMEM((seq, HEAD_DIM + LANES), BF16),
                        pltpu.VMEM((seq, 2 * HEAD_DIM), BF16),
                        pltpu.VMEM((nb, HEAD_DIM), F32)],
        compiler_params=pltpu.CompilerParams(
            dimension_semantics=("arbitrary", "arbitrary"),
            vmem_limit_bytes=VMEM_LIMIT),
    )(rel_bias, proj, proj, proj, tables)


def _out_kernel(attn_ref, ga_ref, zc_ref, ma_ref, mc_ref, x_ref, mod_ref, woa_ref, woc_ref, wout_ref,
                fg_ref, o_ref, *, row_chunk):
    def merged_branches(r):
        rows = slice(r * row_chunk, (r + 1) * row_chunk)
        y_conv = jnp.dot(zc_ref[0, rows, :], woc_ref[...], preferred_element_type=F32)
        z_attn = attn_ref[0, rows, :].astype(F32) * ga_ref[0, rows, :].astype(F32)
        y_attn = jnp.dot(z_attn.astype(BF16), woa_ref[...], preferred_element_type=F32)
        merged = (ma_ref[0, rows, :].astype(F32) * y_attn
                  + mc_ref[0, rows, :].astype(F32) * y_conv)
        return merged.astype(BF16)

    def residual_norm(r, branch):
        rows = slice(r * row_chunk, (r + 1) * row_chunk)
        res = x_ref[0, rows, :] + mod_ref[0, 2:3, :] * branch
        inv = lax.rsqrt(jnp.mean(res * res, axis=-1, keepdims=True) + EPS)
        o_ref[0, rows, :] = res * inv * fg_ref[...]

    n_chunks = x_ref.shape[1] // row_chunk
    merged = merged_branches(0)
    for r in range(n_chunks):
        branch = jnp.dot(merged, wout_ref[...], preferred_element_type=F32)
        merged = merged_branches(r + 1) if r + 1 < n_chunks else None
        residual_norm(r, branch)


def _output_stage(attn, proj, x, mod3, woa, woc, wout, final_g, ts=1024, row_chunk=256):
    bsz, seq, d = x.shape
    assert ts % row_chunk == 0

    def col(k):
        return pl.BlockSpec((1, ts, d), lambda b, s: (b, s, k))

    def whole(shape):
        return pl.BlockSpec(shape, lambda b, s: (0,) * len(shape), pipeline_mode=pl.Buffered(1))

    return pl.pallas_call(
        functools.partial(_out_kernel, row_chunk=row_chunk),
        grid=(bsz, seq // ts),
        in_specs=[col(0), col(P_GA), col(P_ZC), col(P_MA), col(P_MC),
                  col(0), pl.BlockSpec((1, 3, d), lambda b, s: (b, 0, 0)),
                  whole((d, d)), whole((d, d)), whole((d, d)), whole((1, d))],
        out_specs=col(0),
        out_shape=jax.ShapeDtypeStruct((bsz, seq, d), F32),
        compiler_params=pltpu.CompilerParams(
            dimension_semantics=("arbitrary", "arbitrary"),
            vmem_limit_bytes=VMEM_LIMIT),
    )(attn, proj, proj, proj, proj, x, mod3, woa, woc, wout, final_g.reshape(1, d))


def kernel(x, c, norm_g, w_ada, b_ada, w_in, conv_w, w_o_attn, w_o_conv, w_out, rel_bias, final_g):
    bsz, seq, d = x.shape
    depth = norm_g.shape[0]
    assert depth == 1, "the fused output stage applies the final RMSNorm after the single layer"
    assert d == N_HEADS * HEAD_DIM and w_in.shape[2] == N_PROJ * d and seq % MOBA_BLOCK == 0
    tables = _bias_tables(rel_bias)
    mod3 = _modulation(c, w_ada[0], b_ada[0]).reshape(bsz, 3, d)
    proj, woa, woc, wout = _in_projection(x, mod3, norm_g[0], conv_w[0], w_in[0],
                                          (w_o_attn[0], w_o_conv[0], w_out[0]))
    attn = _moba_attention(proj, tables, rel_bias, d)
    return _output_stage(attn, proj, x, mod3, woa, woc, wout, final_g)
```

```python
import functools
import math

import jax
import jax.numpy as jnp
from jax import lax
from jax.experimental import pallas as pl
from jax.experimental.pallas import tpu as pltpu

N_HEADS = 8
HEAD_DIM = 128
CONV_K = 3
MOBA_BLOCK = 256
MOBA_TOPK = 3
N_BUCKETS = 32
MAX_DISTANCE = 128
EPS = 1e-6
N_PROJ = 10

LANES = 128
SUBLANES = 8
BF16_SUBLANES = 16
SEL_GROUP = 4
NEG = -1e30
LOG2E = math.log2(math.e)
VMEM_LIMIT = 56 * 1024 * 1024

F32 = jnp.float32
BF16 = jnp.bfloat16


def _sigmoid(v):
    return 0.5 * jnp.tanh(0.5 * v) + 0.5


def _silu(v):
    return v * _sigmoid(v)


def _split_bf16(v, terms):
    parts = []
    for _ in range(terms - 1):
        hi = v.astype(BF16).astype(F32)
        parts.append(hi)
        v = v - hi
    return parts + [v]


def _mod_kernel(c_ref, w_ref, b_ref, o_ref):
    bsz = c_ref.shape[0]
    a3 = jnp.concatenate(_split_bf16(_silu(c_ref[...]), 3), axis=0).astype(BF16)
    w_hi, w_lo = _split_bf16(w_ref[...], 2)
    r = (jnp.dot(a3, w_hi.astype(BF16), preferred_element_type=F32)
         + jnp.dot(a3, w_lo.astype(BF16), preferred_element_type=F32))
    o_ref[...] = (r[0:bsz] + r[bsz:2 * bsz]) + r[2 * bsz:3 * bsz] + b_ref[...]


def _modulation(c, w_ada, b_ada):
    bsz, d = c.shape
    n = w_ada.shape[1]
    tn = d
    return pl.pallas_call(
        _mod_kernel,
        grid=(n // tn,),
        in_specs=[pl.BlockSpec((bsz, d), lambda j: (0, 0)),
                  pl.BlockSpec((d, tn), lambda j: (0, j)),
                  pl.BlockSpec((1, tn), lambda j: (0, j))],
        out_specs=pl.BlockSpec((bsz, tn), lambda j: (0, j)),
        out_shape=jax.ShapeDtypeStruct((bsz, n), F32),
        compiler_params=pltpu.CompilerParams(vmem_limit_bytes=VMEM_LIMIT),
    )(c, w_ada, b_ada.reshape(1, n))


W_Q, W_K, W_V, W_GA, W_CB, W_CC, W_CX, W_GC, W_MA, W_MC = range(N_PROJ)
N_FUSED = 7
P_Q, P_K, P_V, P_GA, P_ZC, P_MA, P_MC = range(N_FUSED)
CONV_HALO = 8


def _inproj_kernel(x0_ref, xn_ref, mod0_ref, modn_ref, g_ref, cw_ref, w_hbm, *rest, q_scale,
                   row_chunk, n_late):
    late_in, o_ref, late_out = rest[:n_late], rest[n_late], rest[n_late + 1:2 * n_late + 1]
    kmean_o_ref = rest[2 * n_late + 1]
    h_ref, u_ref, t_ref, w_ref, stage_ref, sem = rest[2 * n_late + 2:]
    for src, dst in zip(late_in, late_out):
        dst[...] = src[...].astype(dst.dtype)
    s = pl.program_id(1)
    step = pl.program_id(0) * pl.num_programs(1) + s
    slot = lax.rem(step, 2)
    ts, d = xn_ref.shape[1], xn_ref.shape[2]
    n_chunks = ts // row_chunk
    g = g_ref[...]

    def normalise(x_ref, mod_ref, dst_slot, r):
        rows = slice(r * row_chunk, (r + 1) * row_chunk)
        xr = x_ref[0, rows, :]
        inv = lax.rsqrt(jnp.mean(xr * xr, axis=-1, keepdims=True) + EPS)
        hr = (xr * inv * g) * (1.0 + mod_ref[0, 1:2, :]) + mod_ref[0, 0:1, :]
        h_ref[dst_slot, rows, :] = hr.astype(BF16)

    depth, _, width = stage_ref.shape

    def weight_chunk_copy(c):
        return pltpu.make_async_copy(w_hbm.at[:, c * width:(c + 1) * width],
                                     stage_ref.at[c % depth], sem.at[c % depth])

    @pl.when(step == 0)
    def _():
        n_w = w_ref.shape[1] // width
        for c in range(min(depth, n_w)):
            weight_chunk_copy(c).start()
        for c in range(n_w):
            weight_chunk_copy(c).wait()
            w_ref[:, c * width:(c + 1) * width] = stage_ref[c % depth].astype(BF16)
            if c + depth < n_w:
                weight_chunk_copy(c + depth).start()
        for r in range(n_chunks):
            normalise(x0_ref, mod0_ref, 0, r)

    def proj(j):
        return jnp.dot(h_ref[slot], w_ref[:, j * d:(j + 1) * d], preferred_element_type=F32)

    def put(k, val):
        o_ref[0, :, k * d:(k + 1) * d] = val.astype(o_ref.dtype)

    @pl.when(s == 0)
    def _():
        u_ref[0:CONV_HALO, :] = jnp.zeros((CONV_HALO, d), F32)

    def keep_cc(acc):
        t_ref[...] = acc

    def conv(acc):
        u_ref[CONV_HALO:CONV_HALO + ts, :] = t_ref[...] * acc
        t_ref[...] = (cw_ref[0:1, :] * u_ref[CONV_HALO - 2:CONV_HALO - 2 + ts, :]
                      + cw_ref[1:2, :] * u_ref[CONV_HALO - 1:CONV_HALO - 1 + ts, :]
                      + cw_ref[2:3, :] * u_ref[CONV_HALO:CONV_HALO + ts, :])
        u_ref[0:CONV_HALO, :] = u_ref[ts:ts + CONV_HALO, :]

    def gate_cb(acc):
        t_ref[...] = t_ref[...] * acc

    def put_keys(acc):
        put(P_K, acc)
        for j in range(ts // MOBA_BLOCK):
            kmean_o_ref[0, 0, j:j + 1, :] = jnp.mean(acc[j * MOBA_BLOCK:(j + 1) * MOBA_BLOCK], axis=0,
                                                     keepdims=True)

    stages = [
        (W_GA, lambda acc: put(P_GA, _silu(acc))),
        (W_MA, lambda acc: put(P_MA, _sigmoid(acc))),
        (W_MC, lambda acc: put(P_MC, _sigmoid(acc))),
        (W_CC, keep_cc),
        (W_CX, conv),
        (W_CB, gate_cb),
        (W_GC, lambda acc: put(P_ZC, t_ref[...] * _silu(acc))),
        (W_Q, lambda acc: put(P_Q, acc * q_scale)),
        (W_K, put_keys),
        (W_V, lambda acc: put(P_V, acc)),
    ]
    norm_after = list(range(len(stages) - 1 - n_chunks, len(stages) - 1))
    acc = proj(stages[0][0])
    for n, (_, epilogue) in enumerate(stages):
        nxt = proj(stages[n + 1][0]) if n + 1 < len(stages) else None
        epilogue(acc)
        if n in norm_after:
            normalise(xn_ref, modn_ref, 1 - slot, norm_after.index(n))
        acc = nxt


def _in_projection(x, mod3, norm_g, conv_w, w_in, late_weights, ts=512, row_chunk=128,
                   w_chunk=256, w_depth=4):
    bsz, seq, d = x.shape
    n = w_in.shape[1]
    assert n % w_chunk == 0 and w_chunk % LANES == 0
    n_out = N_FUSED * d
    n_s = seq // ts
    n_steps = bsz * n_s
    assert ts // row_chunk <= N_PROJ
    slab = d // n_steps
    assert slab * n_steps == d and slab % BF16_SUBLANES == 0
    assert all(w.shape == (d, d) for w in late_weights)
    kern = functools.partial(_inproj_kernel, q_scale=LOG2E * HEAD_DIM ** -0.5, row_chunk=row_chunk,
                             n_late=len(late_weights))

    def next_tile(b, s):
        flat = jnp.minimum(b * n_s + s + 1, n_steps - 1)
        return flat // n_s, flat % n_s

    slab_spec = pl.BlockSpec((slab, d), lambda b, s: (b * n_s + s, 0))
    return pl.pallas_call(
        kern,
        grid=(bsz, n_s),
        in_specs=[pl.BlockSpec((1, ts, d), lambda b, s: (0, 0, 0), pipeline_mode=pl.Buffered(1)),
                  pl.BlockSpec((1, ts, d), lambda b, s: (*next_tile(b, s), 0)),
                  pl.BlockSpec((1, 3, d), lambda b, s: (0, 0, 0)),
                  pl.BlockSpec((1, 3, d), lambda b, s: (next_tile(b, s)[0], 0, 0)),
                  pl.BlockSpec((1, d), lambda b, s: (0, 0)),
                  pl.BlockSpec((CONV_K, d), lambda b, s: (0, 0)),
                  pl.BlockSpec(memory_space=pl.ANY)]
                 + [slab_spec] * len(late_weights),
        out_specs=[pl.BlockSpec((1, ts, n_out), lambda b, s: (b, s, 0))]
                  + [slab_spec] * len(late_weights)
                  + [pl.BlockSpec((1, 1, ts // MOBA_BLOCK, d), lambda b, s: (b, s, 0, 0))],
        out_shape=[jax.ShapeDtypeStruct((bsz, seq, n_out), BF16)]
                  + [jax.ShapeDtypeStruct((d, d), BF16)] * len(late_weights)
                  + [jax.ShapeDtypeStruct((bsz, n_s, ts // MOBA_BLOCK, d), F32)],
        scratch_shapes=[pltpu.VMEM((2, ts, d), BF16), pltpu.VMEM((CONV_HALO + ts, d), F32),
                        pltpu.VMEM((ts, d), F32), pltpu.VMEM((d, n), BF16),
                        pltpu.VMEM((w_depth, d, w_chunk), F32),
                        pltpu.SemaphoreType.DMA((w_depth,))],
        compiler_params=pltpu.CompilerParams(
            dimension_semantics=("arbitrary", "arbitrary"),
            vmem_limit_bytes=VMEM_LIMIT),
    )(x, x, mod3, mod3, norm_g.reshape(1, d), conv_w, w_in, *late_weights)


def _t5_bucket(dist):
    n = jnp.maximum(dist, 0)
    max_exact = N_BUCKETS // 2
    nf = jnp.maximum(n, 1).astype(F32)
    large = max_exact + (jnp.log(nf / max_exact) / math.log(MAX_DISTANCE / max_exact)
                         * (N_BUCKETS - max_exact)).astype(jnp.int32)
    large = jnp.minimum(large, N_BUCKETS - 1)
    return jnp.where(n < max_exact, n, large)


def _table_kernel(rb_ref, o_ref):
    n_heads, blk, width = o_ref.shape
    u = lax.broadcasted_iota(jnp.int32, (SUBLANES, width), 1)
    for t in range(2):
        dist = jnp.where(u < blk, t * blk - u, t * blk + width - u)
        bucket = _t5_bucket(dist)
        for h in range(n_heads):
            g = jnp.zeros((SUBLANES, width), F32)
            for b in range(N_BUCKETS):
                g = jnp.where(bucket == b, rb_ref[b, h] * LOG2E, g)
            g = jnp.where(dist >= 0, g, NEG)
            skewed = pltpu.roll(jnp.broadcast_to(g[0:1, :], (blk, width)), 0, 1,
                                stride=1, stride_axis=0)
            o_ref[h, :, (1 - t) * blk:(2 - t) * blk] = skewed[:, 0:blk]


def _bias_tables(rel_bias):
    shape = (N_HEADS, MOBA_BLOCK, 2 * MOBA_BLOCK)
    return pl.pallas_call(
        _table_kernel,
        grid=(1,),
        in_specs=[pl.BlockSpec(memory_space=pltpu.SMEM)],
        out_specs=pl.BlockSpec(shape, lambda i: (0, 0, 0)),
        out_shape=jax.ShapeDtypeStruct(shape, F32),
        compiler_params=pltpu.CompilerParams(vmem_limit_bytes=VMEM_LIMIT),
    )(rel_bias)


def _attn_kernel(rb_ref, q_ref, k_ref, v_ref, km_ref, tab_ref, o_ref, qaug_ref, kaug_ref, vaug_ref,
                 kmean_ref, *, nb):
    h = pl.program_id(1)
    blk = MOBA_BLOCK
    hd = HEAD_DIM
    seq = nb * blk
    nt = (((1,), (1,)), ((), ()))

    lane = lax.broadcasted_iota(jnp.int32, (blk, LANES), 1)
    for n in range(nb):
        rows = slice(n * blk, (n + 1) * blk)
        kaug_ref[rows, 0:hd] = k_ref[0, rows, :]
        kaug_ref[rows, hd:hd + LANES] = jnp.where((lane == n) | (lane == nb + n), 1.0, 0.0).astype(BF16)
    per_tile = km_ref.shape[2]
    for t in range(nb // per_tile):
        kmean_ref[t * per_tile:(t + 1) * per_tile, :] = km_ref[0, t]
    vaug_ref[:, 0:hd] = v_ref[0]
    vaug_ref[:, hd:2 * hd] = jnp.ones((seq, hd), BF16)

    def select(g):
        q_lo = g * SEL_GROUP * blk
        w = SEL_GROUP * blk
        last_blk = (g + 1) * SEL_GROUP - 1
        nk = min(nb, -(-(last_blk + 1) // SUBLANES) * SUBLANES)

        km3 = jnp.concatenate(_split_bf16(kmean_ref[0:nk, :], 3), axis=0).astype(BF16)
        sc3 = lax.dot_general(km3, q_ref[0, q_lo:q_lo + w, :], nt, preferred_element_type=F32)
        sc = (sc3[0:nk] + sc3[nk:2 * nk]) + sc3[2 * nk:3 * nk]

        n_iota = lax.broadcasted_iota(jnp.int32, (nk, w), 0)
        q_blk = g * SEL_GROUP + lax.shift_right_logical(
            lax.broadcasted_iota(jnp.int32, (nk, w), 1), int(math.log2(blk)))
        rank = jnp.zeros((nk, w), jnp.int32)
        for m in range(last_blk):
            row = sc[m:m + 1, :]
            beats = (row > sc) | ((row == sc) & (m < n_iota))
            rank = rank + jnp.where(beats & (m < q_blk), 1, 0)
        past = n_iota < q_blk
        chosen = past & (rank < MOBA_TOPK)
        far = chosen & (n_iota < q_blk - 1)

        b_far = jnp.full((nk, w), rb_ref[N_BUCKETS - 1, h] * LOG2E, F32)
        b_hi = b_far.astype(BF16).astype(F32)
        add_hi = jnp.where(past, jnp.where(chosen, jnp.where(far, b_hi, 0.0), NEG), 0.0)
        add_lo = jnp.where(far, b_far - b_hi, 0.0)
        pieces = [add_hi, jnp.zeros((nb - nk, w), F32), add_lo, jnp.zeros((LANES - nb - nk, w), F32)]
        add_t = jnp.concatenate([p for p in pieces if p.shape[0]], axis=0)
        qaug_ref[q_lo:q_lo + w, 0:hd] = q_ref[0, q_lo:q_lo + w, :]
        qaug_ref[q_lo:q_lo + w, hd:hd + LANES] = add_t.T.astype(BF16)

    def logits(i):
        n_keys = (i + 1) * blk
        s = lax.dot_general(qaug_ref[i * blk:(i + 1) * blk, :], kaug_ref[0:n_keys, :], nt,
                            preferred_element_type=F32)
        if i == 0:
            return s + tab_ref[0, :, blk:2 * blk]
        if i == 1:
            return s + tab_ref[0]
        return jnp.concatenate([s[:, :n_keys - 2 * blk], s[:, n_keys - 2 * blk:] + tab_ref[0]], axis=1)

    for g in range(nb // SEL_GROUP):
        select(g)
    s = logits(0)
    for i in range(nb):
        s_next = logits(i + 1) if i + 1 < nb else None
        s16 = s.astype(BF16)
        p = jnp.exp2(s16 - jnp.max(s16, axis=-1, keepdims=True))
        pv = jnp.dot(p, vaug_ref[0:(i + 1) * blk, :], preferred_element_type=F32)
        o_ref[0, i * blk:(i + 1) * blk, :] = (pv[:, 0:hd] / pv[:, hd:2 * hd]).astype(o_ref.dtype)
        s = s_next


def _moba_attention(proj, kmeans, tables, rel_bias, d_model):
    bsz, seq, _ = proj.shape
    nb = seq // MOBA_BLOCK
    nh = d_model // HEAD_DIM
    assert MAX_DISTANCE <= MOBA_BLOCK + 1 and 2 * nb <= LANES and MOBA_BLOCK & (MOBA_BLOCK - 1) == 0
    assert nb % SEL_GROUP == 0 and nb % SUBLANES == 0
    kern = functools.partial(_attn_kernel, nb=nb)

    def head_cols(tile):
        return pl.BlockSpec((1, seq, HEAD_DIM), lambda b, h: (b, 0, tile * nh + h))

    return pl.pallas_call(
        kern,
        grid=(bsz, nh),
        in_specs=[pl.BlockSpec(memory_space=pltpu.SMEM),
                  head_cols(P_Q), head_cols(P_K), head_cols(P_V),
                  pl.BlockSpec((1,) + kmeans.shape[1:3] + (HEAD_DIM,), lambda b, h: (b, 0, 0, h)),
                  pl.BlockSpec((1, MOBA_BLOCK, 2 * MOBA_BLOCK), lambda b, h: (h, 0, 0))],
        out_specs=head_cols(0),
        out_shape=jax.ShapeDtypeStruct((bsz, seq, d_model), BF16),
        scratch_shapes=[pltpu.VMEM((seq, HEAD_DIM + LANES), BF16),
                        pltpu.VMEM((seq, HEAD_DIM + LANES), BF16),
                        pltpu.VMEM((seq, 2 * HEAD_DIM), BF16),
                        pltpu.VMEM((nb, HEAD_DIM), F32)],
        compiler_params=pltpu.CompilerParams(
            dimension_semantics=("arbitrary", "arbitrary"),
            vmem_limit_bytes=VMEM_LIMIT),
    )(rel_bias, proj, proj, proj, kmeans, tables)


def _out_kernel(attn_ref, ga_ref, zc_ref, ma_ref, mc_ref, x_ref, mod_ref, woa_ref, woc_ref, wout_ref,
                fg_ref, o_ref, *, row_chunk):
    def merged_branches(r):
        rows = slice(r * row_chunk, (r + 1) * row_chunk)
        y_conv = jnp.dot(zc_ref[0, rows, :], woc_ref[...], preferred_element_type=F32)
        z_attn = attn_ref[0, rows, :].astype(F32) * ga_ref[0, rows, :].astype(F32)
        y_attn = jnp.dot(z_attn.astype(BF16), woa_ref[...], preferred_element_type=F32)
        merged = (ma_ref[0, rows, :].astype(F32) * y_attn
                  + mc_ref[0, rows, :].astype(F32) * y_conv)
        return merged.astype(BF16)

    def residual_norm(r, branch):
        rows = slice(r * row_chunk, (r + 1) * row_chunk)
        res = x_ref[0, rows, :] + mod_ref[0, 2:3, :] * branch
        inv = lax.rsqrt(jnp.mean(res * res, axis=-1, keepdims=True) + EPS)
        o_ref[0, rows, :] = res * inv * fg_ref[...]

    n_chunks = x_ref.shape[1] // row_chunk
    merged = merged_branches(0)
    for r in range(n_chunks):
        branch = jnp.dot(merged, wout_ref[...], preferred_element_type=F32)
        merged = merged_branches(r + 1) if r + 1 < n_chunks else None
        residual_norm(r, branch)


def _output_stage(attn, proj, x, mod3, woa, woc, wout, final_g, ts=1024, row_chunk=256):
    bsz, seq, d = x.shape
    assert ts % row_chunk == 0

    def col(k):
        return pl.BlockSpec((1, ts, d), lambda b, s: (b, s, k))

    def whole(shape):
        return pl.BlockSpec(shape, lambda b, s: (0,) * len(shape), pipeline_mode=pl.Buffered(1))

    return pl.pallas_call(
        functools.partial(_out_kernel, row_chunk=row_chunk),
        grid=(bsz, seq // ts),
        in_specs=[col(0), col(P_GA), col(P_ZC), col(P_MA), col(P_MC),
                  col(0), pl.BlockSpec((1, 3, d), lambda b, s: (b, 0, 0)),
                  whole((d, d)), whole((d, d)), whole((d, d)), whole((1, d))],
        out_specs=col(0),
        out_shape=jax.ShapeDtypeStruct((bsz, seq, d), F32),
        compiler_params=pltpu.CompilerParams(
            dimension_semantics=("arbitrary", "arbitrary"),
            vmem_limit_bytes=VMEM_LIMIT),
    )(attn, proj, proj, proj, proj, x, mod3, woa, woc, wout, final_g.reshape(1, d))


def kernel(x, c, norm_g, w_ada, b_ada, w_in, conv_w, w_o_attn, w_o_conv, w_out, rel_bias, final_g):
    bsz, seq, d = x.shape
    depth = norm_g.shape[0]
    assert depth == 1, "the fused output stage applies the final RMSNorm after the single layer"
    assert d == N_HEADS * HEAD_DIM and w_in.shape[2] == N_PROJ * d and seq % MOBA_BLOCK == 0
    tables = _bias_tables(rel_bias)
    mod3 = _modulation(c, w_ada[0], b_ada[0]).reshape(bsz, 3, d)
    proj, woa, woc, wout, kmeans = _in_projection(x, mod3, norm_g[0], conv_w[0], w_in[0],
                                                  (w_o_attn[0], w_o_conv[0], w_out[0]))
    attn = _moba_attention(proj, kmeans, tables, rel_bias, d)
    return _output_stage(attn, proj, x, mod3, woa, woc, wout, final_g)
```

```python
import functools
import math

import jax
import jax.numpy as jnp
from jax import lax
from jax.experimental import pallas as pl
from jax.experimental.pallas import tpu as pltpu

N_HEADS = 8
HEAD_DIM = 128
CONV_K = 3
MOBA_BLOCK = 256
MOBA_TOPK = 3
N_BUCKETS = 32
MAX_DISTANCE = 128
EPS = 1e-6
N_PROJ = 10

LANES = 128
SUBLANES = 8
BF16_SUBLANES = 16
SEL_GROUP = 4
NEG = -1e30
LOG2E = math.log2(math.e)
VMEM_LIMIT = 56 * 1024 * 1024

F32 = jnp.float32
BF16 = jnp.bfloat16


def _sigmoid(v):
    return 0.5 * jnp.tanh(0.5 * v) + 0.5


def _silu(v):
    return v * _sigmoid(v)


def _split_bf16(v, terms):
    parts = []
    for _ in range(terms - 1):
        hi = v.astype(BF16).astype(F32)
        parts.append(hi)
        v = v - hi
    return parts + [v]


def _mod_kernel(c_ref, w_ref, b_ref, o_ref):
    bsz = c_ref.shape[0]
    a3 = jnp.concatenate(_split_bf16(_silu(c_ref[...]), 3), axis=0).astype(BF16)
    w_hi, w_lo = _split_bf16(w_ref[...], 2)
    r = (jnp.dot(a3, w_hi.astype(BF16), preferred_element_type=F32)
         + jnp.dot(a3, w_lo.astype(BF16), preferred_element_type=F32))
    o_ref[...] = (r[0:bsz] + r[bsz:2 * bsz]) + r[2 * bsz:3 * bsz] + b_ref[...]


def _modulation(c, w_ada, b_ada):
    bsz, d = c.shape
    n = w_ada.shape[1]
    tn = d
    return pl.pallas_call(
        _mod_kernel,
        grid=(n // tn,),
        in_specs=[pl.BlockSpec((bsz, d), lambda j: (0, 0)),
                  pl.BlockSpec((d, tn), lambda j: (0, j)),
                  pl.BlockSpec((1, tn), lambda j: (0, j))],
        out_specs=pl.BlockSpec((bsz, tn), lambda j: (0, j)),
        out_shape=jax.ShapeDtypeStruct((bsz, n), F32),
        compiler_params=pltpu.CompilerParams(vmem_limit_bytes=VMEM_LIMIT),
    )(c, w_ada, b_ada.reshape(1, n))


W_Q, W_K, W_V, W_GA, W_CB, W_CC, W_CX, W_GC, W_MA, W_MC = range(N_PROJ)
N_FUSED = 7
P_Q, P_K, P_V, P_GA, P_ZC, P_MA, P_MC = range(N_FUSED)
CONV_HALO = 8


def _inproj_kernel(x0_ref, xn_ref, mod0_ref, modn_ref, g_ref, cw_ref, w_hbm, *rest, q_scale,
                   row_chunk, n_late):
    late_in, o_ref, late_out = rest[:n_late], rest[n_late], rest[n_late + 1:2 * n_late + 1]
    kmean_o_ref = rest[2 * n_late + 1]
    h_ref, u_ref, t_ref, w_ref, stage_ref, sem = rest[2 * n_late + 2:]
    for src, dst in zip(late_in, late_out):
        dst[...] = src[...].astype(dst.dtype)
    s = pl.program_id(1)
    step = pl.program_id(0) * pl.num_programs(1) + s
    slot = lax.rem(step, 2)
    ts, d = xn_ref.shape[1], xn_ref.shape[2]
    n_chunks = ts // row_chunk
    g = g_ref[...]

    def normalise(x_ref, mod_ref, dst_slot, r):
        rows = slice(r * row_chunk, (r + 1) * row_chunk)
        xr = x_ref[0, rows, :]
        inv = lax.rsqrt(jnp.mean(xr * xr, axis=-1, keepdims=True) + EPS)
        hr = (xr * inv * g) * (1.0 + mod_ref[0, 1:2, :]) + mod_ref[0, 0:1, :]
        h_ref[dst_slot, rows, :] = hr.astype(BF16)

    depth, _, width = stage_ref.shape

    def weight_chunk_copy(c):
        return pltpu.make_async_copy(w_hbm.at[:, c * width:(c + 1) * width],
                                     stage_ref.at[c % depth], sem.at[c % depth])

    @pl.when(step == 0)
    def _():
        n_w = w_ref.shape[1] // width
        for c in range(min(depth, n_w)):
            weight_chunk_copy(c).start()
        for c in range(n_w):
            weight_chunk_copy(c).wait()
            w_ref[:, c * width:(c + 1) * width] = stage_ref[c % depth].astype(BF16)
            if c + depth < n_w:
                weight_chunk_copy(c + depth).start()
        for r in range(n_chunks):
            normalise(x0_ref, mod0_ref, 0, r)

    def proj(j):
        return jnp.dot(h_ref[slot], w_ref[:, j * d:(j + 1) * d], preferred_element_type=F32)

    def put(k, val):
        o_ref[0, :, k * d:(k + 1) * d] = val.astype(o_ref.dtype)

    @pl.when(s == 0)
    def _():
        u_ref[0:CONV_HALO, :] = jnp.zeros((CONV_HALO, d), F32)

    def keep_cc(acc):
        t_ref[...] = acc

    def conv(acc):
        u_ref[CONV_HALO:CONV_HALO + ts, :] = t_ref[...] * acc
        t_ref[...] = (cw_ref[0:1, :] * u_ref[CONV_HALO - 2:CONV_HALO - 2 + ts, :]
                      + cw_ref[1:2, :] * u_ref[CONV_HALO - 1:CONV_HALO - 1 + ts, :]
                      + cw_ref[2:3, :] * u_ref[CONV_HALO:CONV_HALO + ts, :])
        u_ref[0:CONV_HALO, :] = u_ref[ts:ts + CONV_HALO, :]

    def gate_cb(acc):
        t_ref[...] = t_ref[...] * acc

    def put_keys(acc):
        put(P_K, acc)
        for j in range(ts // MOBA_BLOCK):
            kmean_o_ref[0, 0, j:j + 1, :] = jnp.mean(acc[j * MOBA_BLOCK:(j + 1) * MOBA_BLOCK], axis=0,
                                                     keepdims=True)

    stages = [
        (W_GA, lambda acc: put(P_GA, _silu(acc))),
        (W_MA, lambda acc: put(P_MA, _sigmoid(acc))),
        (W_MC, lambda acc: put(P_MC, _sigmoid(acc))),
        (W_CC, keep_cc),
        (W_CX, conv),
        (W_CB, gate_cb),
        (W_GC, lambda acc: put(P_ZC, t_ref[...] * _silu(acc))),
        (W_Q, lambda acc: put(P_Q, acc * q_scale)),
        (W_K, put_keys),
        (W_V, lambda acc: put(P_V, acc)),
    ]
    norm_after = list(range(len(stages) - 1 - n_chunks, len(stages) - 1))
    acc = proj(stages[0][0])
    for n, (_, epilogue) in enumerate(stages):
        nxt = proj(stages[n + 1][0]) if n + 1 < len(stages) else None
        epilogue(acc)
        if n in norm_after:
            normalise(xn_ref, modn_ref, 1 - slot, norm_after.index(n))
        acc = nxt


def _in_projection(x, mod3, norm_g, conv_w, w_in, late_weights, ts=512, row_chunk=128,
                   w_chunk=256, w_depth=4):
    bsz, seq, d = x.shape
    n = w_in.shape[1]
    assert n % w_chunk == 0 and w_chunk % LANES == 0
    n_out = N_FUSED * d
    n_s = seq // ts
    n_steps = bsz * n_s
    assert ts // row_chunk <= N_PROJ
    slab = d // n_steps
    assert slab * n_steps == d and slab % BF16_SUBLANES == 0
    assert all(w.shape == (d, d) for w in late_weights)
    kern = functools.partial(_inproj_kernel, q_scale=LOG2E * HEAD_DIM ** -0.5, row_chunk=row_chunk,
                             n_late=len(late_weights))

    def next_tile(b, s):
        flat = jnp.minimum(b * n_s + s + 1, n_steps - 1)
        return flat // n_s, flat % n_s

    slab_spec = pl.BlockSpec((slab, d), lambda b, s: (b * n_s + s, 0))
    return pl.pallas_call(
        kern,
        grid=(bsz, n_s),
        in_specs=[pl.BlockSpec((1, ts, d), lambda b, s: (0, 0, 0), pipeline_mode=pl.Buffered(1)),
                  pl.BlockSpec((1, ts, d), lambda b, s: (*next_tile(b, s), 0)),
                  pl.BlockSpec((1, 3, d), lambda b, s: (0, 0, 0)),
                  pl.BlockSpec((1, 3, d), lambda b, s: (next_tile(b, s)[0], 0, 0)),
                  pl.BlockSpec((1, d), lambda b, s: (0, 0)),
                  pl.BlockSpec((CONV_K, d), lambda b, s: (0, 0)),
                  pl.BlockSpec(memory_space=pl.ANY)]
                 + [slab_spec] * len(late_weights),
        out_specs=[pl.BlockSpec((1, ts, n_out), lambda b, s: (b, s, 0))]
                  + [slab_spec] * len(late_weights)
                  + [pl.BlockSpec((1, 1, ts // MOBA_BLOCK, d), lambda b, s: (b, s, 0, 0))],
        out_shape=[jax.ShapeDtypeStruct((bsz, seq, n_out), BF16)]
                  + [jax.ShapeDtypeStruct((d, d), BF16)] * len(late_weights)
                  + [jax.ShapeDtypeStruct((bsz, n_s, ts // MOBA_BLOCK, d), F32)],
        scratch_shapes=[pltpu.VMEM((2, ts, d), BF16), pltpu.VMEM((CONV_HALO + ts, d), F32),
                        pltpu.VMEM((ts, d), F32), pltpu.VMEM((d, n), BF16),
                        pltpu.VMEM((w_depth, d, w_chunk), F32),
                        pltpu.SemaphoreType.DMA((w_depth,))],
        compiler_params=pltpu.CompilerParams(
            dimension_semantics=("arbitrary", "arbitrary"),
            vmem_limit_bytes=VMEM_LIMIT),
    )(x, x, mod3, mod3, norm_g.reshape(1, d), conv_w, w_in, *late_weights)


def _t5_bucket(dist):
    n = jnp.maximum(dist, 0)
    max_exact = N_BUCKETS // 2
    nf = jnp.maximum(n, 1).astype(F32)
    large = max_exact + (jnp.log(nf / max_exact) / math.log(MAX_DISTANCE / max_exact)
                         * (N_BUCKETS - max_exact)).astype(jnp.int32)
    large = jnp.minimum(large, N_BUCKETS - 1)
    return jnp.where(n < max_exact, n, large)


def _table_kernel(rb_ref, o_ref):
    n_heads, blk, width = o_ref.shape
    u = lax.broadcasted_iota(jnp.int32, (SUBLANES, width), 1)
    for t in range(2):
        dist = jnp.where(u < blk, t * blk - u, t * blk + width - u)
        bucket = _t5_bucket(dist)
        for h in range(n_heads):
            g = jnp.zeros((SUBLANES, width), F32)
            for b in range(N_BUCKETS):
                g = jnp.where(bucket == b, rb_ref[b, h] * LOG2E, g)
            g = jnp.where(dist >= 0, g, NEG)
            skewed = pltpu.roll(jnp.broadcast_to(g[0:1, :], (blk, width)), 0, 1,
                                stride=1, stride_axis=0)
            o_ref[h, :, (1 - t) * blk:(2 - t) * blk] = skewed[:, 0:blk]


def _bias_tables(rel_bias):
    shape = (N_HEADS, MOBA_BLOCK, 2 * MOBA_BLOCK)
    return pl.pallas_call(
        _table_kernel,
        grid=(1,),
        in_specs=[pl.BlockSpec(memory_space=pltpu.SMEM)],
        out_specs=pl.BlockSpec(shape, lambda i: (0, 0, 0)),
        out_shape=jax.ShapeDtypeStruct(shape, F32),
        compiler_params=pltpu.CompilerParams(vmem_limit_bytes=VMEM_LIMIT),
    )(rel_bias)


def _attn_kernel(rb_ref, q_ref, k_ref, v_ref, km_ref, tab_ref, o_ref, qaug_ref, kaug_ref, vaug_ref,
                 kmean_ref, *, nb):
    h = pl.program_id(1)
    blk = MOBA_BLOCK
    hd = HEAD_DIM
    seq = nb * blk
    nt = (((1,), (1,)), ((), ()))

    lane = lax.broadcasted_iota(jnp.int32, (blk, LANES), 1)
    for n in range(nb):
        rows = slice(n * blk, (n + 1) * blk)
        kaug_ref[rows, 0:hd] = k_ref[0, rows, :]
        kaug_ref[rows, hd:hd + LANES] = jnp.where((lane == n) | (lane == nb + n), 1.0, 0.0).astype(BF16)
    per_tile = km_ref.shape[2]
    for t in range(nb // per_tile):
        kmean_ref[t * per_tile:(t + 1) * per_tile, :] = km_ref[0, t]
    vaug_ref[:, 0:hd] = v_ref[0]
    vaug_ref[:, hd:2 * hd] = jnp.ones((seq, hd), BF16)

    def select(g):
        q_lo = g * SEL_GROUP * blk
        w = SEL_GROUP * blk
        last_blk = (g + 1) * SEL_GROUP - 1
        nk = min(nb, -(-(last_blk + 1) // SUBLANES) * SUBLANES)

        km3 = jnp.concatenate(_split_bf16(kmean_ref[0:nk, :], 3), axis=0).astype(BF16)
        sc3 = lax.dot_general(km3, q_ref[0, q_lo:q_lo + w, :], nt, preferred_element_type=F32)
        sc = (sc3[0:nk] + sc3[nk:2 * nk]) + sc3[2 * nk:3 * nk]

        n_iota = lax.broadcasted_iota(jnp.int32, (nk, w), 0)
        q_blk = g * SEL_GROUP + lax.shift_right_logical(
            lax.broadcasted_iota(jnp.int32, (nk, w), 1), int(math.log2(blk)))
        past = n_iota < q_blk
        idx = n_iota.astype(F32)
        avail = past
        chosen = jnp.zeros((nk, w), jnp.bool_)
        for _ in range(MOBA_TOPK):
            cur = jnp.where(avail, sc, NEG)
            best = jnp.max(cur, axis=0, keepdims=True)
            first = jnp.min(jnp.where(avail & (cur == best), idx, float(nk)), axis=0, keepdims=True)
            pick = idx == first
            chosen = chosen | pick
            avail = avail & jnp.logical_not(pick)
        far = chosen & (n_iota < q_blk - 1)

        b_far = jnp.full((nk, w), rb_ref[N_BUCKETS - 1, h] * LOG2E, F32)
        b_hi = b_far.astype(BF16).astype(F32)
        add_hi = jnp.where(past, jnp.where(chosen, jnp.where(far, b_hi, 0.0), NEG), 0.0)
        add_lo = jnp.where(far, b_far - b_hi, 0.0)
        pieces = [add_hi, jnp.zeros((nb - nk, w), F32), add_lo, jnp.zeros((LANES - nb - nk, w), F32)]
        add_t = jnp.concatenate([p for p in pieces if p.shape[0]], axis=0)
        qaug_ref[q_lo:q_lo + w, 0:hd] = q_ref[0, q_lo:q_lo + w, :]
        qaug_ref[q_lo:q_lo + w, hd:hd + LANES] = add_t.T.astype(BF16)

    def logits(i):
        n_keys = (i + 1) * blk
        s = lax.dot_general(qaug_ref[i * blk:(i + 1) * blk, :], kaug_ref[0:n_keys, :], nt,
                            preferred_element_type=F32)
        if i == 0:
            return s + tab_ref[0, :, blk:2 * blk]
        if i == 1:
            return s + tab_ref[0]
        return jnp.concatenate([s[:, :n_keys - 2 * blk], s[:, n_keys - 2 * blk:] + tab_ref[0]], axis=1)

    for g in range(nb // SEL_GROUP):
        select(g)
    s = logits(0)
    for i in range(nb):
        s_next = logits(i + 1) if i + 1 < nb else None
        s16 = s.astype(BF16)
        p = jnp.exp2(s16 - jnp.max(s16, axis=-1, keepdims=True))
        pv = jnp.dot(p, vaug_ref[0:(i + 1) * blk, :], preferred_element_type=F32)
        o_ref[0, i * blk:(i + 1) * blk, :] = (pv[:, 0:hd] / pv[:, hd:2 * hd]).astype(o_ref.dtype)
        s = s_next


def _moba_attention(proj, kmeans, tables, rel_bias, d_model):
    bsz, seq, _ = proj.shape
    nb = seq // MOBA_BLOCK
    nh = d_model // HEAD_DIM
    assert MAX_DISTANCE <= MOBA_BLOCK + 1 and 2 * nb <= LANES and MOBA_BLOCK & (MOBA_BLOCK - 1) == 0
    assert nb % SEL_GROUP == 0 and nb % SUBLANES == 0
    kern = functools.partial(_attn_kernel, nb=nb)

    def head_cols(tile):
        return pl.BlockSpec((1, seq, HEAD_DIM), lambda b, h: (b, 0, tile * nh + h))

    return pl.pallas_call(
        kern,
        grid=(bsz, nh),
        in_specs=[pl.BlockSpec(memory_space=pltpu.SMEM),
                  head_cols(P_Q), head_cols(P_K), head_cols(P_V),
                  pl.BlockSpec((1,) + kmeans.shape[1:3] + (HEAD_DIM,), lambda b, h: (b, 0, 0, h)),
                  pl.BlockSpec((1, MOBA_BLOCK, 2 * MOBA_BLOCK), lambda b, h: (h, 0, 0))],
        out_specs=head_cols(0),
        out_shape=jax.ShapeDtypeStruct((bsz, seq, d_model), BF16),
        scratch_shapes=[pltpu.VMEM((seq, HEAD_DIM + LANES), BF16),
                        pltpu.VMEM((seq, HEAD_DIM + LANES), BF16),
                        pltpu.VMEM((seq, 2 * HEAD_DIM), BF16),
                        pltpu.VMEM((nb, HEAD_DIM), F32)],
        compiler_params=pltpu.CompilerParams(
            dimension_semantics=("arbitrary", "arbitrary"),
            vmem_limit_bytes=VMEM_LIMIT),
    )(rel_bias, proj, proj, proj, kmeans, tables)


def _out_kernel(attn_ref, ga_ref, zc_ref, ma_ref, mc_ref, x_ref, mod_ref, woa_ref, woc_ref, wout_ref,
                fg_ref, o_ref, *, row_chunk):
    def merged_branches(r):
        rows = slice(r * row_chunk, (r + 1) * row_chunk)
        y_conv = jnp.dot(zc_ref[0, rows, :], woc_ref[...], preferred_element_type=F32)
        z_attn = attn_ref[0, rows, :].astype(F32) * ga_ref[0, rows, :].astype(F32)
        y_attn = jnp.dot(z_attn.astype(BF16), woa_ref[...], preferred_element_type=F32)
        merged = (ma_ref[0, rows, :].astype(F32) * y_attn
                  + mc_ref[0, rows, :].astype(F32) * y_conv)
        return merged.astype(BF16)

    def residual_norm(r, branch):
        rows = slice(r * row_chunk, (r + 1) * row_chunk)
        res = x_ref[0, rows, :] + mod_ref[0, 2:3, :] * branch
        inv = lax.rsqrt(jnp.mean(res * res, axis=-1, keepdims=True) + EPS)
        o_ref[0, rows, :] = res * inv * fg_ref[...]

    n_chunks = x_ref.shape[1] // row_chunk
    merged = merged_branches(0)
    for r in range(n_chunks):
        branch = jnp.dot(merged, wout_ref[...], preferred_element_type=F32)
        merged = merged_branches(r + 1) if r + 1 < n_chunks else None
        residual_norm(r, branch)


def _output_stage(attn, proj, x, mod3, woa, woc, wout, final_g, ts=1024, row_chunk=256):
    bsz, seq, d = x.shape
    assert ts % row_chunk == 0

    def col(k):
        return pl.BlockSpec((1, ts, d), lambda b, s: (b, s, k))

    def whole(shape):
        return pl.BlockSpec(shape, lambda b, s: (0,) * len(shape), pipeline_mode=pl.Buffered(1))

    return pl.pallas_call(
        functools.partial(_out_kernel, row_chunk=row_chunk),
        grid=(bsz, seq // ts),
        in_specs=[col(0), col(P_GA), col(P_ZC), col(P_MA), col(P_MC),
                  col(0), pl.BlockSpec((1, 3, d), lambda b, s: (b, 0, 0)),
                  whole((d, d)), whole((d, d)), whole((d, d)), whole((1, d))],
        out_specs=col(0),
        out_shape=jax.ShapeDtypeStruct((bsz, seq, d), F32),
        compiler_params=pltpu.CompilerParams(
            dimension_semantics=("arbitrary", "arbitrary"),
            vmem_limit_bytes=VMEM_LIMIT),
    )(attn, proj, proj, proj, proj, x, mod3, woa, woc, wout, final_g.reshape(1, d))


def kernel(x, c, norm_g, w_ada, b_ada, w_in, conv_w, w_o_attn, w_o_conv, w_out, rel_bias, final_g):
    bsz, seq, d = x.shape
    depth = norm_g.shape[0]
    assert depth == 1, "the fused output stage applies the final RMSNorm after the single layer"
    assert d == N_HEADS * HEAD_DIM and w_in.shape[2] == N_PROJ * d and seq % MOBA_BLOCK == 0
    tables = _bias_tables(rel_bias)
    mod3 = _modulation(c, w_ada[0], b_ada[0]).reshape(bsz, 3, d)
    proj, woa, woc, wout, kmeans = _in_projection(x, mod3, norm_g[0], conv_w[0], w_in[0],
                                                  (w_o_attn[0], w_o_conv[0], w_out[0]))
    attn = _moba_attention(proj, kmeans, tables, rel_bias, d)
    return _output_stage(attn, proj, x, mod3, woa, woc, wout, final_g)
```

```python
import functools
import math

import jax
import jax.numpy as jnp
from jax import lax
from jax.experimental import pallas as pl
from jax.experimental.pallas import tpu as pltpu

N_HEADS = 8
HEAD_DIM = 128
CONV_K = 3
MOBA_BLOCK = 256
MOBA_TOPK = 3
N_BUCKETS = 32
MAX_DISTANCE = 128
EPS = 1e-6
N_PROJ = 10

LANES = 128
SUBLANES = 8
BF16_SUBLANES = 16
SEL_GROUP = 4
NEG = -1e30
LOG2E = math.log2(math.e)
VMEM_LIMIT = 56 * 1024 * 1024

F32 = jnp.float32
BF16 = jnp.bfloat16


def _sigmoid(v):
    return 0.5 * jnp.tanh(0.5 * v) + 0.5


def _silu(v):
    return v * _sigmoid(v)


def _split_bf16(v, terms):
    parts = []
    for _ in range(terms - 1):
        hi = v.astype(BF16).astype(F32)
        parts.append(hi)
        v = v - hi
    return parts + [v]


def _mod_kernel(c_ref, w_ref, b_ref, o_ref):
    bsz = c_ref.shape[0]
    a3 = jnp.concatenate(_split_bf16(_silu(c_ref[...]), 3), axis=0).astype(BF16)
    w_hi, w_lo = _split_bf16(w_ref[...], 2)
    r = (jnp.dot(a3, w_hi.astype(BF16), preferred_element_type=F32)
         + jnp.dot(a3, w_lo.astype(BF16), preferred_element_type=F32))
    o_ref[...] = (r[0:bsz] + r[bsz:2 * bsz]) + r[2 * bsz:3 * bsz] + b_ref[...]


def _modulation(c, w_ada, b_ada):
    bsz, d = c.shape
    n = w_ada.shape[1]
    tn = d
    return pl.pallas_call(
        _mod_kernel,
        grid=(n // tn,),
        in_specs=[pl.BlockSpec((bsz, d), lambda j: (0, 0)),
                  pl.BlockSpec((d, tn), lambda j: (0, j)),
                  pl.BlockSpec((1, tn), lambda j: (0, j))],
        out_specs=pl.BlockSpec((bsz, tn), lambda j: (0, j)),
        out_shape=jax.ShapeDtypeStruct((bsz, n), F32),
        compiler_params=pltpu.CompilerParams(vmem_limit_bytes=VMEM_LIMIT),
    )(c, w_ada, b_ada.reshape(1, n))


W_Q, W_K, W_V, W_GA, W_CB, W_CC, W_CX, W_GC, W_MA, W_MC = range(N_PROJ)
N_FUSED = 7
P_Q, P_K, P_V, P_GA, P_ZC, P_MA, P_MC = range(N_FUSED)
CONV_HALO = 8


def _inproj_kernel(x0_ref, xn_ref, mod0_ref, modn_ref, g_ref, cw_ref, w_hbm, *rest, q_scale,
                   row_chunk, n_late):
    late_in, o_ref, late_out = rest[:n_late], rest[n_late], rest[n_late + 1:2 * n_late + 1]
    kmean_o_ref = rest[2 * n_late + 1]
    h_ref, u_ref, t_ref, w_ref, stage_ref, sem = rest[2 * n_late + 2:]
    for src, dst in zip(late_in, late_out):
        dst[...] = src[...].astype(dst.dtype)
    s = pl.program_id(1)
    step = pl.program_id(0) * pl.num_programs(1) + s
    slot = lax.rem(step, 2)
    ts, d = xn_ref.shape[1], xn_ref.shape[2]
    n_chunks = ts // row_chunk
    g = g_ref[...]

    def normalise(x_ref, mod_ref, dst_slot, r):
        rows = slice(r * row_chunk, (r + 1) * row_chunk)
        xr = x_ref[0, rows, :]
        inv = lax.rsqrt(jnp.mean(xr * xr, axis=-1, keepdims=True) + EPS)
        hr = (xr * inv * g) * (1.0 + mod_ref[0, 1:2, :]) + mod_ref[0, 0:1, :]
        h_ref[dst_slot, rows, :] = hr.astype(BF16)

    depth, _, width = stage_ref.shape

    def weight_chunk_copy(c):
        return pltpu.make_async_copy(w_hbm.at[:, c * width:(c + 1) * width],
                                     stage_ref.at[c % depth], sem.at[c % depth])

    @pl.when(step == 0)
    def _():
        n_w = w_ref.shape[1] // width
        for c in range(min(depth, n_w)):
            weight_chunk_copy(c).start()
        for c in range(n_w):
            weight_chunk_copy(c).wait()
            w_ref[:, c * width:(c + 1) * width] = stage_ref[c % depth].astype(BF16)
            if c + depth < n_w:
                weight_chunk_copy(c + depth).start()
        for r in range(n_chunks):
            normalise(x0_ref, mod0_ref, 0, r)

    def proj(j):
        return jnp.dot(h_ref[slot], w_ref[:, j * d:(j + 1) * d], preferred_element_type=F32)

    def put(k, val):
        o_ref[0, :, k * d:(k + 1) * d] = val.astype(o_ref.dtype)

    @pl.when(s == 0)
    def _():
        u_ref[0:CONV_HALO, :] = jnp.zeros((CONV_HALO, d), F32)

    def keep_cc(acc):
        t_ref[...] = acc

    def conv(acc):
        u_ref[CONV_HALO:CONV_HALO + ts, :] = t_ref[...] * acc
        t_ref[...] = (cw_ref[0:1, :] * u_ref[CONV_HALO - 2:CONV_HALO - 2 + ts, :]
                      + cw_ref[1:2, :] * u_ref[CONV_HALO - 1:CONV_HALO - 1 + ts, :]
                      + cw_ref[2:3, :] * u_ref[CONV_HALO:CONV_HALO + ts, :])
        u_ref[0:CONV_HALO, :] = u_ref[ts:ts + CONV_HALO, :]

    def gate_cb(acc):
        t_ref[...] = t_ref[...] * acc

    def put_keys(acc):
        put(P_K, acc)
        for j in range(ts // MOBA_BLOCK):
            kmean_o_ref[0, 0, j:j + 1, :] = jnp.mean(acc[j * MOBA_BLOCK:(j + 1) * MOBA_BLOCK], axis=0,
                                                     keepdims=True)

    stages = [
        (W_GA, lambda acc: put(P_GA, _silu(acc))),
        (W_MA, lambda acc: put(P_MA, _sigmoid(acc))),
        (W_MC, lambda acc: put(P_MC, _sigmoid(acc))),
        (W_CC, keep_cc),
        (W_CX, conv),
        (W_CB, gate_cb),
        (W_GC, lambda acc: put(P_ZC, t_ref[...] * _silu(acc))),
        (W_Q, lambda acc: put(P_Q, acc * q_scale)),
        (W_K, put_keys),
        (W_V, lambda acc: put(P_V, acc)),
    ]
    norm_after = list(range(len(stages) - 1 - n_chunks, len(stages) - 1))
    acc = proj(stages[0][0])
    for n, (_, epilogue) in enumerate(stages):
        nxt = proj(stages[n + 1][0]) if n + 1 < len(stages) else None
        epilogue(acc)
        if n in norm_after:
            normalise(xn_ref, modn_ref, 1 - slot, norm_after.index(n))
        acc = nxt


def _in_projection(x, mod3, norm_g, conv_w, w_in, late_weights, ts=512, row_chunk=128,
                   w_chunk=256, w_depth=4):
    bsz, seq, d = x.shape
    n = w_in.shape[1]
    assert n % w_chunk == 0 and w_chunk % LANES == 0
    n_out = N_FUSED * d
    n_s = seq // ts
    n_steps = bsz * n_s
    assert ts // row_chunk <= N_PROJ
    slab = d // n_steps
    assert slab * n_steps == d and slab % BF16_SUBLANES == 0
    assert all(w.shape == (d, d) for w in late_weights)
    kern = functools.partial(_inproj_kernel, q_scale=LOG2E * HEAD_DIM ** -0.5, row_chunk=row_chunk,
                             n_late=len(late_weights))

    def next_tile(b, s):
        flat = jnp.minimum(b * n_s + s + 1, n_steps - 1)
        return flat // n_s, flat % n_s

    slab_spec = pl.BlockSpec((slab, d), lambda b, s: (b * n_s + s, 0))
    return pl.pallas_call(
        kern,
        grid=(bsz, n_s),
        in_specs=[pl.BlockSpec((1, ts, d), lambda b, s: (0, 0, 0), pipeline_mode=pl.Buffered(1)),
                  pl.BlockSpec((1, ts, d), lambda b, s: (*next_tile(b, s), 0)),
                  pl.BlockSpec((1, 3, d), lambda b, s: (0, 0, 0)),
                  pl.BlockSpec((1, 3, d), lambda b, s: (next_tile(b, s)[0], 0, 0)),
                  pl.BlockSpec((1, d), lambda b, s: (0, 0)),
                  pl.BlockSpec((CONV_K, d), lambda b, s: (0, 0)),
                  pl.BlockSpec(memory_space=pl.ANY)]
                 + [slab_spec] * len(late_weights),
        out_specs=[pl.BlockSpec((1, ts, n_out), lambda b, s: (b, s, 0))]
                  + [slab_spec] * len(late_weights)
                  + [pl.BlockSpec((1, 1, ts // MOBA_BLOCK, d), lambda b, s: (b, s, 0, 0))],
        out_shape=[jax.ShapeDtypeStruct((bsz, seq, n_out), BF16)]
                  + [jax.ShapeDtypeStruct((d, d), BF16)] * len(late_weights)
                  + [jax.ShapeDtypeStruct((bsz, n_s, ts // MOBA_BLOCK, d), F32)],
        scratch_shapes=[pltpu.VMEM((2, ts, d), BF16), pltpu.VMEM((CONV_HALO + ts, d), F32),
                        pltpu.VMEM((ts, d), F32), pltpu.VMEM((d, n), BF16),
                        pltpu.VMEM((w_depth, d, w_chunk), F32),
                        pltpu.SemaphoreType.DMA((w_depth,))],
        compiler_params=pltpu.CompilerParams(
            dimension_semantics=("arbitrary", "arbitrary"),
            vmem_limit_bytes=VMEM_LIMIT),
    )(x, x, mod3, mod3, norm_g.reshape(1, d), conv_w, w_in, *late_weights)


def _t5_bucket(dist):
    n = jnp.maximum(dist, 0)
    max_exact = N_BUCKETS // 2
    nf = jnp.maximum(n, 1).astype(F32)
    large = max_exact + (jnp.log(nf / max_exact) / math.log(MAX_DISTANCE / max_exact)
                         * (N_BUCKETS - max_exact)).astype(jnp.int32)
    large = jnp.minimum(large, N_BUCKETS - 1)
    return jnp.where(n < max_exact, n, large)


def _table_kernel(rb_ref, o_ref):
    n_heads, blk, width = o_ref.shape
    u = lax.broadcasted_iota(jnp.int32, (SUBLANES, width), 1)
    for t in range(2):
        dist = jnp.where(u < blk, t * blk - u, t * blk + width - u)
        bucket = _t5_bucket(dist)
        for h in range(n_heads):
            g = jnp.zeros((SUBLANES, width), F32)
            for b in range(N_BUCKETS):
                g = jnp.where(bucket == b, rb_ref[b, h] * LOG2E, g)
            g = jnp.where(dist >= 0, g, NEG)
            skewed = pltpu.roll(jnp.broadcast_to(g[0:1, :], (blk, width)), 0, 1,
                                stride=1, stride_axis=0)
            o_ref[h, :, (1 - t) * blk:(2 - t) * blk] = skewed[:, 0:blk]


def _bias_tables(rel_bias):
    shape = (N_HEADS, MOBA_BLOCK, 2 * MOBA_BLOCK)
    return pl.pallas_call(
        _table_kernel,
        grid=(1,),
        in_specs=[pl.BlockSpec(memory_space=pltpu.SMEM)],
        out_specs=pl.BlockSpec(shape, lambda i: (0, 0, 0)),
        out_shape=jax.ShapeDtypeStruct(shape, F32),
        compiler_params=pltpu.CompilerParams(vmem_limit_bytes=VMEM_LIMIT),
    )(rel_bias)


def _attn_kernel(rb_ref, q_ref, k_ref, v_ref, km_ref, tab_ref, o_ref, qaug_ref, kaug_ref, vaug_ref,
                 kmean_ref, *, nb):
    h = pl.program_id(1)
    blk = MOBA_BLOCK
    hd = HEAD_DIM
    seq = nb * blk
    nt = (((1,), (1,)), ((), ()))

    lane = lax.broadcasted_iota(jnp.int32, (blk, LANES), 1)
    for n in range(nb):
        rows = slice(n * blk, (n + 1) * blk)
        kaug_ref[rows, 0:hd] = k_ref[0, rows, :]
        kaug_ref[rows, hd:hd + LANES] = jnp.where((lane == n) | (lane == nb + n), 1.0, 0.0).astype(BF16)
    per_tile = km_ref.shape[2]
    for t in range(nb // per_tile):
        kmean_ref[t * per_tile:(t + 1) * per_tile, :] = km_ref[0, t]
    vaug_ref[:, 0:hd] = v_ref[0]
    vaug_ref[:, hd:2 * hd] = jnp.ones((seq, hd), BF16)

    def select(g):
        q_lo = g * SEL_GROUP * blk
        w = SEL_GROUP * blk
        last_blk = (g + 1) * SEL_GROUP - 1
        nk = min(nb, -(-(last_blk + 1) // SUBLANES) * SUBLANES)

        km3 = jnp.concatenate(_split_bf16(kmean_ref[0:nk, :], 3), axis=0).astype(BF16)
        sc3 = lax.dot_general(km3, q_ref[0, q_lo:q_lo + w, :], nt, preferred_element_type=F32)
        sc = (sc3[0:nk] + sc3[nk:2 * nk]) + sc3[2 * nk:3 * nk]

        n_iota = lax.broadcasted_iota(jnp.int32, (nk, w), 0)
        q_blk = g * SEL_GROUP + lax.shift_right_logical(
            lax.broadcasted_iota(jnp.int32, (nk, w), 1), int(math.log2(blk)))
        past = n_iota < q_blk
        sc = jnp.where(past, sc, NEG)
        rank = jnp.zeros((nk, w), jnp.int32)
        for m in range(last_blk):
            row = sc[m:m + 1, :]
            beats = (row > sc) | ((row == sc) & (m < n_iota))
            rank = rank + jnp.where(beats, 1, 0)
        chosen = past & (rank < MOBA_TOPK)
        far = chosen & (n_iota < q_blk - 1)

        b_far = jnp.full((nk, w), rb_ref[N_BUCKETS - 1, h] * LOG2E, F32)
        b_hi = b_far.astype(BF16).astype(F32)
        add_hi = jnp.where(past, jnp.where(chosen, jnp.where(far, b_hi, 0.0), NEG), 0.0)
        add_lo = jnp.where(far, b_far - b_hi, 0.0)
        pieces = [add_hi, jnp.zeros((nb - nk, w), F32), add_lo, jnp.zeros((LANES - nb - nk, w), F32)]
        add_t = jnp.concatenate([p for p in pieces if p.shape[0]], axis=0)
        qaug_ref[q_lo:q_lo + w, 0:hd] = q_ref[0, q_lo:q_lo + w, :]
        qaug_ref[q_lo:q_lo + w, hd:hd + LANES] = add_t.T.astype(BF16)

    def logits(i):
        n_keys = (i + 1) * blk
        s = lax.dot_general(qaug_ref[i * blk:(i + 1) * blk, :], kaug_ref[0:n_keys, :], nt,
                            preferred_element_type=F32)
        if i == 0:
            return s + tab_ref[0, :, blk:2 * blk]
        if i == 1:
            return s + tab_ref[0]
        return jnp.concatenate([s[:, :n_keys - 2 * blk], s[:, n_keys - 2 * blk:] + tab_ref[0]], axis=1)

    for g in range(nb // SEL_GROUP):
        select(g)
    s = logits(0)
    for i in range(nb):
        s_next = logits(i + 1) if i + 1 < nb else None
        s16 = s.astype(BF16)
        p = jnp.exp2(s16 - jnp.max(s16, axis=-1, keepdims=True))
        pv = jnp.dot(p, vaug_ref[0:(i + 1) * blk, :], preferred_element_type=F32)
        o_ref[0, i * blk:(i + 1) * blk, :] = (pv[:, 0:hd] / pv[:, hd:2 * hd]).astype(o_ref.dtype)
        s = s_next


def _moba_attention(proj, kmeans, tables, rel_bias, d_model):
    bsz, seq, _ = proj.shape
    nb = seq // MOBA_BLOCK
    nh = d_model // HEAD_DIM
    assert MAX_DISTANCE <= MOBA_BLOCK + 1 and 2 * nb <= LANES and MOBA_BLOCK & (MOBA_BLOCK - 1) == 0
    assert nb % SEL_GROUP == 0 and nb % SUBLANES == 0
    kern = functools.partial(_attn_kernel, nb=nb)

    def head_cols(tile):
        return pl.BlockSpec((1, seq, HEAD_DIM), lambda b, h: (b, 0, tile * nh + h))

    return pl.pallas_call(
        kern,
        grid=(bsz, nh),
        in_specs=[pl.BlockSpec(memory_space=pltpu.SMEM),
                  head_cols(P_Q), head_cols(P_K), head_cols(P_V),
                  pl.BlockSpec((1,) + kmeans.shape[1:3] + (HEAD_DIM,), lambda b, h: (b, 0, 0, h)),
                  pl.BlockSpec((1, MOBA_BLOCK, 2 * MOBA_BLOCK), lambda b, h: (h, 0, 0))],
        out_specs=head_cols(0),
        out_shape=jax.ShapeDtypeStruct((bsz, seq, d_model), BF16),
        scratch_shapes=[pltpu.VMEM((seq, HEAD_DIM + LANES), BF16),
                        pltpu.VMEM((seq, HEAD_DIM + LANES), BF16),
                        pltpu.VMEM((seq, 2 * HEAD_DIM), BF16),
                        pltpu.VMEM((nb, HEAD_DIM), F32)],
        compiler_params=pltpu.CompilerParams(
            dimension_semantics=("arbitrary", "arbitrary"),
            vmem_limit_bytes=VMEM_LIMIT),
    )(rel_bias, proj, proj, proj, kmeans, tables)


def _out_kernel(attn_ref, ga_ref, zc_ref, ma_ref, mc_ref, x_ref, mod_ref, woa_ref, woc_ref, wout_ref,
                fg_ref, o_ref, *, row_chunk):
    def merged_branches(r):
        rows = slice(r * row_chunk, (r + 1) * row_chunk)
        y_conv = jnp.dot(zc_ref[0, rows, :], woc_ref[...], preferred_element_type=F32)
        z_attn = attn_ref[0, rows, :].astype(F32) * ga_ref[0, rows, :].astype(F32)
        y_attn = jnp.dot(z_attn.astype(BF16), woa_ref[...], preferred_element_type=F32)
        merged = (ma_ref[0, rows, :].astype(F32) * y_attn
                  + mc_ref[0, rows, :].astype(F32) * y_conv)
        return merged.astype(BF16)

    def residual_norm(r, branch):
        rows = slice(r * row_chunk, (r + 1) * row_chunk)
        res = x_ref[0, rows, :] + mod_ref[0, 2:3, :] * branch
        inv = lax.rsqrt(jnp.mean(res * res, axis=-1, keepdims=True) + EPS)
        o_ref[0, rows, :] = res * inv * fg_ref[...]

    n_chunks = x_ref.shape[1] // row_chunk
    merged = merged_branches(0)
    for r in range(n_chunks):
        branch = jnp.dot(merged, wout_ref[...], preferred_element_type=F32)
        merged = merged_branches(r + 1) if r + 1 < n_chunks else None
        residual_norm(r, branch)


def _output_stage(attn, proj, x, mod3, woa, woc, wout, final_g, ts=1024, row_chunk=256):
    bsz, seq, d = x.shape
    assert ts % row_chunk == 0

    def col(k):
        return pl.BlockSpec((1, ts, d), lambda b, s: (b, s, k))

    def whole(shape):
        return pl.BlockSpec(shape, lambda b, s: (0,) * len(shape), pipeline_mode=pl.Buffered(1))

    return pl.pallas_call(
        functools.partial(_out_kernel, row_chunk=row_chunk),
        grid=(bsz, seq // ts),
        in_specs=[col(0), col(P_GA), col(P_ZC), col(P_MA), col(P_MC),
                  col(0), pl.BlockSpec((1, 3, d), lambda b, s: (b, 0, 0)),
                  whole((d, d)), whole((d, d)), whole((d, d)), whole((1, d))],
        out_specs=col(0),
        out_shape=jax.ShapeDtypeStruct((bsz, seq, d), F32),
        compiler_params=pltpu.CompilerParams(
            dimension_semantics=("arbitrary", "arbitrary"),
            vmem_limit_bytes=VMEM_LIMIT),
    )(attn, proj, proj, proj, proj, x, mod3, woa, woc, wout, final_g.reshape(1, d))


def kernel(x, c, norm_g, w_ada, b_ada, w_in, conv_w, w_o_attn, w_o_conv, w_out, rel_bias, final_g):
    bsz, seq, d = x.shape
    depth = norm_g.shape[0]
    assert depth == 1, "the fused output stage applies the final RMSNorm after the single layer"
    assert d == N_HEADS * HEAD_DIM and w_in.shape[2] == N_PROJ * d and seq % MOBA_BLOCK == 0
    tables = _bias_tables(rel_bias)
    mod3 = _modulation(c, w_ada[0], b_ada[0]).reshape(bsz, 3, d)
    proj, woa, woc, wout, kmeans = _in_projection(x, mod3, norm_g[0], conv_w[0], w_in[0],
                                                  (w_o_attn[0], w_o_conv[0], w_out[0]))
    attn = _moba_attention(proj, kmeans, tables, rel_bias, d)
    return _output_stage(attn, proj, x, mod3, woa, woc, wout, final_g)
```

```python
import functools
import math

import jax
import jax.numpy as jnp
from jax import lax
from jax.experimental import pallas as pl
from jax.experimental.pallas import tpu as pltpu

N_HEADS = 8
HEAD_DIM = 128
CONV_K = 3
MOBA_BLOCK = 256
MOBA_TOPK = 3
N_BUCKETS = 32
MAX_DISTANCE = 128
EPS = 1e-6
N_PROJ = 10

LANES = 128
SUBLANES = 8
BF16_SUBLANES = 16
SEL_GROUP = 4
NEG = -1e30
LOG2E = math.log2(math.e)
VMEM_LIMIT = 56 * 1024 * 1024

F32 = jnp.float32
BF16 = jnp.bfloat16


def _sigmoid(v):
    return 0.5 * jnp.tanh(0.5 * v) + 0.5


def _silu(v):
    return v * _sigmoid(v)


def _split_bf16(v, terms):
    parts = []
    for _ in range(terms - 1):
        hi = v.astype(BF16).astype(F32)
        parts.append(hi)
        v = v - hi
    return parts + [v]


def _mod_kernel(c_ref, w_ref, b_ref, o_ref):
    bsz = c_ref.shape[0]
    a3 = jnp.concatenate(_split_bf16(_silu(c_ref[...]), 3), axis=0).astype(BF16)
    w_hi, w_lo = _split_bf16(w_ref[...], 2)
    r = (jnp.dot(a3, w_hi.astype(BF16), preferred_element_type=F32)
         + jnp.dot(a3, w_lo.astype(BF16), preferred_element_type=F32))
    o_ref[...] = (r[0:bsz] + r[bsz:2 * bsz]) + r[2 * bsz:3 * bsz] + b_ref[...]


def _modulation(c, w_ada, b_ada):
    bsz, d = c.shape
    n = w_ada.shape[1]
    tn = d
    return pl.pallas_call(
        _mod_kernel,
        grid=(n // tn,),
        in_specs=[pl.BlockSpec((bsz, d), lambda j: (0, 0)),
                  pl.BlockSpec((d, tn), lambda j: (0, j)),
                  pl.BlockSpec((1, tn), lambda j: (0, j))],
        out_specs=pl.BlockSpec((bsz, tn), lambda j: (0, j)),
        out_shape=jax.ShapeDtypeStruct((bsz, n), F32),
        compiler_params=pltpu.CompilerParams(vmem_limit_bytes=VMEM_LIMIT),
    )(c, w_ada, b_ada.reshape(1, n))


W_Q, W_K, W_V, W_GA, W_CB, W_CC, W_CX, W_GC, W_MA, W_MC = range(N_PROJ)
N_FUSED = 7
P_Q, P_K, P_V, P_GA, P_ZC, P_MA, P_MC = range(N_FUSED)
CONV_HALO = 8


def _inproj_kernel(x0_ref, xn_ref, mod0_ref, modn_ref, g_ref, cw_ref, w_hbm, *rest, q_scale,
                   row_chunk, n_late):
    late_in, o_ref, late_out = rest[:n_late], rest[n_late], rest[n_late + 1:2 * n_late + 1]
    kmean_o_ref = rest[2 * n_late + 1]
    h_ref, u_ref, t_ref, w_ref, stage_ref, sem = rest[2 * n_late + 2:]
    for src, dst in zip(late_in, late_out):
        dst[...] = src[...].astype(dst.dtype)
    s = pl.program_id(1)
    step = pl.program_id(0) * pl.num_programs(1) + s
    slot = lax.rem(step, 2)
    ts, d = xn_ref.shape[1], xn_ref.shape[2]
    n_chunks = ts // row_chunk
    g = g_ref[...]

    def normalise(x_ref, mod_ref, dst_slot, r):
        rows = slice(r * row_chunk, (r + 1) * row_chunk)
        xr = x_ref[0, rows, :]
        inv = lax.rsqrt(jnp.mean(xr * xr, axis=-1, keepdims=True) + EPS)
        hr = (xr * inv * g) * (1.0 + mod_ref[0, 1:2, :]) + mod_ref[0, 0:1, :]
        h_ref[dst_slot, rows, :] = hr.astype(BF16)

    depth, _, width = stage_ref.shape

    def weight_chunk_copy(c):
        return pltpu.make_async_copy(w_hbm.at[:, c * width:(c + 1) * width],
                                     stage_ref.at[c % depth], sem.at[c % depth])

    @pl.when(step == 0)
    def _():
        n_w = w_ref.shape[1] // width
        for c in range(min(depth, n_w)):
            weight_chunk_copy(c).start()
        for c in range(n_w):
            weight_chunk_copy(c).wait()
            w_ref[:, c * width:(c + 1) * width] = stage_ref[c % depth].astype(BF16)
            if c + depth < n_w:
                weight_chunk_copy(c + depth).start()
        for r in range(n_chunks):
            normalise(x0_ref, mod0_ref, 0, r)

    def proj(j):
        return jnp.dot(h_ref[slot], w_ref[:, j * d:(j + 1) * d], preferred_element_type=F32)

    def put(k, val):
        o_ref[0, :, k * d:(k + 1) * d] = val.astype(o_ref.dtype)

    @pl.when(s == 0)
    def _():
        u_ref[0:CONV_HALO, :] = jnp.zeros((CONV_HALO, d), F32)

    def keep_cc(acc):
        t_ref[...] = acc

    def conv(acc):
        u_ref[CONV_HALO:CONV_HALO + ts, :] = t_ref[...] * acc
        t_ref[...] = (cw_ref[0:1, :] * u_ref[CONV_HALO - 2:CONV_HALO - 2 + ts, :]
                      + cw_ref[1:2, :] * u_ref[CONV_HALO - 1:CONV_HALO - 1 + ts, :]
                      + cw_ref[2:3, :] * u_ref[CONV_HALO:CONV_HALO + ts, :])
        u_ref[0:CONV_HALO, :] = u_ref[ts:ts + CONV_HALO, :]

    def gate_cb(acc):
        t_ref[...] = t_ref[...] * acc

    def put_keys(acc):
        put(P_K, acc)
        for j in range(ts // MOBA_BLOCK):
            kmean_o_ref[0, 0, j:j + 1, :] = jnp.mean(acc[j * MOBA_BLOCK:(j + 1) * MOBA_BLOCK], axis=0,
                                                     keepdims=True)

    stages = [
        (W_GA, lambda acc: put(P_GA, _silu(acc))),
        (W_MA, lambda acc: put(P_MA, _sigmoid(acc))),
        (W_MC, lambda acc: put(P_MC, _sigmoid(acc))),
        (W_CC, keep_cc),
        (W_CX, conv),
        (W_CB, gate_cb),
        (W_GC, lambda acc: put(P_ZC, t_ref[...] * _silu(acc))),
        (W_Q, lambda acc: put(P_Q, acc * q_scale)),
        (W_K, put_keys),
        (W_V, lambda acc: put(P_V, acc)),
    ]
    norm_after = list(range(len(stages) - 1 - n_chunks, len(stages) - 1))
    acc = proj(stages[0][0])
    for n, (_, epilogue) in enumerate(stages):
        nxt = proj(stages[n + 1][0]) if n + 1 < len(stages) else None
        epilogue(acc)
        if n in norm_after:
            normalise(xn_ref, modn_ref, 1 - slot, norm_after.index(n))
        acc = nxt


def _in_projection(x, mod3, norm_g, conv_w, w_in, late_weights, ts=512, row_chunk=128,
                   w_chunk=256, w_depth=4):
    bsz, seq, d = x.shape
    n = w_in.shape[1]
    assert n % w_chunk == 0 and w_chunk % LANES == 0
    n_out = N_FUSED * d
    n_s = seq // ts
    n_steps = bsz * n_s
    assert ts // row_chunk <= N_PROJ
    slab = d // n_steps
    assert slab * n_steps == d and slab % BF16_SUBLANES == 0
    assert all(w.shape == (d, d) for w in late_weights)
    kern = functools.partial(_inproj_kernel, q_scale=LOG2E * HEAD_DIM ** -0.5, row_chunk=row_chunk,
                             n_late=len(late_weights))

    def next_tile(b, s):
        flat = jnp.minimum(b * n_s + s + 1, n_steps - 1)
        return flat // n_s, flat % n_s

    slab_spec = pl.BlockSpec((slab, d), lambda b, s: (b * n_s + s, 0))
    return pl.pallas_call(
        kern,
        grid=(bsz, n_s),
        in_specs=[pl.BlockSpec((1, ts, d), lambda b, s: (0, 0, 0), pipeline_mode=pl.Buffered(1)),
                  pl.BlockSpec((1, ts, d), lambda b, s: (*next_tile(b, s), 0)),
                  pl.BlockSpec((1, 3, d), lambda b, s: (0, 0, 0)),
                  pl.BlockSpec((1, 3, d), lambda b, s: (next_tile(b, s)[0], 0, 0)),
                  pl.BlockSpec((1, d), lambda b, s: (0, 0)),
                  pl.BlockSpec((CONV_K, d), lambda b, s: (0, 0)),
                  pl.BlockSpec(memory_space=pl.ANY)]
                 + [slab_spec] * len(late_weights),
        out_specs=[pl.BlockSpec((1, ts, n_out), lambda b, s: (b, s, 0))]
                  + [slab_spec] * len(late_weights)
                  + [pl.BlockSpec((1, 1, ts // MOBA_BLOCK, d), lambda b, s: (b, s, 0, 0))],
        out_shape=[jax.ShapeDtypeStruct((bsz, seq, n_out), BF16)]
                  + [jax.ShapeDtypeStruct((d, d), BF16)] * len(late_weights)
                  + [jax.ShapeDtypeStruct((bsz, n_s, ts // MOBA_BLOCK, d), F32)],
        scratch_shapes=[pltpu.VMEM((2, ts, d), BF16), pltpu.VMEM((CONV_HALO + ts, d), F32),
                        pltpu.VMEM((ts, d), F32), pltpu.VMEM((d, n), BF16),
                        pltpu.VMEM((w_depth, d, w_chunk), F32),
                        pltpu.SemaphoreType.DMA((w_depth,))],
        compiler_params=pltpu.CompilerParams(
            dimension_semantics=("arbitrary", "arbitrary"),
            vmem_limit_bytes=VMEM_LIMIT),
    )(x, x, mod3, mod3, norm_g.reshape(1, d), conv_w, w_in, *late_weights)


def _t5_bucket(dist):
    n = jnp.maximum(dist, 0)
    max_exact = N_BUCKETS // 2
    nf = jnp.maximum(n, 1).astype(F32)
    large = max_exact + (jnp.log(nf / max_exact) / math.log(MAX_DISTANCE / max_exact)
                         * (N_BUCKETS - max_exact)).astype(jnp.int32)
    large = jnp.minimum(large, N_BUCKETS - 1)
    return jnp.where(n < max_exact, n, large)


def _table_kernel(rb_ref, o_ref):
    n_heads, blk, width = o_ref.shape
    u = lax.broadcasted_iota(jnp.int32, (SUBLANES, width), 1)
    for t in range(2):
        dist = jnp.where(u < blk, t * blk - u, t * blk + width - u)
        bucket = _t5_bucket(dist)
        for h in range(n_heads):
            g = jnp.zeros((SUBLANES, width), F32)
            for b in range(N_BUCKETS):
                g = jnp.where(bucket == b, rb_ref[b, h] * LOG2E, g)
            g = jnp.where(dist >= 0, g, NEG)
            skewed = pltpu.roll(jnp.broadcast_to(g[0:1, :], (blk, width)), 0, 1,
                                stride=1, stride_axis=0)
            o_ref[h, :, (1 - t) * blk:(2 - t) * blk] = skewed[:, 0:blk]


def _bias_tables(rel_bias):
    shape = (N_HEADS, MOBA_BLOCK, 2 * MOBA_BLOCK)
    return pl.pallas_call(
        _table_kernel,
        grid=(1,),
        in_specs=[pl.BlockSpec(memory_space=pltpu.SMEM)],
        out_specs=pl.BlockSpec(shape, lambda i: (0, 0, 0)),
        out_shape=jax.ShapeDtypeStruct(shape, F32),
        compiler_params=pltpu.CompilerParams(vmem_limit_bytes=VMEM_LIMIT),
    )(rel_bias)


def _attn_kernel(rb_ref, q_ref, k_ref, v_ref, km_ref, tab_ref, o_ref, qaug_ref, kaug_ref, vaug_ref,
                 kmean_ref, *, nb):
    h = pl.program_id(1)
    blk = MOBA_BLOCK
    hd = HEAD_DIM
    seq = nb * blk
    nt = (((1,), (1,)), ((), ()))

    @pl.when((pl.program_id(0) == 0) & (h == 0))
    def _():
        lane = lax.broadcasted_iota(jnp.int32, (blk, LANES), 1)
        for n in range(nb):
            onehot = jnp.where((lane == n) | (lane == nb + n), 1.0, 0.0)
            kaug_ref[n * blk:(n + 1) * blk, hd:hd + LANES] = onehot.astype(BF16)
        vaug_ref[:, hd:2 * hd] = jnp.ones((seq, hd), BF16)

    kaug_ref[:, 0:hd] = k_ref[0]
    per_tile = km_ref.shape[2]
    for t in range(nb // per_tile):
        kmean_ref[t * per_tile:(t + 1) * per_tile, :] = km_ref[0, t]
    vaug_ref[:, 0:hd] = v_ref[0]

    def select(g):
        q_lo = g * SEL_GROUP * blk
        w = SEL_GROUP * blk
        last_blk = (g + 1) * SEL_GROUP - 1
        nk = min(nb, -(-(last_blk + 1) // SUBLANES) * SUBLANES)

        km3 = jnp.concatenate(_split_bf16(kmean_ref[0:nk, :], 3), axis=0).astype(BF16)
        sc3 = lax.dot_general(km3, q_ref[0, q_lo:q_lo + w, :], nt, preferred_element_type=F32)
        sc = (sc3[0:nk] + sc3[nk:2 * nk]) + sc3[2 * nk:3 * nk]

        n_iota = lax.broadcasted_iota(jnp.int32, (nk, w), 0)
        q_blk = g * SEL_GROUP + lax.shift_right_logical(
            lax.broadcasted_iota(jnp.int32, (nk, w), 1), int(math.log2(blk)))
        rank = jnp.zeros((nk, w), jnp.int32)
        for m in range(last_blk):
            row = sc[m:m + 1, :]
            beats = (row > sc) | ((row == sc) & (m < n_iota))
            rank = rank + jnp.where(beats & (m < q_blk), 1, 0)
        past = n_iota < q_blk
        chosen = past & (rank < MOBA_TOPK)
        far = chosen & (n_iota < q_blk - 1)

        b_far = jnp.full((nk, w), rb_ref[N_BUCKETS - 1, h] * LOG2E, F32)
        b_hi = b_far.astype(BF16).astype(F32)
        add_hi = jnp.where(past, jnp.where(chosen, jnp.where(far, b_hi, 0.0), NEG), 0.0)
        add_lo = jnp.where(far, b_far - b_hi, 0.0)
        pieces = [add_hi, jnp.zeros((nb - nk, w), F32), add_lo, jnp.zeros((LANES - nb - nk, w), F32)]
        add_t = jnp.concatenate([p for p in pieces if p.shape[0]], axis=0)
        qaug_ref[q_lo:q_lo + w, 0:hd] = q_ref[0, q_lo:q_lo + w, :]
        qaug_ref[q_lo:q_lo + w, hd:hd + LANES] = add_t.T.astype(BF16)

    def logits(i):
        n_keys = (i + 1) * blk
        s = lax.dot_general(qaug_ref[i * blk:(i + 1) * blk, :], kaug_ref[0:n_keys, :], nt,
                            preferred_element_type=F32)
        if i == 0:
            return s + tab_ref[0, :, blk:2 * blk]
        if i == 1:
            return s + tab_ref[0]
        return jnp.concatenate([s[:, :n_keys - 2 * blk], s[:, n_keys - 2 * blk:] + tab_ref[0]], axis=1)

    for g in range(nb // SEL_GROUP):
        select(g)
    s = logits(0)
    for i in range(nb):
        s_next = logits(i + 1) if i + 1 < nb else None
        s16 = s.astype(BF16)
        p = jnp.exp2(s16 - jnp.max(s16, axis=-1, keepdims=True))
        pv = jnp.dot(p, vaug_ref[0:(i + 1) * blk, :], preferred_element_type=F32)
        o_ref[0, i * blk:(i + 1) * blk, :] = (pv[:, 0:hd] / pv[:, hd:2 * hd]).astype(o_ref.dtype)
        s = s_next


def _moba_attention(proj, kmeans, tables, rel_bias, d_model):
    bsz, seq, _ = proj.shape
    nb = seq // MOBA_BLOCK
    nh = d_model // HEAD_DIM
    assert MAX_DISTANCE <= MOBA_BLOCK + 1 and 2 * nb <= LANES and MOBA_BLOCK & (MOBA_BLOCK - 1) == 0
    assert nb % SEL_GROUP == 0 and nb % SUBLANES == 0
    kern = functools.partial(_attn_kernel, nb=nb)

    def head_cols(tile):
        return pl.BlockSpec((1, seq, HEAD_DIM), lambda b, h: (b, 0, tile * nh + h))

    return pl.pallas_call(
        kern,
        grid=(bsz, nh),
        in_specs=[pl.BlockSpec(memory_space=pltpu.SMEM),
                  head_cols(P_Q), head_cols(P_K), head_cols(P_V),
                  pl.BlockSpec((1,) + kmeans.shape[1:3] + (HEAD_DIM,), lambda b, h: (b, 0, 0, h)),
                  pl.BlockSpec((1, MOBA_BLOCK, 2 * MOBA_BLOCK), lambda b, h: (h, 0, 0))],
        out_specs=head_cols(0),
        out_shape=jax.ShapeDtypeStruct((bsz, seq, d_model), BF16),
        scratch_shapes=[pltpu.VMEM((seq, HEAD_DIM + LANES), BF16),
                        pltpu.VMEM((seq, HEAD_DIM + LANES), BF16),
                        pltpu.VMEM((seq, 2 * HEAD_DIM), BF16),
                        pltpu.VMEM((nb, HEAD_DIM), F32)],
        compiler_params=pltpu.CompilerParams(
            dimension_semantics=("arbitrary", "arbitrary"),
            vmem_limit_bytes=VMEM_LIMIT),
    )(rel_bias, proj, proj, proj, kmeans, tables)


def _out_kernel(attn_ref, ga_ref, zc_ref, ma_ref, mc_ref, x_ref, mod_ref, woa_ref, woc_ref, wout_ref,
                fg_ref, o_ref, *, row_chunk):
    def merged_branches(r):
        rows = slice(r * row_chunk, (r + 1) * row_chunk)
        y_conv = jnp.dot(zc_ref[0, rows, :], woc_ref[...], preferred_element_type=F32)
        z_attn = attn_ref[0, rows, :].astype(F32) * ga_ref[0, rows, :].astype(F32)
        y_attn = jnp.dot(z_attn.astype(BF16), woa_ref[...], preferred_element_type=F32)
        merged = (ma_ref[0, rows, :].astype(F32) * y_attn
                  + mc_ref[0, rows, :].astype(F32) * y_conv)
        return merged.astype(BF16)

    def residual_norm(r, branch):
        rows = slice(r * row_chunk, (r + 1) * row_chunk)
        res = x_ref[0, rows, :] + mod_ref[0, 2:3, :] * branch
        inv = lax.rsqrt(jnp.mean(res * res, axis=-1, keepdims=True) + EPS)
        o_ref[0, rows, :] = res * inv * fg_ref[...]

    n_chunks = x_ref.shape[1] // row_chunk
    merged = merged_branches(0)
    for r in range(n_chunks):
        branch = jnp.dot(merged, wout_ref[...], preferred_element_type=F32)
        merged = merged_branches(r + 1) if r + 1 < n_chunks else None
        residual_norm(r, branch)


def _output_stage(attn, proj, x, mod3, woa, woc, wout, final_g, ts=1024, row_chunk=256):
    bsz, seq, d = x.shape
    assert ts % row_chunk == 0

    def col(k):
        return pl.BlockSpec((1, ts, d), lambda b, s: (b, s, k))

    def whole(shape):
        return pl.BlockSpec(shape, lambda b, s: (0,) * len(shape), pipeline_mode=pl.Buffered(1))

    return pl.pallas_call(
        functools.partial(_out_kernel, row_chunk=row_chunk),
        grid=(bsz, seq // ts),
        in_specs=[col(0), col(P_GA), col(P_ZC), col(P_MA), col(P_MC),
                  col(0), pl.BlockSpec((1, 3, d), lambda b, s: (b, 0, 0)),
                  whole((d, d)), whole((d, d)), whole((d, d)), whole((1, d))],
        out_specs=col(0),
        out_shape=jax.ShapeDtypeStruct((bsz, seq, d), F32),
        compiler_params=pltpu.CompilerParams(
            dimension_semantics=("arbitrary", "arbitrary"),
            vmem_limit_bytes=VMEM_LIMIT),
    )(attn, proj, proj, proj, proj, x, mod3, woa, woc, wout, final_g.reshape(1, d))


def kernel(x, c, norm_g, w_ada, b_ada, w_in, conv_w, w_o_attn, w_o_conv, w_out, rel_bias, final_g):
    bsz, seq, d = x.shape
    depth = norm_g.shape[0]
    assert depth == 1, "the fused output stage applies the final RMSNorm after the single layer"
    assert d == N_HEADS * HEAD_DIM and w_in.shape[2] == N_PROJ * d and seq % MOBA_BLOCK == 0
    tables = _bias_tables(rel_bias)
    mod3 = _modulation(c, w_ada[0], b_ada[0]).reshape(bsz, 3, d)
    proj, woa, woc, wout, kmeans = _in_projection(x, mod3, norm_g[0], conv_w[0], w_in[0],
                                                  (w_o_attn[0], w_o_conv[0], w_out[0]))
    attn = _moba_attention(proj, kmeans, tables, rel_bias, d)
    return _output_stage(attn, proj, x, mod3, woa, woc, wout, final_g)
```

```python
import functools
import math

import jax
import jax.numpy as jnp
from jax import lax
from jax.experimental import pallas as pl
from jax.experimental.pallas import tpu as pltpu

N_HEADS = 8
HEAD_DIM = 128
CONV_K = 3
MOBA_BLOCK = 256
MOBA_TOPK = 3
N_BUCKETS = 32
MAX_DISTANCE = 128
EPS = 1e-6
N_PROJ = 10

LANES = 128
SUBLANES = 8
BF16_SUBLANES = 16
SEL_GROUP = 4
NEG = -1e30
LOG2E = math.log2(math.e)
VMEM_LIMIT = 56 * 1024 * 1024

F32 = jnp.float32
BF16 = jnp.bfloat16


def _sigmoid(v):
    return 0.5 * jnp.tanh(0.5 * v) + 0.5


def _silu(v):
    return v * _sigmoid(v)


def _split_bf16(v, terms):
    parts = []
    for _ in range(terms - 1):
        hi = v.astype(BF16).astype(F32)
        parts.append(hi)
        v = v - hi
    return parts + [v]


def _mod_kernel(c_ref, w_ref, b_ref, o_ref):
    bsz = c_ref.shape[0]
    a3 = jnp.concatenate(_split_bf16(_silu(c_ref[...]), 3), axis=0).astype(BF16)
    w_hi, w_lo = _split_bf16(w_ref[...], 2)
    r = (jnp.dot(a3, w_hi.astype(BF16), preferred_element_type=F32)
         + jnp.dot(a3, w_lo.astype(BF16), preferred_element_type=F32))
    o_ref[...] = (r[0:bsz] + r[bsz:2 * bsz]) + r[2 * bsz:3 * bsz] + b_ref[...]


def _modulation(c, w_ada, b_ada):
    bsz, d = c.shape
    n = w_ada.shape[1]
    tn = d
    return pl.pallas_call(
        _mod_kernel,
        grid=(n // tn,),
        in_specs=[pl.BlockSpec((bsz, d), lambda j: (0, 0)),
                  pl.BlockSpec((d, tn), lambda j: (0, j)),
                  pl.BlockSpec((1, tn), lambda j: (0, j))],
        out_specs=pl.BlockSpec((bsz, tn), lambda j: (0, j)),
        out_shape=jax.ShapeDtypeStruct((bsz, n), F32),
        compiler_params=pltpu.CompilerParams(vmem_limit_bytes=VMEM_LIMIT),
    )(c, w_ada, b_ada.reshape(1, n))


W_Q, W_K, W_V, W_GA, W_CB, W_CC, W_CX, W_GC, W_MA, W_MC = range(N_PROJ)
N_FUSED = 7
P_Q, P_K, P_V, P_GA, P_ZC, P_MA, P_MC = range(N_FUSED)
CONV_HALO = 8


def _inproj_kernel(x0_ref, xn_ref, mod0_ref, modn_ref, g_ref, cw_ref, w_hbm, *rest, q_scale,
                   row_chunk, n_late):
    late_in, o_ref, late_out = rest[:n_late], rest[n_late], rest[n_late + 1:2 * n_late + 1]
    kmean_o_ref = rest[2 * n_late + 1]
    h_ref, u_ref, t_ref, w_ref, stage_ref, sem = rest[2 * n_late + 2:]
    for src, dst in zip(late_in, late_out):
        dst[...] = src[...].astype(dst.dtype)
    s = pl.program_id(1)
    step = pl.program_id(0) * pl.num_programs(1) + s
    slot = lax.rem(step, 2)
    ts, d = xn_ref.shape[1], xn_ref.shape[2]
    n_chunks = ts // row_chunk
    g = g_ref[...]

    def normalise(x_ref, mod_ref, dst_slot, r):
        rows = slice(r * row_chunk, (r + 1) * row_chunk)
        xr = x_ref[0, rows, :]
        inv = lax.rsqrt(jnp.mean(xr * xr, axis=-1, keepdims=True) + EPS)
        hr = (xr * inv * g) * (1.0 + mod_ref[0, 1:2, :]) + mod_ref[0, 0:1, :]
        h_ref[dst_slot, rows, :] = hr.astype(BF16)

    depth, _, width = stage_ref.shape

    def weight_chunk_copy(c):
        return pltpu.make_async_copy(w_hbm.at[:, c * width:(c + 1) * width],
                                     stage_ref.at[c % depth], sem.at[c % depth])

    @pl.when(step == 0)
    def _():
        n_w = w_ref.shape[1] // width
        for c in range(min(depth, n_w)):
            weight_chunk_copy(c).start()
        for c in range(n_w):
            weight_chunk_copy(c).wait()
            w_ref[:, c * width:(c + 1) * width] = stage_ref[c % depth].astype(BF16)
            if c + depth < n_w:
                weight_chunk_copy(c + depth).start()
        for r in range(n_chunks):
            normalise(x0_ref, mod0_ref, 0, r)

    def proj(j):
        return jnp.dot(h_ref[slot], w_ref[:, j * d:(j + 1) * d], preferred_element_type=F32)

    def put(k, val):
        o_ref[0, :, k * d:(k + 1) * d] = val.astype(o_ref.dtype)

    @pl.when(s == 0)
    def _():
        u_ref[0:CONV_HALO, :] = jnp.zeros((CONV_HALO, d), F32)

    def keep_cc(acc):
        t_ref[...] = acc

    def conv(acc):
        u_ref[CONV_HALO:CONV_HALO + ts, :] = t_ref[...] * acc
        t_ref[...] = (cw_ref[0:1, :] * u_ref[CONV_HALO - 2:CONV_HALO - 2 + ts, :]
                      + cw_ref[1:2, :] * u_ref[CONV_HALO - 1:CONV_HALO - 1 + ts, :]
                      + cw_ref[2:3, :] * u_ref[CONV_HALO:CONV_HALO + ts, :])
        u_ref[0:CONV_HALO, :] = u_ref[ts:ts + CONV_HALO, :]

    def gate_cb(acc):
        t_ref[...] = t_ref[...] * acc

    def put_keys(acc):
        put(P_K, acc)
        for j in range(ts // MOBA_BLOCK):
            kmean_o_ref[0, 0, j:j + 1, :] = jnp.mean(acc[j * MOBA_BLOCK:(j + 1) * MOBA_BLOCK], axis=0,
                                                     keepdims=True)

    stages = [
        (W_GA, lambda acc: put(P_GA, _silu(acc))),
        (W_MA, lambda acc: put(P_MA, _sigmoid(acc))),
        (W_MC, lambda acc: put(P_MC, _sigmoid(acc))),
        (W_CC, keep_cc),
        (W_CX, conv),
        (W_CB, gate_cb),
        (W_GC, lambda acc: put(P_ZC, t_ref[...] * _silu(acc))),
        (W_Q, lambda acc: put(P_Q, acc * q_scale)),
        (W_K, put_keys),
        (W_V, lambda acc: put(P_V, acc)),
    ]
    norm_after = list(range(len(stages) - 1 - n_chunks, len(stages) - 1))
    acc = proj(stages[0][0])
    for n, (_, epilogue) in enumerate(stages):
        nxt = proj(stages[n + 1][0]) if n + 1 < len(stages) else None
        epilogue(acc)
        if n in norm_after:
            normalise(xn_ref, modn_ref, 1 - slot, norm_after.index(n))
        acc = nxt


def _in_projection(x, mod3, norm_g, conv_w, w_in, late_weights, ts=512, row_chunk=128,
                   w_chunk=256, w_depth=4):
    bsz, seq, d = x.shape
    n = w_in.shape[1]
    assert n % w_chunk == 0 and w_chunk % LANES == 0
    n_out = N_FUSED * d
    n_s = seq // ts
    n_steps = bsz * n_s
    assert ts // row_chunk <= N_PROJ
    slab = d // n_steps
    assert slab * n_steps == d and slab % BF16_SUBLANES == 0
    assert all(w.shape == (d, d) for w in late_weights)
    kern = functools.partial(_inproj_kernel, q_scale=LOG2E * HEAD_DIM ** -0.5, row_chunk=row_chunk,
                             n_late=len(late_weights))

    def next_tile(b, s):
        flat = jnp.minimum(b * n_s + s + 1, n_steps - 1)
        return flat // n_s, flat % n_s

    slab_spec = pl.BlockSpec((slab, d), lambda b, s: (b * n_s + s, 0))
    return pl.pallas_call(
        kern,
        grid=(bsz, n_s),
        in_specs=[pl.BlockSpec((1, ts, d), lambda b, s: (0, 0, 0), pipeline_mode=pl.Buffered(1)),
                  pl.BlockSpec((1, ts, d), lambda b, s: (*next_tile(b, s), 0)),
                  pl.BlockSpec((1, 3, d), lambda b, s: (0, 0, 0)),
                  pl.BlockSpec((1, 3, d), lambda b, s: (next_tile(b, s)[0], 0, 0)),
                  pl.BlockSpec((1, d), lambda b, s: (0, 0)),
                  pl.BlockSpec((CONV_K, d), lambda b, s: (0, 0)),
                  pl.BlockSpec(memory_space=pl.ANY)]
                 + [slab_spec] * len(late_weights),
        out_specs=[pl.BlockSpec((1, ts, n_out), lambda b, s: (b, s, 0))]
                  + [slab_spec] * len(late_weights)
                  + [pl.BlockSpec((1, 1, ts // MOBA_BLOCK, d), lambda b, s: (b, s, 0, 0))],
        out_shape=[jax.ShapeDtypeStruct((bsz, seq, n_out), BF16)]
                  + [jax.ShapeDtypeStruct((d, d), BF16)] * len(late_weights)
                  + [jax.ShapeDtypeStruct((bsz, n_s, ts // MOBA_BLOCK, d), F32)],
        scratch_shapes=[pltpu.VMEM((2, ts, d), BF16), pltpu.VMEM((CONV_HALO + ts, d), F32),
                        pltpu.VMEM((ts, d), F32), pltpu.VMEM((d, n), BF16),
                        pltpu.VMEM((w_depth, d, w_chunk), F32),
                        pltpu.SemaphoreType.DMA((w_depth,))],
        compiler_params=pltpu.CompilerParams(
            dimension_semantics=("arbitrary", "arbitrary"),
            vmem_limit_bytes=VMEM_LIMIT),
    )(x, x, mod3, mod3, norm_g.reshape(1, d), conv_w, w_in, *late_weights)


def _t5_bucket(dist):
    n = jnp.maximum(dist, 0)
    max_exact = N_BUCKETS // 2
    nf = jnp.maximum(n, 1).astype(F32)
    large = max_exact + (jnp.log(nf / max_exact) / math.log(MAX_DISTANCE / max_exact)
                         * (N_BUCKETS - max_exact)).astype(jnp.int32)
    large = jnp.minimum(large, N_BUCKETS - 1)
    return jnp.where(n < max_exact, n, large)


def _table_kernel(rb_ref, o_ref):
    n_heads, blk, width = o_ref.shape
    u = lax.broadcasted_iota(jnp.int32, (SUBLANES, width), 1)
    for t in range(2):
        dist = jnp.where(u < blk, t * blk - u, t * blk + width - u)
        bucket = _t5_bucket(dist)
        for h in range(n_heads):
            g = jnp.zeros((SUBLANES, width), F32)
            for b in range(N_BUCKETS):
                g = jnp.where(bucket == b, rb_ref[b, h] * LOG2E, g)
            g = jnp.where(dist >= 0, g, NEG)
            skewed = pltpu.roll(jnp.broadcast_to(g[0:1, :], (blk, width)), 0, 1,
                                stride=1, stride_axis=0)
            o_ref[h, :, (1 - t) * blk:(2 - t) * blk] = skewed[:, 0:blk]


def _bias_tables(rel_bias):
    shape = (N_HEADS, MOBA_BLOCK, 2 * MOBA_BLOCK)
    return pl.pallas_call(
        _table_kernel,
        grid=(1,),
        in_specs=[pl.BlockSpec(memory_space=pltpu.SMEM)],
        out_specs=pl.BlockSpec(shape, lambda i: (0, 0, 0)),
        out_shape=jax.ShapeDtypeStruct(shape, F32),
        compiler_params=pltpu.CompilerParams(vmem_limit_bytes=VMEM_LIMIT),
    )(rel_bias)


def _attn_kernel(rb_ref, q_ref, k_ref, v_ref, km_ref, tab_ref, o_ref, qaug_ref, kaug_ref, vaug_ref,
                 kmean_ref, *, nb):
    h = pl.program_id(1)
    blk = MOBA_BLOCK
    hd = HEAD_DIM
    seq = nb * blk
    nt = (((1,), (1,)), ((), ()))

    @pl.when((pl.program_id(0) == 0) & (h == 0))
    def _():
        lane = lax.broadcasted_iota(jnp.int32, (blk, LANES), 1)
        for n in range(nb):
            onehot = jnp.where((lane == n) | (lane == nb + n), 1.0, 0.0)
            kaug_ref[n * blk:(n + 1) * blk, hd:hd + LANES] = onehot.astype(BF16)
        vaug_ref[:, hd:2 * hd] = jnp.ones((seq, hd), BF16)

    per_tile = km_ref.shape[2]
    for t in range(nb // per_tile):
        kmean_ref[t * per_tile:(t + 1) * per_tile, :] = km_ref[0, t]

    def select(g):
        q_lo = g * SEL_GROUP * blk
        w = SEL_GROUP * blk
        last_blk = (g + 1) * SEL_GROUP - 1
        nk = min(nb, -(-(last_blk + 1) // SUBLANES) * SUBLANES)

        km3 = jnp.concatenate(_split_bf16(kmean_ref[0:nk, :], 3), axis=0).astype(BF16)
        sc3 = lax.dot_general(km3, q_ref[0, q_lo:q_lo + w, :], nt, preferred_element_type=F32)
        sc = (sc3[0:nk] + sc3[nk:2 * nk]) + sc3[2 * nk:3 * nk]

        n_iota = lax.broadcasted_iota(jnp.int32, (nk, w), 0)
        q_blk = g * SEL_GROUP + lax.shift_right_logical(
            lax.broadcasted_iota(jnp.int32, (nk, w), 1), int(math.log2(blk)))
        rank = jnp.zeros((nk, w), jnp.int32)
        for m in range(last_blk):
            row = sc[m:m + 1, :]
            beats = (row > sc) | ((row == sc) & (m < n_iota))
            rank = rank + jnp.where(beats & (m < q_blk), 1, 0)
        past = n_iota < q_blk
        chosen = past & (rank < MOBA_TOPK)
        far = chosen & (n_iota < q_blk - 1)

        b_far = jnp.full((nk, w), rb_ref[N_BUCKETS - 1, h] * LOG2E, F32)
        b_hi = b_far.astype(BF16).astype(F32)
        add_hi = jnp.where(past, jnp.where(chosen, jnp.where(far, b_hi, 0.0), NEG), 0.0)
        add_lo = jnp.where(far, b_far - b_hi, 0.0)
        pieces = [add_hi, jnp.zeros((nb - nk, w), F32), add_lo, jnp.zeros((LANES - nb - nk, w), F32)]
        add_t = jnp.concatenate([p for p in pieces if p.shape[0]], axis=0)
        qaug_ref[q_lo:q_lo + w, 0:hd] = q_ref[0, q_lo:q_lo + w, :]
        qaug_ref[q_lo:q_lo + w, hd:hd + LANES] = add_t.T.astype(BF16)

    def logits(i):
        rows = slice(i * blk, (i + 1) * blk)
        kaug_ref[rows, 0:hd] = k_ref[0, rows, :]
        vaug_ref[rows, 0:hd] = v_ref[0, rows, :]
        n_keys = (i + 1) * blk
        s = lax.dot_general(qaug_ref[i * blk:(i + 1) * blk, :], kaug_ref[0:n_keys, :], nt,
                            preferred_element_type=F32)
        if i == 0:
            return s + tab_ref[0, :, blk:2 * blk]
        if i == 1:
            return s + tab_ref[0]
        return jnp.concatenate([s[:, :n_keys - 2 * blk], s[:, n_keys - 2 * blk:] + tab_ref[0]], axis=1)

    for g in range(nb // SEL_GROUP):
        select(g)
    s = logits(0)
    for i in range(nb):
        s_next = logits(i + 1) if i + 1 < nb else None
        s16 = s.astype(BF16)
        p = jnp.exp2(s16 - jnp.max(s16, axis=-1, keepdims=True))
        pv = jnp.dot(p, vaug_ref[0:(i + 1) * blk, :], preferred_element_type=F32)
        o_ref[0, i * blk:(i + 1) * blk, :] = (pv[:, 0:hd] / pv[:, hd:2 * hd]).astype(o_ref.dtype)
        s = s_next


def _moba_attention(proj, kmeans, tables, rel_bias, d_model):
    bsz, seq, _ = proj.shape
    nb = seq // MOBA_BLOCK
    nh = d_model // HEAD_DIM
    assert MAX_DISTANCE <= MOBA_BLOCK + 1 and 2 * nb <= LANES and MOBA_BLOCK & (MOBA_BLOCK - 1) == 0
    assert nb % SEL_GROUP == 0 and nb % SUBLANES == 0
    kern = functools.partial(_attn_kernel, nb=nb)

    def head_cols(tile):
        return pl.BlockSpec((1, seq, HEAD_DIM), lambda b, h: (b, 0, tile * nh + h))

    return pl.pallas_call(
        kern,
        grid=(bsz, nh),
        in_specs=[pl.BlockSpec(memory_space=pltpu.SMEM),
                  head_cols(P_Q), head_cols(P_K), head_cols(P_V),
                  pl.BlockSpec((1,) + kmeans.shape[1:3] + (HEAD_DIM,), lambda b, h: (b, 0, 0, h)),
                  pl.BlockSpec((1, MOBA_BLOCK, 2 * MOBA_BLOCK), lambda b, h: (h, 0, 0))],
        out_specs=head_cols(0),
        out_shape=jax.ShapeDtypeStruct((bsz, seq, d_model), BF16),
        scratch_shapes=[pltpu.VMEM((seq, HEAD_DIM + LANES), BF16),
                        pltpu.VMEM((seq, HEAD_DIM + LANES), BF16),
                        pltpu.VMEM((seq, 2 * HEAD_DIM), BF16),
                        pltpu.VMEM((nb, HEAD_DIM), F32)],
        compiler_params=pltpu.CompilerParams(
            dimension_semantics=("arbitrary", "arbitrary"),
            vmem_limit_bytes=VMEM_LIMIT),
    )(rel_bias, proj, proj, proj, kmeans, tables)


def _out_kernel(attn_ref, ga_ref, zc_ref, ma_ref, mc_ref, x_ref, mod_ref, woa_ref, woc_ref, wout_ref,
                fg_ref, o_ref, *, row_chunk):
    def merged_branches(r):
        rows = slice(r * row_chunk, (r + 1) * row_chunk)
        y_conv = jnp.dot(zc_ref[0, rows, :], woc_ref[...], preferred_element_type=F32)
        z_attn = attn_ref[0, rows, :].astype(F32) * ga_ref[0, rows, :].astype(F32)
        y_attn = jnp.dot(z_attn.astype(BF16), woa_ref[...], preferred_element_type=F32)
        merged = (ma_ref[0, rows, :].astype(F32) * y_attn
                  + mc_ref[0, rows, :].astype(F32) * y_conv)
        return merged.astype(BF16)

    def residual_norm(r, branch):
        rows = slice(r * row_chunk, (r + 1) * row_chunk)
        res = x_ref[0, rows, :] + mod_ref[0, 2:3, :] * branch
        inv = lax.rsqrt(jnp.mean(res * res, axis=-1, keepdims=True) + EPS)
        o_ref[0, rows, :] = res * inv * fg_ref[...]

    n_chunks = x_ref.shape[1] // row_chunk
    merged = merged_branches(0)
    for r in range(n_chunks):
        branch = jnp.dot(merged, wout_ref[...], preferred_element_type=F32)
        merged = merged_branches(r + 1) if r + 1 < n_chunks else None
        residual_norm(r, branch)


def _output_stage(attn, proj, x, mod3, woa, woc, wout, final_g, ts=1024, row_chunk=256):
    bsz, seq, d = x.shape
    assert ts % row_chunk == 0

    def col(k):
        return pl.BlockSpec((1, ts, d), lambda b, s: (b, s, k))

    def whole(shape):
        return pl.BlockSpec(shape, lambda b, s: (0,) * len(shape), pipeline_mode=pl.Buffered(1))

    return pl.pallas_call(
        functools.partial(_out_kernel, row_chunk=row_chunk),
        grid=(bsz, seq // ts),
        in_specs=[col(0), col(P_GA), col(P_ZC), col(P_MA), col(P_MC),
                  col(0), pl.BlockSpec((1, 3, d), lambda b, s: (b, 0, 0)),
                  whole((d, d)), whole((d, d)), whole((d, d)), whole((1, d))],
        out_specs=col(0),
        out_shape=jax.ShapeDtypeStruct((bsz, seq, d), F32),
        compiler_params=pltpu.CompilerParams(
            dimension_semantics=("arbitrary", "arbitrary"),
            vmem_limit_bytes=VMEM_LIMIT),
    )(attn, proj, proj, proj, proj, x, mod3, woa, woc, wout, final_g.reshape(1, d))


def kernel(x, c, norm_g, w_ada, b_ada, w_in, conv_w, w_o_attn, w_o_conv, w_out, rel_bias, final_g):
    bsz, seq, d = x.shape
    depth = norm_g.shape[0]
    assert depth == 1, "the fused output stage applies the final RMSNorm after the single layer"
    assert d == N_HEADS * HEAD_DIM and w_in.shape[2] == N_PROJ * d and seq % MOBA_BLOCK == 0
    tables = _bias_tables(rel_bias)
    mod3 = _modulation(c, w_ada[0], b_ada[0]).reshape(bsz, 3, d)
    proj, woa, woc, wout, kmeans = _in_projection(x, mod3, norm_g[0], conv_w[0], w_in[0],
                                                  (w_o_attn[0], w_o_conv[0], w_out[0]))
    attn = _moba_attention(proj, kmeans, tables, rel_bias, d)
    return _output_stage(attn, proj, x, mod3, woa, woc, wout, final_g)
```

```python
import functools
import math

import jax
import jax.numpy as jnp
from jax import lax
from jax.experimental import pallas as pl
from jax.experimental.pallas import tpu as pltpu

N_HEADS = 8
HEAD_DIM = 128
CONV_K = 3
MOBA_BLOCK = 256
MOBA_TOPK = 3
N_BUCKETS = 32
MAX_DISTANCE = 128
EPS = 1e-6
N_PROJ = 10

LANES = 128
SUBLANES = 8
BF16_SUBLANES = 16
SEL_GROUP = 4
NEG = -1e30
LOG2E = math.log2(math.e)
VMEM_LIMIT = 56 * 1024 * 1024

F32 = jnp.float32
BF16 = jnp.bfloat16


def _sigmoid(v):
    return 0.5 * jnp.tanh(0.5 * v) + 0.5


def _silu(v):
    return v * _sigmoid(v)


def _split_bf16(v, terms):
    parts = []
    for _ in range(terms - 1):
        hi = v.astype(BF16).astype(F32)
        parts.append(hi)
        v = v - hi
    return parts + [v]


def _mod_kernel(c_ref, w_ref, b_ref, o_ref):
    bsz = c_ref.shape[0]
    a3 = jnp.concatenate(_split_bf16(_silu(c_ref[...]), 3), axis=0).astype(BF16)
    w_hi, w_lo = _split_bf16(w_ref[...], 2)
    r = (jnp.dot(a3, w_hi.astype(BF16), preferred_element_type=F32)
         + jnp.dot(a3, w_lo.astype(BF16), preferred_element_type=F32))
    o_ref[...] = (r[0:bsz] + r[bsz:2 * bsz]) + r[2 * bsz:3 * bsz] + b_ref[...]


def _modulation(c, w_ada, b_ada):
    bsz, d = c.shape
    n = w_ada.shape[1]
    tn = d
    return pl.pallas_call(
        _mod_kernel,
        grid=(n // tn,),
        in_specs=[pl.BlockSpec((bsz, d), lambda j: (0, 0)),
                  pl.BlockSpec((d, tn), lambda j: (0, j)),
                  pl.BlockSpec((1, tn), lambda j: (0, j))],
        out_specs=pl.BlockSpec((bsz, tn), lambda j: (0, j)),
        out_shape=jax.ShapeDtypeStruct((bsz, n), F32),
        compiler_params=pltpu.CompilerParams(vmem_limit_bytes=VMEM_LIMIT),
    )(c, w_ada, b_ada.reshape(1, n))


W_Q, W_K, W_V, W_GA, W_CB, W_CC, W_CX, W_GC, W_MA, W_MC = range(N_PROJ)
N_FUSED = 7
P_Q, P_K, P_V, P_GA, P_ZC, P_MA, P_MC = range(N_FUSED)
CONV_HALO = 8


def _inproj_kernel(x0_ref, xn_ref, mod0_ref, modn_ref, g_ref, cw_ref, w_hbm, *rest, q_scale,
                   row_chunk, n_late):
    late_in, o_ref, late_out = rest[:n_late], rest[n_late], rest[n_late + 1:2 * n_late + 1]
    kmean_o_ref = rest[2 * n_late + 1]
    h_ref, u_ref, t_ref, w_ref, stage_ref, sem = rest[2 * n_late + 2:]
    for src, dst in zip(late_in, late_out):
        dst[...] = src[...].astype(dst.dtype)
    s = pl.program_id(1)
    step = pl.program_id(0) * pl.num_programs(1) + s
    slot = lax.rem(step, 2)
    ts, d = xn_ref.shape[1], xn_ref.shape[2]
    n_chunks = ts // row_chunk
    g = g_ref[...]

    def normalise(x_ref, mod_ref, dst_slot, r):
        rows = slice(r * row_chunk, (r + 1) * row_chunk)
        xr = x_ref[0, rows, :]
        inv = lax.rsqrt(jnp.mean(xr * xr, axis=-1, keepdims=True) + EPS)
        hr = (xr * inv * g) * (1.0 + mod_ref[0, 1:2, :]) + mod_ref[0, 0:1, :]
        h_ref[dst_slot, rows, :] = hr.astype(BF16)

    depth, _, width = stage_ref.shape

    def weight_chunk_copy(c):
        return pltpu.make_async_copy(w_hbm.at[:, c * width:(c + 1) * width],
                                     stage_ref.at[c % depth], sem.at[c % depth])

    @pl.when(step == 0)
    def _():
        n_w = w_ref.shape[1] // width
        for c in range(min(depth, n_w)):
            weight_chunk_copy(c).start()
        for c in range(n_w):
            weight_chunk_copy(c).wait()
            w_ref[:, c * width:(c + 1) * width] = stage_ref[c % depth].astype(BF16)
            if c + depth < n_w:
                weight_chunk_copy(c + depth).start()
        for r in range(n_chunks):
            normalise(x0_ref, mod0_ref, 0, r)

    def proj(j):
        return jnp.dot(h_ref[slot], w_ref[:, j * d:(j + 1) * d], preferred_element_type=F32)

    def put(k, val):
        o_ref[0, :, k * d:(k + 1) * d] = val.astype(o_ref.dtype)

    @pl.when(s == 0)
    def _():
        u_ref[0:CONV_HALO, :] = jnp.zeros((CONV_HALO, d), F32)

    def keep_cc(acc):
        t_ref[...] = acc

    def conv(acc):
        u_ref[CONV_HALO:CONV_HALO + ts, :] = t_ref[...] * acc
        t_ref[...] = (cw_ref[0:1, :] * u_ref[CONV_HALO - 2:CONV_HALO - 2 + ts, :]
                      + cw_ref[1:2, :] * u_ref[CONV_HALO - 1:CONV_HALO - 1 + ts, :]
                      + cw_ref[2:3, :] * u_ref[CONV_HALO:CONV_HALO + ts, :])
        u_ref[0:CONV_HALO, :] = u_ref[ts:ts + CONV_HALO, :]

    def gate_cb(acc):
        t_ref[...] = t_ref[...] * acc

    def put_keys(acc):
        put(P_K, acc)
        for j in range(ts // MOBA_BLOCK):
            kmean_o_ref[0, 0, j:j + 1, :] = jnp.mean(acc[j * MOBA_BLOCK:(j + 1) * MOBA_BLOCK], axis=0,
                                                     keepdims=True)

    stages = [
        (W_GA, lambda acc: put(P_GA, _silu(acc))),
        (W_MA, lambda acc: put(P_MA, _sigmoid(acc))),
        (W_MC, lambda acc: put(P_MC, _sigmoid(acc))),
        (W_CC, keep_cc),
        (W_CX, conv),
        (W_CB, gate_cb),
        (W_GC, lambda acc: put(P_ZC, t_ref[...] * _silu(acc))),
        (W_Q, lambda acc: put(P_Q, acc * q_scale)),
        (W_K, put_keys),
        (W_V, lambda acc: put(P_V, acc)),
    ]
    norm_after = list(range(len(stages) - 1 - n_chunks, len(stages) - 1))
    acc = proj(stages[0][0])
    for n, (_, epilogue) in enumerate(stages):
        nxt = proj(stages[n + 1][0]) if n + 1 < len(stages) else None
        epilogue(acc)
        if n in norm_after:
            normalise(xn_ref, modn_ref, 1 - slot, norm_after.index(n))
        acc = nxt


def _in_projection(x, mod3, norm_g, conv_w, w_in, late_weights, ts=512, row_chunk=128,
                   w_chunk=256, w_depth=4):
    bsz, seq, d = x.shape
    n = w_in.shape[1]
    assert n % w_chunk == 0 and w_chunk % LANES == 0
    n_out = N_FUSED * d
    n_s = seq // ts
    n_steps = bsz * n_s
    assert ts // row_chunk <= N_PROJ
    slab = d // n_steps
    assert slab * n_steps == d and slab % BF16_SUBLANES == 0
    assert all(w.shape == (d, d) for w in late_weights)
    kern = functools.partial(_inproj_kernel, q_scale=LOG2E * HEAD_DIM ** -0.5, row_chunk=row_chunk,
                             n_late=len(late_weights))

    def next_tile(b, s):
        flat = jnp.minimum(b * n_s + s + 1, n_steps - 1)
        return flat // n_s, flat % n_s

    slab_spec = pl.BlockSpec((slab, d), lambda b, s: (b * n_s + s, 0))
    return pl.pallas_call(
        kern,
        grid=(bsz, n_s),
        in_specs=[pl.BlockSpec((1, ts, d), lambda b, s: (0, 0, 0), pipeline_mode=pl.Buffered(1)),
                  pl.BlockSpec((1, ts, d), lambda b, s: (*next_tile(b, s), 0)),
                  pl.BlockSpec((1, 3, d), lambda b, s: (0, 0, 0)),
                  pl.BlockSpec((1, 3, d), lambda b, s: (next_tile(b, s)[0], 0, 0)),
                  pl.BlockSpec((1, d), lambda b, s: (0, 0)),
                  pl.BlockSpec((CONV_K, d), lambda b, s: (0, 0)),
                  pl.BlockSpec(memory_space=pl.ANY)]
                 + [slab_spec] * len(late_weights),
        out_specs=[pl.BlockSpec((1, ts, n_out), lambda b, s: (b, s, 0))]
                  + [slab_spec] * len(late_weights)
                  + [pl.BlockSpec((1, 1, ts // MOBA_BLOCK, d), lambda b, s: (b, s, 0, 0))],
        out_shape=[jax.ShapeDtypeStruct((bsz, seq, n_out), BF16)]
                  + [jax.ShapeDtypeStruct((d, d), BF16)] * len(late_weights)
                  + [jax.ShapeDtypeStruct((bsz, n_s, ts // MOBA_BLOCK, d), F32)],
        scratch_shapes=[pltpu.VMEM((2, ts, d), BF16), pltpu.VMEM((CONV_HALO + ts, d), F32),
                        pltpu.VMEM((ts, d), F32), pltpu.VMEM((d, n), BF16),
                        pltpu.VMEM((w_depth, d, w_chunk), F32),
                        pltpu.SemaphoreType.DMA((w_depth,))],
        compiler_params=pltpu.CompilerParams(
            dimension_semantics=("arbitrary", "arbitrary"),
            vmem_limit_bytes=VMEM_LIMIT),
    )(x, x, mod3, mod3, norm_g.reshape(1, d), conv_w, w_in, *late_weights)


def _t5_bucket(dist):
    n = jnp.maximum(dist, 0)
    max_exact = N_BUCKETS // 2
    nf = jnp.maximum(n, 1).astype(F32)
    large = max_exact + (jnp.log(nf / max_exact) / math.log(MAX_DISTANCE / max_exact)
                         * (N_BUCKETS - max_exact)).astype(jnp.int32)
    large = jnp.minimum(large, N_BUCKETS - 1)
    return jnp.where(n < max_exact, n, large)


def _table_kernel(rb_ref, o_ref):
    n_heads, blk, width = o_ref.shape
    u = lax.broadcasted_iota(jnp.int32, (SUBLANES, width), 1)
    for t in range(2):
        dist = jnp.where(u < blk, t * blk - u, t * blk + width - u)
        bucket = _t5_bucket(dist)
        for h in range(n_heads):
            g = jnp.zeros((SUBLANES, width), F32)
            for b in range(N_BUCKETS):
                g = jnp.where(bucket == b, rb_ref[b, h] * LOG2E, g)
            g = jnp.where(dist >= 0, g, NEG)
            skewed = pltpu.roll(jnp.broadcast_to(g[0:1, :], (blk, width)), 0, 1,
                                stride=1, stride_axis=0)
            o_ref[h, :, (1 - t) * blk:(2 - t) * blk] = skewed[:, 0:blk]


def _bias_tables(rel_bias):
    shape = (N_HEADS, MOBA_BLOCK, 2 * MOBA_BLOCK)
    return pl.pallas_call(
        _table_kernel,
        grid=(1,),
        in_specs=[pl.BlockSpec(memory_space=pltpu.SMEM)],
        out_specs=pl.BlockSpec(shape, lambda i: (0, 0, 0)),
        out_shape=jax.ShapeDtypeStruct(shape, F32),
        compiler_params=pltpu.CompilerParams(vmem_limit_bytes=VMEM_LIMIT),
    )(rel_bias)


def _attn_kernel(rb_ref, q_ref, k_ref, v_ref, km_ref, tab_ref, o_ref, qaug_ref, kaug_ref, vaug_ref,
                 kmean_ref, *, nb):
    h = pl.program_id(1)
    blk = MOBA_BLOCK
    hd = HEAD_DIM
    seq = nb * blk
    nt = (((1,), (1,)), ((), ()))

    @pl.when((pl.program_id(0) == 0) & (h == 0))
    def _():
        lane = lax.broadcasted_iota(jnp.int32, (blk, LANES), 1)
        for n in range(nb):
            onehot = jnp.where((lane == n) | (lane == nb + n), 1.0, 0.0)
            kaug_ref[n * blk:(n + 1) * blk, hd:hd + LANES] = onehot.astype(BF16)
        vaug_ref[:, hd:2 * hd] = jnp.ones((seq, hd), BF16)

    per_tile = km_ref.shape[2]
    for t in range(nb // per_tile):
        kmean_ref[t * per_tile:(t + 1) * per_tile, :] = km_ref[0, t]

    def select(g):
        q_lo = g * SEL_GROUP * blk
        w = SEL_GROUP * blk
        last_blk = (g + 1) * SEL_GROUP - 1
        nk = min(nb, -(-(last_blk + 1) // SUBLANES) * SUBLANES)

        km3 = jnp.concatenate(_split_bf16(kmean_ref[0:nk, :], 3), axis=0).astype(BF16)
        sc3 = lax.dot_general(km3, q_ref[0, q_lo:q_lo + w, :], nt, preferred_element_type=F32)
        sc = (sc3[0:nk] + sc3[nk:2 * nk]) + sc3[2 * nk:3 * nk]

        n_iota = lax.broadcasted_iota(jnp.int32, (nk, w), 0)
        q_blk = g * SEL_GROUP + lax.shift_right_logical(
            lax.broadcasted_iota(jnp.int32, (nk, w), 1), int(math.log2(blk)))
        rank = jnp.zeros((nk, w), jnp.int32)
        for m in range(last_blk):
            row = sc[m:m + 1, :]
            beats = (row > sc) | ((row == sc) & (m < n_iota))
            rank = rank + jnp.where(beats & (m < q_blk), 1, 0)
        past = n_iota < q_blk
        chosen = past & (rank < MOBA_TOPK)
        far = chosen & (n_iota < q_blk - 1)

        b_far = jnp.full((nk, w), rb_ref[N_BUCKETS - 1, h] * LOG2E, F32)
        b_hi = b_far.astype(BF16).astype(F32)
        add_hi = jnp.where(past, jnp.where(chosen, jnp.where(far, b_hi, 0.0), NEG), 0.0)
        add_lo = jnp.where(far, b_far - b_hi, 0.0)
        pieces = [add_hi, jnp.zeros((nb - nk, w), F32), add_lo, jnp.zeros((LANES - nb - nk, w), F32)]
        add_t = jnp.concatenate([p for p in pieces if p.shape[0]], axis=0)
        qaug_ref[q_lo:q_lo + w, 0:hd] = q_ref[0, q_lo:q_lo + w, :]
        qaug_ref[q_lo:q_lo + w, hd:hd + LANES] = add_t.T.astype(BF16)

    def logits(i):
        rows = slice(i * blk, (i + 1) * blk)
        kaug_ref[rows, 0:hd] = k_ref[0, rows, :]
        vaug_ref[rows, 0:hd] = v_ref[0, rows, :]
        n_keys = (i + 1) * blk
        s = lax.dot_general(qaug_ref[i * blk:(i + 1) * blk, :], kaug_ref[0:n_keys, :], nt,
                            preferred_element_type=F32)
        if i == 0:
            return s + tab_ref[0, :, blk:2 * blk]
        if i == 1:
            return s + tab_ref[0]
        return jnp.concatenate([s[:, :n_keys - 2 * blk], s[:, n_keys - 2 * blk:] + tab_ref[0]], axis=1)

    select(0)
    s = logits(0)
    for i in range(nb):
        if (i + 2) % SEL_GROUP == 0 and (i + 2) // SEL_GROUP < nb // SEL_GROUP:
            select((i + 2) // SEL_GROUP)
        s_next = logits(i + 1) if i + 1 < nb else None
        s16 = s.astype(BF16)
        p = jnp.exp2(s16 - jnp.max(s16, axis=-1, keepdims=True))
        pv = jnp.dot(p, vaug_ref[0:(i + 1) * blk, :], preferred_element_type=F32)
        o_ref[0, i * blk:(i + 1) * blk, :] = (pv[:, 0:hd] / pv[:, hd:2 * hd]).astype(o_ref.dtype)
        s = s_next


def _moba_attention(proj, kmeans, tables, rel_bias, d_model):
    bsz, seq, _ = proj.shape
    nb = seq // MOBA_BLOCK
    nh = d_model // HEAD_DIM
    assert MAX_DISTANCE <= MOBA_BLOCK + 1 and 2 * nb <= LANES and MOBA_BLOCK & (MOBA_BLOCK - 1) == 0
    assert nb % SEL_GROUP == 0 and nb % SUBLANES == 0
    kern = functools.partial(_attn_kernel, nb=nb)

    def head_cols(tile):
        return pl.BlockSpec((1, seq, HEAD_DIM), lambda b, h: (b, 0, tile * nh + h))

    return pl.pallas_call(
        kern,
        grid=(bsz, nh),
        in_specs=[pl.BlockSpec(memory_space=pltpu.SMEM),
                  head_cols(P_Q), head_cols(P_K), head_cols(P_V),
                  pl.BlockSpec((1,) + kmeans.shape[1:3] + (HEAD_DIM,), lambda b, h: (b, 0, 0, h)),
                  pl.BlockSpec((1, MOBA_BLOCK, 2 * MOBA_BLOCK), lambda b, h: (h, 0, 0))],
        out_specs=head_cols(0),
        out_shape=jax.ShapeDtypeStruct((bsz, seq, d_model), BF16),
        scratch_shapes=[pltpu.VMEM((seq, HEAD_DIM + LANES), BF16),
                        pltpu.VMEM((seq, HEAD_DIM + LANES), BF16),
                        pltpu.VMEM((seq, 2 * HEAD_DIM), BF16),
                        pltpu.VMEM((nb, HEAD_DIM), F32)],
        compiler_params=pltpu.CompilerParams(
            dimension_semantics=("arbitrary", "arbitrary"),
            vmem_limit_bytes=VMEM_LIMIT),
    )(rel_bias, proj, proj, proj, kmeans, tables)


def _out_kernel(attn_ref, ga_ref, zc_ref, ma_ref, mc_ref, x_ref, mod_ref, woa_ref, woc_ref, wout_ref,
                fg_ref, o_ref, *, row_chunk):
    def merged_branches(r):
        rows = slice(r * row_chunk, (r + 1) * row_chunk)
        y_conv = jnp.dot(zc_ref[0, rows, :], woc_ref[...], preferred_element_type=F32)
        z_attn = attn_ref[0, rows, :].astype(F32) * ga_ref[0, rows, :].astype(F32)
        y_attn = jnp.dot(z_attn.astype(BF16), woa_ref[...], preferred_element_type=F32)
        merged = (ma_ref[0, rows, :].astype(F32) * y_attn
                  + mc_ref[0, rows, :].astype(F32) * y_conv)
        return merged.astype(BF16)

    def residual_norm(r, branch):
        rows = slice(r * row_chunk, (r + 1) * row_chunk)
        res = x_ref[0, rows, :] + mod_ref[0, 2:3, :] * branch
        inv = lax.rsqrt(jnp.mean(res * res, axis=-1, keepdims=True) + EPS)
        o_ref[0, rows, :] = res * inv * fg_ref[...]

    n_chunks = x_ref.shape[1] // row_chunk
    merged = merged_branches(0)
    for r in range(n_chunks):
        branch = jnp.dot(merged, wout_ref[...], preferred_element_type=F32)
        merged = merged_branches(r + 1) if r + 1 < n_chunks else None
        residual_norm(r, branch)


def _output_stage(attn, proj, x, mod3, woa, woc, wout, final_g, ts=1024, row_chunk=256):
    bsz, seq, d = x.shape
    assert ts % row_chunk == 0

    def col(k):
        return pl.BlockSpec((1, ts, d), lambda b, s: (b, s, k))

    def whole(shape):
        return pl.BlockSpec(shape, lambda b, s: (0,) * len(shape), pipeline_mode=pl.Buffered(1))

    return pl.pallas_call(
        functools.partial(_out_kernel, row_chunk=row_chunk),
        grid=(bsz, seq // ts),
        in_specs=[col(0), col(P_GA), col(P_ZC), col(P_MA), col(P_MC),
                  col(0), pl.BlockSpec((1, 3, d), lambda b, s: (b, 0, 0)),
                  whole((d, d)), whole((d, d)), whole((d, d)), whole((1, d))],
        out_specs=col(0),
        out_shape=jax.ShapeDtypeStruct((bsz, seq, d), F32),
        compiler_params=pltpu.CompilerParams(
            dimension_semantics=("arbitrary", "arbitrary"),
            vmem_limit_bytes=VMEM_LIMIT),
    )(attn, proj, proj, proj, proj, x, mod3, woa, woc, wout, final_g.reshape(1, d))


def kernel(x, c, norm_g, w_ada, b_ada, w_in, conv_w, w_o_attn, w_o_conv, w_out, rel_bias, final_g):
    bsz, seq, d = x.shape
    depth = norm_g.shape[0]
    assert depth == 1, "the fused output stage applies the final RMSNorm after the single layer"
    assert d == N_HEADS * HEAD_DIM and w_in.shape[2] == N_PROJ * d and seq % MOBA_BLOCK == 0
    tables = _bias_tables(rel_bias)
    mod3 = _modulation(c, w_ada[0], b_ada[0]).reshape(bsz, 3, d)
    proj, woa, woc, wout, kmeans = _in_projection(x, mod3, norm_g[0], conv_w[0], w_in[0],
                                                  (w_o_attn[0], w_o_conv[0], w_out[0]))
    attn = _moba_attention(proj, kmeans, tables, rel_bias, d)
    return _output_stage(attn, proj, x, mod3, woa, woc, wout, final_g)
```

```python
import functools
import math

import jax
import jax.numpy as jnp
from jax import lax
from jax.experimental import pallas as pl
from jax.experimental.pallas import tpu as pltpu

N_HEADS = 8
HEAD_DIM = 128
CONV_K = 3
MOBA_BLOCK = 256
MOBA_TOPK = 3
N_BUCKETS = 32
MAX_DISTANCE = 128
EPS = 1e-6
N_PROJ = 10

LANES = 128
SUBLANES = 8
BF16_SUBLANES = 16
SEL_GROUP = 4
NEG = -1e30
LOG2E = math.log2(math.e)
VMEM_LIMIT = 56 * 1024 * 1024

F32 = jnp.float32
BF16 = jnp.bfloat16


def _sigmoid(v):
    return 0.5 * jnp.tanh(0.5 * v) + 0.5


def _silu(v):
    return v * _sigmoid(v)


def _split_bf16(v, terms):
    parts = []
    for _ in range(terms - 1):
        hi = v.astype(BF16).astype(F32)
        parts.append(hi)
        v = v - hi
    return parts + [v]


def _mod_kernel(rb_ref, c_ref, w_ref, b_ref, o_ref, tab_ref):
    @pl.when(pl.program_id(0) == 0)
    def _():
        _table_kernel(rb_ref, tab_ref)

    bsz = c_ref.shape[0]
    a3 = jnp.concatenate(_split_bf16(_silu(c_ref[...]), 3), axis=0).astype(BF16)
    w_hi, w_lo = _split_bf16(w_ref[...], 2)
    r = (jnp.dot(a3, w_hi.astype(BF16), preferred_element_type=F32)
         + jnp.dot(a3, w_lo.astype(BF16), preferred_element_type=F32))
    o_ref[...] = (r[0:bsz] + r[bsz:2 * bsz]) + r[2 * bsz:3 * bsz] + b_ref[...]


def _modulation_and_tables(c, w_ada, b_ada, rel_bias):
    bsz, d = c.shape
    n = w_ada.shape[1]
    tn = d
    tab_shape = (N_HEADS, MOBA_BLOCK, 2 * MOBA_BLOCK)
    return pl.pallas_call(
        _mod_kernel,
        grid=(n // tn,),
        in_specs=[pl.BlockSpec(memory_space=pltpu.SMEM),
                  pl.BlockSpec((bsz, d), lambda j: (0, 0)),
                  pl.BlockSpec((d, tn), lambda j: (0, j)),
                  pl.BlockSpec((1, tn), lambda j: (0, j))],
        out_specs=[pl.BlockSpec((bsz, tn), lambda j: (0, j)),
                   pl.BlockSpec(tab_shape, lambda j: (0, 0, 0))],
        out_shape=[jax.ShapeDtypeStruct((bsz, n), F32), jax.ShapeDtypeStruct(tab_shape, F32)],
        compiler_params=pltpu.CompilerParams(dimension_semantics=("arbitrary",),
                                             vmem_limit_bytes=VMEM_LIMIT),
    )(rel_bias, c, w_ada, b_ada.reshape(1, n))


W_Q, W_K, W_V, W_GA, W_CB, W_CC, W_CX, W_GC, W_MA, W_MC = range(N_PROJ)
N_FUSED = 7
P_Q, P_K, P_V, P_GA, P_ZC, P_MA, P_MC = range(N_FUSED)
CONV_HALO = 8


def _inproj_kernel(x0_ref, xn_ref, mod0_ref, modn_ref, g_ref, cw_ref, w_hbm, *rest, q_scale,
                   row_chunk, n_late):
    late_in, o_ref, late_out = rest[:n_late], rest[n_late], rest[n_late + 1:2 * n_late + 1]
    kmean_o_ref = rest[2 * n_late + 1]
    h_ref, u_ref, t_ref, w_ref, stage_ref, sem = rest[2 * n_late + 2:]
    for src, dst in zip(late_in, late_out):
        dst[...] = src[...].astype(dst.dtype)
    s = pl.program_id(1)
    step = pl.program_id(0) * pl.num_programs(1) + s
    slot = lax.rem(step, 2)
    ts, d = xn_ref.shape[1], xn_ref.shape[2]
    n_chunks = ts // row_chunk
    g = g_ref[...]

    def normalise(x_ref, mod_ref, dst_slot, r):
        rows = slice(r * row_chunk, (r + 1) * row_chunk)
        xr = x_ref[0, rows, :]
        inv = lax.rsqrt(jnp.mean(xr * xr, axis=-1, keepdims=True) + EPS)
        hr = (xr * inv * g) * (1.0 + mod_ref[0, 1:2, :]) + mod_ref[0, 0:1, :]
        h_ref[dst_slot, rows, :] = hr.astype(BF16)

    depth, _, width = stage_ref.shape

    def weight_chunk_copy(c):
        return pltpu.make_async_copy(w_hbm.at[:, c * width:(c + 1) * width],
                                     stage_ref.at[c % depth], sem.at[c % depth])

    @pl.when(step == 0)
    def _():
        n_w = w_ref.shape[1] // width
        for c in range(min(depth, n_w)):
            weight_chunk_copy(c).start()
        for c in range(n_w):
            weight_chunk_copy(c).wait()
            w_ref[:, c * width:(c + 1) * width] = stage_ref[c % depth].astype(BF16)
            if c + depth < n_w:
                weight_chunk_copy(c + depth).start()
        for r in range(n_chunks):
            normalise(x0_ref, mod0_ref, 0, r)

    def proj(j):
        return jnp.dot(h_ref[slot], w_ref[:, j * d:(j + 1) * d], preferred_element_type=F32)

    def put(k, val):
        o_ref[0, :, k * d:(k + 1) * d] = val.astype(o_ref.dtype)

    @pl.when(s == 0)
    def _():
        u_ref[0:CONV_HALO, :] = jnp.zeros((CONV_HALO, d), F32)

    def keep_cc(acc):
        t_ref[...] = acc

    def conv(acc):
        u_ref[CONV_HALO:CONV_HALO + ts, :] = t_ref[...] * acc
        t_ref[...] = (cw_ref[0:1, :] * u_ref[CONV_HALO - 2:CONV_HALO - 2 + ts, :]
                      + cw_ref[1:2, :] * u_ref[CONV_HALO - 1:CONV_HALO - 1 + ts, :]
                      + cw_ref[2:3, :] * u_ref[CONV_HALO:CONV_HALO + ts, :])
        u_ref[0:CONV_HALO, :] = u_ref[ts:ts + CONV_HALO, :]

    def gate_cb(acc):
        t_ref[...] = t_ref[...] * acc

    def put_keys(acc):
        put(P_K, acc)
        for j in range(ts // MOBA_BLOCK):
            kmean_o_ref[0, 0, j:j + 1, :] = jnp.mean(acc[j * MOBA_BLOCK:(j + 1) * MOBA_BLOCK], axis=0,
                                                     keepdims=True)

    stages = [
        (W_GA, lambda acc: put(P_GA, _silu(acc))),
        (W_MA, lambda acc: put(P_MA, _sigmoid(acc))),
        (W_MC, lambda acc: put(P_MC, _sigmoid(acc))),
        (W_CC, keep_cc),
        (W_CX, conv),
        (W_CB, gate_cb),
        (W_GC, lambda acc: put(P_ZC, t_ref[...] * _silu(acc))),
        (W_Q, lambda acc: put(P_Q, acc * q_scale)),
        (W_K, put_keys),
        (W_V, lambda acc: put(P_V, acc)),
    ]
    norm_after = list(range(len(stages) - 1 - n_chunks, len(stages) - 1))
    acc = proj(stages[0][0])
    for n, (_, epilogue) in enumerate(stages):
        nxt = proj(stages[n + 1][0]) if n + 1 < len(stages) else None
        epilogue(acc)
        if n in norm_after:
            normalise(xn_ref, modn_ref, 1 - slot, norm_after.index(n))
        acc = nxt


def _in_projection(x, mod3, norm_g, conv_w, w_in, late_weights, ts=512, row_chunk=128,
                   w_chunk=256, w_depth=4):
    bsz, seq, d = x.shape
    n = w_in.shape[1]
    assert n % w_chunk == 0 and w_chunk % LANES == 0
    n_out = N_FUSED * d
    n_s = seq // ts
    n_steps = bsz * n_s
    assert ts // row_chunk <= N_PROJ
    slab = d // n_steps
    assert slab * n_steps == d and slab % BF16_SUBLANES == 0
    assert all(w.shape == (d, d) for w in late_weights)
    kern = functools.partial(_inproj_kernel, q_scale=LOG2E * HEAD_DIM ** -0.5, row_chunk=row_chunk,
                             n_late=len(late_weights))

    def next_tile(b, s):
        flat = jnp.minimum(b * n_s + s + 1, n_steps - 1)
        return flat // n_s, flat % n_s

    slab_spec = pl.BlockSpec((slab, d), lambda b, s: (b * n_s + s, 0))
    return pl.pallas_call(
        kern,
        grid=(bsz, n_s),
        in_specs=[pl.BlockSpec((1, ts, d), lambda b, s: (0, 0, 0), pipeline_mode=pl.Buffered(1)),
                  pl.BlockSpec((1, ts, d), lambda b, s: (*next_tile(b, s), 0)),
                  pl.BlockSpec((1, 3, d), lambda b, s: (0, 0, 0)),
                  pl.BlockSpec((1, 3, d), lambda b, s: (next_tile(b, s)[0], 0, 0)),
                  pl.BlockSpec((1, d), lambda b, s: (0, 0)),
                  pl.BlockSpec((CONV_K, d), lambda b, s: (0, 0)),
                  pl.BlockSpec(memory_space=pl.ANY)]
                 + [slab_spec] * len(late_weights),
        out_specs=[pl.BlockSpec((1, ts, n_out), lambda b, s: (b, s, 0))]
                  + [slab_spec] * len(late_weights)
                  + [pl.BlockSpec((1, 1, ts // MOBA_BLOCK, d), lambda b, s: (b, s, 0, 0))],
        out_shape=[jax.ShapeDtypeStruct((bsz, seq, n_out), BF16)]
                  + [jax.ShapeDtypeStruct((d, d), BF16)] * len(late_weights)
                  + [jax.ShapeDtypeStruct((bsz, n_s, ts // MOBA_BLOCK, d), F32)],
        scratch_shapes=[pltpu.VMEM((2, ts, d), BF16), pltpu.VMEM((CONV_HALO + ts, d), F32),
                        pltpu.VMEM((ts, d), F32), pltpu.VMEM((d, n), BF16),
                        pltpu.VMEM((w_depth, d, w_chunk), F32),
                        pltpu.SemaphoreType.DMA((w_depth,))],
        compiler_params=pltpu.CompilerParams(
            dimension_semantics=("arbitrary", "arbitrary"),
            vmem_limit_bytes=VMEM_LIMIT),
    )(x, x, mod3, mod3, norm_g.reshape(1, d), conv_w, w_in, *late_weights)


def _t5_bucket(dist):
    n = jnp.maximum(dist, 0)
    max_exact = N_BUCKETS // 2
    nf = jnp.maximum(n, 1).astype(F32)
    large = max_exact + (jnp.log(nf / max_exact) / math.log(MAX_DISTANCE / max_exact)
                         * (N_BUCKETS - max_exact)).astype(jnp.int32)
    large = jnp.minimum(large, N_BUCKETS - 1)
    return jnp.where(n < max_exact, n, large)


def _table_kernel(rb_ref, o_ref):
    n_heads, blk, width = o_ref.shape
    u = lax.broadcasted_iota(jnp.int32, (SUBLANES, width), 1)
    for t in range(2):
        dist = jnp.where(u < blk, t * blk - u, t * blk + width - u)
        bucket = _t5_bucket(dist)
        for h in range(n_heads):
            g = jnp.zeros((SUBLANES, width), F32)
            for b in range(N_BUCKETS):
                g = jnp.where(bucket == b, rb_ref[b, h] * LOG2E, g)
            g = jnp.where(dist >= 0, g, NEG)
            skewed = pltpu.roll(jnp.broadcast_to(g[0:1, :], (blk, width)), 0, 1,
                                stride=1, stride_axis=0)
            o_ref[h, :, (1 - t) * blk:(2 - t) * blk] = skewed[:, 0:blk]


def _attn_kernel(rb_ref, q_ref, k_ref, v_ref, km_ref, tab_ref, o_ref, qaug_ref, kaug_ref, vaug_ref,
                 kmean_ref, *, nb):
    h = pl.program_id(1)
    blk = MOBA_BLOCK
    hd = HEAD_DIM
    seq = nb * blk
    nt = (((1,), (1,)), ((), ()))

    @pl.when((pl.program_id(0) == 0) & (h == 0))
    def _():
        lane = lax.broadcasted_iota(jnp.int32, (blk, LANES), 1)
        for n in range(nb):
            onehot = jnp.where((lane == n) | (lane == nb + n), 1.0, 0.0)
            kaug_ref[n * blk:(n + 1) * blk, hd:hd + LANES] = onehot.astype(BF16)
        vaug_ref[:, hd:2 * hd] = jnp.ones((seq, hd), BF16)

    per_tile = km_ref.shape[2]
    for t in range(nb // per_tile):
        kmean_ref[t * per_tile:(t + 1) * per_tile, :] = km_ref[0, t]

    def select(g):
        q_lo = g * SEL_GROUP * blk
        w = SEL_GROUP * blk
        last_blk = (g + 1) * SEL_GROUP - 1
        nk = min(nb, -(-(last_blk + 1) // SUBLANES) * SUBLANES)

        km3 = jnp.concatenate(_split_bf16(kmean_ref[0:nk, :], 3), axis=0).astype(BF16)
        sc3 = lax.dot_general(km3, q_ref[0, q_lo:q_lo + w, :], nt, preferred_element_type=F32)
        sc = (sc3[0:nk] + sc3[nk:2 * nk]) + sc3[2 * nk:3 * nk]

        n_iota = lax.broadcasted_iota(jnp.int32, (nk, w), 0)
        q_blk = g * SEL_GROUP + lax.shift_right_logical(
            lax.broadcasted_iota(jnp.int32, (nk, w), 1), int(math.log2(blk)))
        rank = jnp.zeros((nk, w), jnp.int32)
        for m in range(last_blk):
            row = sc[m:m + 1, :]
            beats = (row > sc) | ((row == sc) & (m < n_iota))
            rank = rank + jnp.where(beats & (m < q_blk), 1, 0)
        past = n_iota < q_blk
        chosen = past & (rank < MOBA_TOPK)
        far = chosen & (n_iota < q_blk - 1)

        b_far = jnp.full((nk, w), rb_ref[N_BUCKETS - 1, h] * LOG2E, F32)
        b_hi = b_far.astype(BF16).astype(F32)
        add_hi = jnp.where(past, jnp.where(chosen, jnp.where(far, b_hi, 0.0), NEG), 0.0)
        add_lo = jnp.where(far, b_far - b_hi, 0.0)
        pieces = [add_hi, jnp.zeros((nb - nk, w), F32), add_lo, jnp.zeros((LANES - nb - nk, w), F32)]
        add_t = jnp.concatenate([p for p in pieces if p.shape[0]], axis=0)
        qaug_ref[q_lo:q_lo + w, 0:hd] = q_ref[0, q_lo:q_lo + w, :]
        qaug_ref[q_lo:q_lo + w, hd:hd + LANES] = add_t.T.astype(BF16)

    def logits(i):
        rows = slice(i * blk, (i + 1) * blk)
        kaug_ref[rows, 0:hd] = k_ref[0, rows, :]
        vaug_ref[rows, 0:hd] = v_ref[0, rows, :]
        n_keys = (i + 1) * blk
        s = lax.dot_general(qaug_ref[i * blk:(i + 1) * blk, :], kaug_ref[0:n_keys, :], nt,
                            preferred_element_type=F32)
        if i == 0:
            return s + tab_ref[0, :, blk:2 * blk]
        if i == 1:
            return s + tab_ref[0]
        return jnp.concatenate([s[:, :n_keys - 2 * blk], s[:, n_keys - 2 * blk:] + tab_ref[0]], axis=1)

    for g in range(nb // SEL_GROUP):
        select(g)
    s = logits(0)
    for i in range(nb):
        s_next = logits(i + 1) if i + 1 < nb else None
        s16 = s.astype(BF16)
        p = jnp.exp2(s16 - jnp.max(s16, axis=-1, keepdims=True))
        pv = jnp.dot(p, vaug_ref[0:(i + 1) * blk, :], preferred_element_type=F32)
        o_ref[0, i * blk:(i + 1) * blk, :] = (pv[:, 0:hd] / pv[:, hd:2 * hd]).astype(o_ref.dtype)
        s = s_next


def _moba_attention(proj, kmeans, tables, rel_bias, d_model):
    bsz, seq, _ = proj.shape
    nb = seq // MOBA_BLOCK
    nh = d_model // HEAD_DIM
    assert MAX_DISTANCE <= MOBA_BLOCK + 1 and 2 * nb <= LANES and MOBA_BLOCK & (MOBA_BLOCK - 1) == 0
    assert nb % SEL_GROUP == 0 and nb % SUBLANES == 0
    kern = functools.partial(_attn_kernel, nb=nb)

    def head_cols(tile):
        return pl.BlockSpec((1, seq, HEAD_DIM), lambda b, h: (b, 0, tile * nh + h))

    return pl.pallas_call(
        kern,
        grid=(bsz, nh),
        in_specs=[pl.BlockSpec(memory_space=pltpu.SMEM),
                  head_cols(P_Q), head_cols(P_K), head_cols(P_V),
                  pl.BlockSpec((1,) + kmeans.shape[1:3] + (HEAD_DIM,), lambda b, h: (b, 0, 0, h)),
                  pl.BlockSpec((1, MOBA_BLOCK, 2 * MOBA_BLOCK), lambda b, h: (h, 0, 0))],
        out_specs=head_cols(0),
        out_shape=jax.ShapeDtypeStruct((bsz, seq, d_model), BF16),
        scratch_shapes=[pltpu.VMEM((seq, HEAD_DIM + LANES), BF16),
                        pltpu.VMEM((seq, HEAD_DIM + LANES), BF16),
                        pltpu.VMEM((seq, 2 * HEAD_DIM), BF16),
                        pltpu.VMEM((nb, HEAD_DIM), F32)],
        compiler_params=pltpu.CompilerParams(
            dimension_semantics=("arbitrary", "arbitrary"),
            vmem_limit_bytes=VMEM_LIMIT),
    )(rel_bias, proj, proj, proj, kmeans, tables)


def _out_kernel(attn_ref, ga_ref, zc_ref, ma_ref, mc_ref, x_ref, mod_ref, woa_ref, woc_ref, wout_ref,
                fg_ref, o_ref, *, row_chunk):
    def merged_branches(r):
        rows = slice(r * row_chunk, (r + 1) * row_chunk)
        y_conv = jnp.dot(zc_ref[0, rows, :], woc_ref[...], preferred_element_type=F32)
        z_attn = attn_ref[0, rows, :].astype(F32) * ga_ref[0, rows, :].astype(F32)
        y_attn = jnp.dot(z_attn.astype(BF16), woa_ref[...], preferred_element_type=F32)
        merged = (ma_ref[0, rows, :].astype(F32) * y_attn
                  + mc_ref[0, rows, :].astype(F32) * y_conv)
        return merged.astype(BF16)

    def residual_norm(r, branch):
        rows = slice(r * row_chunk, (r + 1) * row_chunk)
        res = x_ref[0, rows, :] + mod_ref[0, 2:3, :] * branch
        inv = lax.rsqrt(jnp.mean(res * res, axis=-1, keepdims=True) + EPS)
        o_ref[0, rows, :] = res * inv * fg_ref[...]

    n_chunks = x_ref.shape[1] // row_chunk
    merged = merged_branches(0)
    for r in range(n_chunks):
        branch = jnp.dot(merged, wout_ref[...], preferred_element_type=F32)
        merged = merged_branches(r + 1) if r + 1 < n_chunks else None
        residual_norm(r, branch)


def _output_stage(attn, proj, x, mod3, woa, woc, wout, final_g, ts=1024, row_chunk=256):
    bsz, seq, d = x.shape
    assert ts % row_chunk == 0

    def col(k):
        return pl.BlockSpec((1, ts, d), lambda b, s: (b, s, k))

    def whole(shape):
        return pl.BlockSpec(shape, lambda b, s: (0,) * len(shape), pipeline_mode=pl.Buffered(1))

    return pl.pallas_call(
        functools.partial(_out_kernel, row_chunk=row_chunk),
        grid=(bsz, seq // ts),
        in_specs=[col(0), col(P_GA), col(P_ZC), col(P_MA), col(P_MC),
                  col(0), pl.BlockSpec((1, 3, d), lambda b, s: (b, 0, 0)),
                  whole((d, d)), whole((d, d)), whole((d, d)), whole((1, d))],
        out_specs=col(0),
        out_shape=jax.ShapeDtypeStruct((bsz, seq, d), F32),
        compiler_params=pltpu.CompilerParams(
            dimension_semantics=("arbitrary", "arbitrary"),
            vmem_limit_bytes=VMEM_LIMIT),
    )(attn, proj, proj, proj, proj, x, mod3, woa, woc, wout, final_g.reshape(1, d))


def kernel(x, c, norm_g, w_ada, b_ada, w_in, conv_w, w_o_attn, w_o_conv, w_out, rel_bias, final_g):
    bsz, seq, d = x.shape
    depth = norm_g.shape[0]
    assert depth == 1, "the fused output stage applies the final RMSNorm after the single layer"
    assert d == N_HEADS * HEAD_DIM and w_in.shape[2] == N_PROJ * d and seq % MOBA_BLOCK == 0
    mod, tables = _modulation_and_tables(c, w_ada[0], b_ada[0], rel_bias)
    mod3 = mod.reshape(bsz, 3, d)
    proj, woa, woc, wout, kmeans = _in_projection(x, mod3, norm_g[0], conv_w[0], w_in[0],
                                                  (w_o_attn[0], w_o_conv[0], w_out[0]))
    attn = _moba_attention(proj, kmeans, tables, rel_bias, d)
    return _output_stage(attn, proj, x, mod3, woa, woc, wout, final_g)
```
